```python
import math
import jax
import jax.numpy as jnp
from jax import lax
import numpy as np

D_MODEL = 1024
BATCH = 1
SEQ = 16384
DEPTH = 4

GRID_W = 64
CTX_LEN = 256
HEAD_DIM = 64
SCALE = HEAD_DIM ** -0.5
ROPE_AXIS_DIM = HEAD_DIM // 2
ROPE_THETA = 10000.0
Q_BLOCK = 128
DA_HEADS = 4
DA_QK = DA_HEADS * 2 * HEAD_DIM
DA_V = DA_HEADS * 2 * HEAD_DIM
GQA_HEADS = 4
GQA_KV_HEADS = 2
GQA_GROUP = GQA_HEADS // GQA_KV_HEADS
GQA_Q = GQA_HEADS * HEAD_DIM
GQA_KV = GQA_KV_HEADS * HEAD_DIM
LRU_WIDTH = 256
LRU_BLOCKS = 4
LRU_BLOCK = LRU_WIDTH // LRU_BLOCKS
CONV_W = 4
LRU_C = 8.0
MIX_WIDTH = DA_V + GQA_Q + LRU_WIDTH
IN_SPLITS = (DA_QK, DA_QK, DA_V, GQA_Q, GQA_KV, GQA_KV, LRU_WIDTH, LRU_WIDTH)
IN_WIDTH = 2560
D_FF = 2816
N_MOD = 9
EPS = 1e-6

kernel_name = 'hymba_style_diffattn_gqa_rglru_macaron_dit'


def rms_norm(x, g):
    xf = x.astype(jnp.float32)
    y = xf * lax.rsqrt(jnp.mean(xf * xf, axis=-1, keepdims=True) + EPS)
    return y.astype(x.dtype) * g


def modulate(x, shift, scale):
    return x * (1.0 + scale) + shift


def swiglu(x, w13, w2):
    a, b = jnp.split(x @ w13, 2, axis=-1)
    return (jax.nn.silu(a) * b) @ w2


def axial_rope_tables(n):
    rows = n // GRID_W
    row = jnp.repeat(jnp.arange(rows, dtype=jnp.float32), GRID_W)
    col = jnp.tile(jnp.arange(GRID_W, dtype=jnp.float32), rows)
    half = ROPE_AXIS_DIM // 2
    inv = ROPE_THETA ** (-jnp.arange(half, dtype=jnp.float32) / half)
    ang = jnp.stack([row[:, None] * inv, col[:, None] * inv], axis=0)
    return jnp.cos(ang), jnp.sin(ang)


def apply_axial_rope(x, cos, sin):
    shape = (2, 1, x.shape[1]) + (1,) * (x.ndim - 3) + (cos.shape[-1],)
    cos = cos.reshape(shape).astype(x.dtype)
    sin = sin.reshape(shape).astype(x.dtype)
    xa = jnp.stack(jnp.split(x, 2, axis=-1), axis=0)
    x1, x2 = jnp.split(xa, 2, axis=-1)
    out = jnp.concatenate([x1 * cos - x2 * sin, x1 * sin + x2 * cos], axis=-1)
    return jnp.concatenate([out[0], out[1]], axis=-1)


def over_query_blocks(fn, qs):
    b, n = qs[0].shape[:2]
    nb = n // Q_BLOCK
    blocks = tuple(jnp.moveaxis(q.reshape((b, nb, Q_BLOCK) + q.shape[2:]), 1, 0) for q in qs)
    out = lax.map(lambda qb: fn(*qb), blocks)
    return jnp.moveaxis(out, 0, 1).reshape((b, n) + out.shape[3:])


def diff_attn_core(q1, q2, k1, k2, v, lam, subln_g, lam_init):
    s1 = jnp.einsum('bqhd,bkhd->bhqk', q1, k1, preferred_element_type=jnp.float32) * SCALE
    s2 = jnp.einsum('bqhd,bkhd->bhqk', q2, k2, preferred_element_type=jnp.float32) * SCALE
    p = jax.nn.softmax(s1, axis=-1) - lam * jax.nn.softmax(s2, axis=-1)
    o = jnp.einsum('bhqk,bkhe->bqhe', p.astype(v.dtype), v)
    return rms_norm(o, subln_g) * (1.0 - lam_init)


def gqa_core(q, k, v):
    s = jnp.einsum('bqngd,bknd->bngqk', q, k, preferred_element_type=jnp.float32) * SCALE
    p = jax.nn.softmax(s, axis=-1)
    return jnp.einsum('bngqk,bknd->bqngd', p.astype(v.dtype), v)


def depthwise_conv(x, w, b):
    y = lax.conv_general_dilated(x, w[:, None, :], window_strides=(1,), padding=[(1, 2)],
                                 dimension_numbers=('NWC', 'WIO', 'NWC'),
                                 feature_group_count=x.shape[-1])
    return y + b


def block_diag(x, w):
    b, n, _ = x.shape
    y = jnp.einsum('bnkc,kcd->bnkd', x.reshape(b, n, LRU_BLOCKS, LRU_BLOCK), w)
    return y.reshape(b, n, LRU_WIDTH)


def rglru_coeffs(x, wa, ba, wi, bi, lam):
    r = jax.nn.sigmoid(block_diag(x, wa).astype(jnp.float32) + ba.astype(jnp.float32))
    i = jax.nn.sigmoid(block_diag(x, wi).astype(jnp.float32) + bi.astype(jnp.float32))
    log_a = -LRU_C * r * jax.nn.softplus(-lam.astype(jnp.float32))
    a = jnp.exp(log_a)
    u = jnp.sqrt(-jnp.expm1(2.0 * log_a)) * (i * x.astype(jnp.float32))
    return a, u


def _lin_combine(left, right):
    a1, b1 = left
    a2, b2 = right
    return a1 * a2, a2 * b1 + b2


def linear_scan(a, u, h0, reverse):
    if reverse:
        a = jnp.flip(a, axis=1)
        u = jnp.flip(u, axis=1)
    u = u.at[:, 0].add(a[:, 0] * h0)
    _, h = lax.associative_scan(_lin_combine, (a, u), axis=1)
    return jnp.flip(h, axis=1) if reverse else h


def hybrid_mixer(z_lat, z_ctx, w_in, w_out, da_lam, da_subln_g, lam_init, qk_norm_g,
                 conv_w, conv_b, wa, ba, wi, bi, lru_lambda, cos, sin, ctx_out):
    b, n, _ = z_lat.shape
    m = z_ctx.shape[1]
    cuts = [int(v) for v in np.cumsum(IN_SPLITS)[:-1]]
    lat = jnp.split(z_lat @ w_in, cuts, axis=-1)
    cx = jnp.split(z_ctx @ w_in, cuts, axis=-1)

    def da_split(qa, ka, va, length):
        q = qa.reshape(b, length, DA_HEADS, 2, HEAD_DIM)
        k = ka.reshape(b, length, DA_HEADS, 2, HEAD_DIM)
        v = va.reshape(b, length, DA_HEADS, 2 * HEAD_DIM)
        return q[..., 0, :], q[..., 1, :], k[..., 0, :], k[..., 1, :], v

    lq1, lq2, lk1, lk2, lv = da_split(lat[0], lat[1], lat[2], n)
    cq1, cq2, ck1, ck2, cv = da_split(cx[0], cx[1], cx[2], m)
    lq1, lq2, lk1, lk2 = [apply_axial_rope(t, cos, sin) for t in (lq1, lq2, lk1, lk2)]
    lf = da_lam.astype(jnp.float32)
    lam = jnp.exp(jnp.sum(lf[0] * lf[1])) - jnp.exp(jnp.sum(lf[2] * lf[3])) + lam_init
    k1_all = jnp.concatenate([ck1, lk1], axis=1)
    k2_all = jnp.concatenate([ck2, lk2], axis=1)
    v_all = jnp.concatenate([cv, lv], axis=1)
    a_lat = over_query_blocks(
        lambda q1, q2: diff_attn_core(q1, q2, k1_all, k2_all, v_all, lam, da_subln_g, lam_init),
        (lq1, lq2))

    def gqa_split(qa, ka, va, length):
        q = rms_norm(qa.reshape(b, length, GQA_HEADS, HEAD_DIM), qk_norm_g[0])
        k = rms_norm(ka.reshape(b, length, GQA_KV_HEADS, HEAD_DIM), qk_norm_g[1])
        return q, k, va.reshape(b, length, GQA_KV_HEADS, HEAD_DIM)

    gq, gk, gv = gqa_split(lat[3], lat[4], lat[5], n)
    cgq, cgk, cgv = gqa_split(cx[3], cx[4], cx[5], m)
    gq = apply_axial_rope(gq, cos, sin).reshape(b, n, GQA_KV_HEADS, GQA_GROUP, HEAD_DIM)
    gk = apply_axial_rope(gk, cos, sin)
    gk_all = jnp.concatenate([cgk, gk], axis=1)
    gv_all = jnp.concatenate([cgv, gv], axis=1)
    g_lat = over_query_blocks(lambda q: gqa_core(q, gk_all, gv_all), (gq,))

    xl = depthwise_conv(lat[6], conv_w, conv_b)
    xc = depthwise_conv(cx[6], conv_w, conv_b)
    h_lat_dirs = []
    h_ctx_dirs = []
    for d, rev in ((0, False), (1, True)):
        a_c, u_c = rglru_coeffs(xc, wa[d], ba[d], wi[d], bi[d], lru_lambda[d])
        h_c = linear_scan(a_c, u_c, jnp.zeros((b, LRU_WIDTH), jnp.float32), rev)
        a_l, u_l = rglru_coeffs(xl, wa[d], ba[d], wi[d], bi[d], lru_lambda[d])
        h_l = linear_scan(a_l, u_l, h_c[:, 0] if rev else h_c[:, -1], rev)
        h_lat_dirs.append(h_l)
        h_ctx_dirs.append(h_c)
    r_lat = (h_lat_dirs[0] + h_lat_dirs[1]).astype(z_lat.dtype) * jax.nn.gelu(lat[7])

    y_lat = jnp.concatenate([a_lat.reshape(b, n, DA_V), g_lat.reshape(b, n, GQA_Q), r_lat],
                            axis=-1) @ w_out
    if not ctx_out:
        return y_lat, None
    a_ctx = diff_attn_core(cq1, cq2, ck1, ck2, cv, lam, da_subln_g, lam_init)
    g_ctx = gqa_core(cgq.reshape(b, m, GQA_KV_HEADS, GQA_GROUP, HEAD_DIM), cgk, cgv)
    r_ctx = (h_ctx_dirs[0] + h_ctx_dirs[1]).astype(z_ctx.dtype) * jax.nn.gelu(cx[7])
    y_ctx = jnp.concatenate([a_ctx.reshape(b, m, DA_V), g_ctx.reshape(b, m, GQA_Q), r_ctx],
                            axis=-1) @ w_out
    return y_lat, y_ctx


def setup_inputs(seed: int = 0) -> dict:
    key = jax.random.key(seed)
    ks = iter(jax.random.split(key, 32))
    f32 = jnp.float32
    L, D = DEPTH, D_MODEL

    def nrm(shape, scale):
        return jax.random.normal(next(ks), shape, f32) * scale

    u = jax.random.uniform(next(ks), (L, 2, LRU_WIDTH), f32, 0.9, 0.999)
    sig = u ** (1.0 / LRU_C)
    lru_lambda = jnp.log(sig) - jnp.log1p(-sig)
    return {
        'x': nrm((BATCH, SEQ, D), 1.0),
        'c': nrm((BATCH, D), 1.0),
        'ctx': nrm((BATCH, CTX_LEN, D), 1.0),
        'c_ctx': nrm((D,), 1.0),
        'ada_w': nrm((L, D, N_MOD * D), 0.5 * D ** -0.5),
        'ada_b': nrm((L, N_MOD * D), 0.02),
        'norm_g': 1.0 + nrm((L, 3, D), 0.01),
        'ffn1_w13': nrm((L, D, 2 * D_FF), D ** -0.5),
        'ffn1_w2': nrm((L, D_FF, D), D_FF ** -0.5),
        'ffn2_w13': nrm((L, D, 2 * D_FF), D ** -0.5),
        'ffn2_w2': nrm((L, D_FF, D), D_FF ** -0.5),
        'w_in': nrm((L, D, IN_WIDTH), D ** -0.5),
        'w_out': nrm((L, MIX_WIDTH, D), MIX_WIDTH ** -0.5),
        'da_lam': nrm((L, 4, HEAD_DIM), 0.1),
        'da_subln_g': 1.0 + nrm((L, 2 * HEAD_DIM), 0.01),
        'qk_norm_g': 1.0 + nrm((L, 2, HEAD_DIM), 0.01),
        'lru_conv_w': nrm((L, CONV_W, LRU_WIDTH), CONV_W ** -0.5),
        'lru_conv_b': nrm((L, LRU_WIDTH), 0.02),
        'lru_wa': nrm((L, 2, LRU_BLOCKS, LRU_BLOCK, LRU_BLOCK), LRU_BLOCK ** -0.5),
        'lru_ba': nrm((L, 2, LRU_WIDTH), 0.02),
        'lru_wi': nrm((L, 2, LRU_BLOCKS, LRU_BLOCK, LRU_BLOCK), LRU_BLOCK ** -0.5),
        'lru_bi': nrm((L, 2, LRU_WIDTH), 0.02),
        'lru_lambda': lru_lambda,
        'final_g': 1.0 + nrm((D,), 0.01),
    }


def reference(x, c, ctx, c_ctx, ada_w, ada_b, norm_g, ffn1_w13, ffn1_w2, ffn2_w13, ffn2_w2,
              w_in, w_out, da_lam, da_subln_g, qk_norm_g, lru_conv_w, lru_conv_b,
              lru_wa, lru_ba, lru_wi, lru_bi, lru_lambda, final_g):
    b, n, d_model = x.shape
    cos, sin = axial_rope_tables(n)

    def ffn_step(s, mods, w13, w2, g, i0):
        y = swiglu(modulate(rms_norm(s, g), mods[i0], mods[i0 + 1]), w13, w2)
        return s + 0.5 * mods[i0 + 2] * y

    h, hc = x, ctx
    for l in range(DEPTH):
        last = l == DEPTH - 1
        lam_init = 0.8 - 0.6 * math.exp(-0.3 * l)
        m_lat = (jax.nn.silu(c) @ ada_w[l] + ada_b[l]).reshape(b, N_MOD, d_model)
        m_ctx = (jax.nn.silu(c_ctx) @ ada_w[l] + ada_b[l]).reshape(N_MOD, d_model)
        ml = [m_lat[:, k, None, :] for k in range(N_MOD)]
        mc = [m_ctx[k] for k in range(N_MOD)]

        h = ffn_step(h, ml, ffn1_w13[l], ffn1_w2[l], norm_g[l, 0], 0)
        hc = ffn_step(hc, mc, ffn1_w13[l], ffn1_w2[l], norm_g[l, 0], 0)

        zl = modulate(rms_norm(h, norm_g[l, 1]), ml[3], ml[4])
        zc = modulate(rms_norm(hc, norm_g[l, 1]), mc[3], mc[4])
        yl, yc = hybrid_mixer(zl, zc, w_in[l], w_out[l], da_lam[l], da_subln_g[l], lam_init,
                              qk_norm_g[l], lru_conv_w[l], lru_conv_b[l], lru_wa[l], lru_ba[l],
                              lru_wi[l], lru_bi[l], lru_lambda[l], cos, sin, not last)
        h = h + ml[5] * yl

        h = ffn_step(h, ml, ffn2_w13[l], ffn2_w2[l], norm_g[l, 2], 6)
        if not last:
            hc = hc + mc[5] * yc
            hc = ffn_step(hc, mc, ffn2_w13[l], ffn2_w2[l], norm_g[l, 2], 6)
    return rms_norm(h, final_g)
```

```python
import functools
import math

import jax
import jax.numpy as jnp
from jax import lax
from jax.experimental import pallas as pl
from jax.experimental.pallas import tpu as pltpu

D_MODEL = 1024
SEQ = 16384
CTX_LEN = 256
T_ALL = SEQ + CTX_LEN
DEPTH = 4
GRID_W = 64
HEAD_DIM = 64
SCALE = HEAD_DIM ** -0.5
LOG2E = math.log2(math.e)
ROPE_THETA = 10000.0
DA_HEADS = 4
GQA_KV_HEADS = 2
LRU_WIDTH = 256
LRU_BLOCKS = 4
LRU_BLOCK = LRU_WIDTH // LRU_BLOCKS
LRU_C = 8.0
IN_WIDTH = 2560
D_FF = 2816
N_MOD = 9
EPS = 1e-6

LANES = 128
MXU_DIM = 256

TM = 256
N_TILES = T_ALL // TM
N_LAT_TILES = SEQ // TM
FF_CHUNK = MXU_DIM
N_FF_CHUNKS = D_FF // FF_CHUNK
TQ = 256
TK = 512
N_K_CHUNKS = SEQ // TK
NEG_BIG = -1e30
VMEM_LIMIT = 48 * 1024 * 1024

F32 = jnp.float32
BF16 = jnp.bfloat16


def _params(n_axes):
    return pltpu.CompilerParams(dimension_semantics=("arbitrary",) * n_axes,
                                vmem_limit_bytes=VMEM_LIMIT)


def _resident(shape):
    zeros = (0,) * len(shape)
    return pl.BlockSpec(shape, lambda *_: zeros, pipeline_mode=pl.Buffered(1))


def _sigmoid(x):
    return 1.0 / (1.0 + jnp.exp(-x))


def _norm_mod(x, g, shift, scale):
    ms = jnp.mean(x * x, axis=-1, keepdims=True)
    y = (x * lax.rsqrt(ms + EPS)) * g
    return y * (1.0 + scale) + shift


ADA_COLS = 1152


def _ada_kernel(c_ref, w_ref, b_ref, o_ref):
    c = c_ref[...]
    s = (c * _sigmoid(c)).astype(BF16)
    o_ref[0] = jnp.dot(s, w_ref[0].astype(BF16), preferred_element_type=F32) + b_ref[0]


def _ada_mods(cc, ada_w, ada_b):
    width = N_MOD * D_MODEL
    return pl.pallas_call(
        _ada_kernel,
        grid=(DEPTH, width // ADA_COLS),
        in_specs=[
            pl.BlockSpec((8, D_MODEL), lambda l, j: (0, 0)),
            pl.BlockSpec((1, D_MODEL, ADA_COLS), lambda l, j: (l, 0, j)),
            pl.BlockSpec((1, 1, ADA_COLS), lambda l, j: (l, 0, j)),
        ],
        out_specs=pl.BlockSpec((1, 8, ADA_COLS), lambda l, j: (l, 0, j)),
        out_shape=jax.ShapeDtypeStruct((DEPTH, 8, width), F32),
        compiler_params=_params(2),
        name="ada_mods",
    )(cc, ada_w, ada_b.reshape(DEPTH, 1, width))


def _ffn_kernel(i0, h_ref, mod_ref, g_ref, w1_ref, w3_ref, w2_ref, o_ref, acc_ref):
    x = h_ref[...]
    mods = mod_ref[0]
    z = _norm_mod(x, g_ref[...], mods[i0:i0 + 1], mods[i0 + 1:i0 + 2]).astype(BF16)
    acc_ref[...] = jnp.zeros_like(acc_ref)

    def chunk(c, carry):
        a = jnp.dot(z, w1_ref[c], preferred_element_type=F32)
        b = jnp.dot(z, w3_ref[c], preferred_element_type=F32)
        gated = (a * _sigmoid(a) * b).astype(BF16)
        acc_ref[...] += jnp.dot(gated, w2_ref[c], preferred_element_type=F32)
        return carry

    lax.fori_loop(0, N_FF_CHUNKS, chunk, 0)
    o_ref[...] = x + (0.5 * mods[i0 + 2:i0 + 3]) * acc_ref[...]


def _ffn(h, mods, g, w1c, w3c, w2c, i0, n_tiles):
    return pl.pallas_call(
        functools.partial(_ffn_kernel, i0),
        grid=(n_tiles,),
        in_specs=[
            pl.BlockSpec((TM, D_MODEL), lambda i: (i, 0)),
            pl.BlockSpec((1, N_MOD, D_MODEL), lambda i: (i // N_LAT_TILES, 0, 0)),
            pl.BlockSpec((1, D_MODEL), lambda i: (0, 0)),
            _resident((N_FF_CHUNKS, D_MODEL, FF_CHUNK)),
            _resident((N_FF_CHUNKS, D_MODEL, FF_CHUNK)),
            _resident((N_FF_CHUNKS, FF_CHUNK, D_MODEL)),
        ],
        out_specs=pl.BlockSpec((TM, D_MODEL), lambda i: (i, 0)),
        out_shape=jax.ShapeDtypeStruct((n_tiles * TM, D_MODEL), F32),
        scratch_shapes=[pltpu.VMEM((TM, D_MODEL), F32)],
        compiler_params=_params(1),
        name="ffn",
    )(h, mods, g, w1c, w3c, w2c)


def _swap16(x):
    lane = lax.broadcasted_iota(jnp.int32, x.shape, 1)
    return jnp.where(lane % 32 < 16, pltpu.roll(x, LANES - 16, 1), pltpu.roll(x, 16, 1))


def _rope(x, cos, sin):
    return x * cos + _swap16(x) * sin


def _head_mean_sq(x):
    sq = x * x
    hi = sq.astype(BF16)
    lo = (sq - hi.astype(F32)).astype(BF16)
    r = lax.broadcasted_iota(jnp.int32, (LANES, LANES), 0) // HEAD_DIM
    c = lax.broadcasted_iota(jnp.int32, (LANES, LANES), 1) // HEAD_DIM
    ones = jnp.where(r == c, 1.0, 0.0).astype(BF16)
    tot = (jnp.dot(hi, ones, preferred_element_type=F32)
           + jnp.dot(lo, ones, preferred_element_type=F32))
    return tot * (1.0 / HEAD_DIM)


def _inproj_kernel(h_ref, mod_ref, g_ref, w_ref, cos_ref, sin_ref, qkg_ref,
                   qt_da_ref, k_da_ref, vt_da_ref, qt_g_ref, k_g_ref, vt_g_ref, xb_ref, gb_ref):
    mods = mod_ref[0]
    z = _norm_mod(h_ref[...], g_ref[...], mods[3:4], mods[4:5]).astype(BF16)
    proj = jnp.dot(z, w_ref[...], preferred_element_type=F32)
    cos = cos_ref[...]
    sin = sin_ref[...]
    q_scale = SCALE * LOG2E

    def slab(j):
        return proj[:, j * LANES:(j + 1) * LANES]

    for h in range(DA_HEADS):
        q = _rope(slab(h), cos, sin) * q_scale
        qt_da_ref[h] = q.T.astype(BF16)
        k_da_ref[:, h * LANES:(h + 1) * LANES] = _rope(slab(4 + h), cos, sin).astype(BF16)
        vt_da_ref[h] = slab(8 + h).T.astype(BF16)

    gq_gain = qkg_ref[0:1, :]
    gk_gain = qkg_ref[1:2, :]
    for g in range(GQA_KV_HEADS):
        q = slab(12 + g)
        q = (q * lax.rsqrt(_head_mean_sq(q) + EPS)) * gq_gain
        qt_g_ref[g] = (_rope(q, cos, sin) * q_scale).T.astype(BF16)
    k = slab(14)
    k = (k * lax.rsqrt(_head_mean_sq(k) + EPS)) * gk_gain
    k_g_ref[...] = _rope(k, cos, sin).astype(BF16)
    vt_g_ref[...] = slab(15).T.astype(BF16)
    xb_ref[...] = proj[:, 2048:2304]
    gb_ref[...] = proj[:, 2304:2560]


def _inproj(h, mods, g, w_in, cos_t, sin_t, qk_gain):
    row_tile = lambda width: pl.BlockSpec((TM, width), lambda i: (i, 0))
    return pl.pallas_call(
        _inproj_kernel,
        grid=(N_TILES,),
        in_specs=[
            row_tile(D_MODEL),
            pl.BlockSpec((1, N_MOD, D_MODEL), lambda i: (i // N_LAT_TILES, 0, 0)),
            pl.BlockSpec((1, D_MODEL), lambda i: (0, 0)),
            _resident((D_MODEL, IN_WIDTH)),
            row_tile(LANES),
            row_tile(LANES),
            pl.BlockSpec((2, LANES), lambda i: (0, 0)),
        ],
        out_specs=[
            pl.BlockSpec((DA_HEADS, LANES, TM), lambda i: (0, 0, i)),
            row_tile(DA_HEADS * LANES),
            pl.BlockSpec((DA_HEADS, LANES, TM), lambda i: (0, 0, i)),
            pl.BlockSpec((GQA_KV_HEADS, LANES, TM), lambda i: (0, 0, i)),
            row_tile(LANES),
            pl.BlockSpec((LANES, TM), lambda i: (0, i)),
            row_tile(LRU_WIDTH),
            row_tile(LRU_WIDTH),
        ],
        out_shape=[
            jax.ShapeDtypeStruct((DA_HEADS, LANES, T_ALL), BF16),
            jax.ShapeDtypeStruct((T_ALL, DA_HEADS * LANES), BF16),
            jax.ShapeDtypeStruct((DA_HEADS, LANES, T_ALL), BF16),
            jax.ShapeDtypeStruct((GQA_KV_HEADS, LANES, T_ALL), BF16),
            jax.ShapeDtypeStruct((T_ALL, LANES), BF16),
            jax.ShapeDtypeStruct((LANES, T_ALL), BF16),
            jax.ShapeDtypeStruct((T_ALL, LRU_WIDTH), F32),
            jax.ShapeDtypeStruct((T_ALL, LRU_WIDTH), F32),
        ],
        compiler_params=_params(1),
        name="inproj",
    )(h, mods, g, w_in, cos_t, sin_t, qk_gain)


def _lru_tile_index(d, j):
    lat = jnp.where(d == 0, j - 1, N_LAT_TILES - j)
    return jnp.where(j == 0, N_LAT_TILES, lat)


def _lru_kernel(x_ref, cw_ref, cb_ref, wa_ref, wi_ref, ba_ref, bi_ref, lam_ref, o_ref, carry_ref):
    d = pl.program_id(0)
    j = pl.program_id(1)
    ti = _lru_tile_index(d, j)
    t0 = pl.multiple_of(ti * TM, TM)

    @pl.when(j == 0)
    def _():
        carry_ref[...] = jnp.zeros_like(carry_ref)

    x = x_ref[pl.ds(t0, TM), :]
    prev8 = x_ref[pl.ds(pl.multiple_of(jnp.maximum(t0 - 8, 0), 8), 8), :]
    next8 = x_ref[pl.ds(pl.multiple_of(jnp.minimum(t0 + TM, T_ALL - 8), 8), 8), :]
    has_prev = jnp.logical_and(ti != 0, ti != N_LAT_TILES)
    has_next = jnp.logical_and(ti != N_LAT_TILES - 1, ti != N_LAT_TILES)
    prev_row = jnp.where(has_prev, prev8[7:8], 0.0)
    next_row0 = jnp.where(has_next, next8[0:1], 0.0)
    next_row1 = jnp.where(has_next, next8[1:2], 0.0)
    row = lax.broadcasted_iota(jnp.int32, (TM, LRU_WIDTH), 0)
    xm1 = jnp.where(row == 0, prev_row, pltpu.roll(x, 1, 0))
    xp1 = jnp.where(row == TM - 1, next_row0, pltpu.roll(x, TM - 1, 0))
    xp2 = jnp.where(row == TM - 2, next_row0,
                    jnp.where(row == TM - 1, next_row1, pltpu.roll(x, TM - 2, 0)))
    cw = cw_ref[...]
    xc = xm1 * cw[0:1] + x * cw[1:2] + xp1 * cw[2:3] + xp2 * cw[3:4] + cb_ref[...]

    xcb = xc.astype(BF16)
    r = _sigmoid(jnp.dot(xcb, wa_ref[0], preferred_element_type=F32) + ba_ref[0])
    gi = _sigmoid(jnp.dot(xcb, wi_ref[0], preferred_element_type=F32) + bi_ref[0])
    neg_lam = -lam_ref[0]
    softplus = jnp.maximum(neg_lam, 0.0) + jnp.log1p(jnp.exp(-jnp.abs(neg_lam)))
    log_a = (-LRU_C) * r * softplus
    a = jnp.exp(log_a)
    u = jnp.sqrt(-jnp.tanh(log_a) * (a * a + 1.0)) * (gi * xc)

    def scan(A, B, forward):
        k = 1
        while k < TM:
            if forward:
                keep = row >= k
                As = jnp.where(keep, pltpu.roll(A, k, 0), 1.0)
                Bs = jnp.where(keep, pltpu.roll(B, k, 0), 0.0)
            else:
                keep = row < TM - k
                As = jnp.where(keep, pltpu.roll(A, TM - k, 0), 1.0)
                Bs = jnp.where(keep, pltpu.roll(B, TM - k, 0), 0.0)
            B = A * Bs + B
            A = A * As
            k *= 2
        return A, B

    carry = carry_ref[0:1, :]

    @pl.when(d == 0)
    def _():
        A, B = scan(a, u, True)
        hs = B + A * carry
        o_ref[0] = hs
        carry_ref[0:1, :] = hs[TM - 1:TM]

    @pl.when(d == 1)
    def _():
        A, B = scan(a, u, False)
        hs = B + A * carry
        o_ref[0] = hs
        carry_ref[0:1, :] = hs[0:1]


def _lru(xb, conv_w, conv_b, wa, wi, ba, bi, lam):
    per_dir = lambda *tail: pl.BlockSpec((1,) + tail, lambda d, j: (d,) + (0,) * len(tail))
    return pl.pallas_call(
        _lru_kernel,
        grid=(2, N_TILES),
        in_specs=[
            _resident((T_ALL, LRU_WIDTH)),
            pl.BlockSpec((4, LRU_WIDTH), lambda d, j: (0, 0)),
            pl.BlockSpec((1, LRU_WIDTH), lambda d, j: (0, 0)),
            per_dir(LRU_WIDTH, LRU_WIDTH),
            per_dir(LRU_WIDTH, LRU_WIDTH),
            per_dir(1, LRU_WIDTH),
            per_dir(1, LRU_WIDTH),
            per_dir(1, LRU_WIDTH),
        ],
        out_specs=pl.BlockSpec((1, TM, LRU_WIDTH), lambda d, j: (d, _lru_tile_index(d, j), 0)),
        out_shape=jax.ShapeDtypeStruct((2, T_ALL, LRU_WIDTH), F32),
        scratch_shapes=[pltpu.VMEM((8, LRU_WIDTH), F32)],
        compiler_params=_params(2),
        name="rglru",
    )(xb, conv_w, conv_b, wa, wi, ba, bi, lam)


def _attend(k_ref, v_chunk, weights, qi, m_ref, l_ref, acc_ref):
    dv = acc_ref.shape[1]
    m_ref[...] = jnp.full_like(m_ref, NEG_BIG)
    l_ref[...] = jnp.zeros_like(l_ref)
    acc_ref[...] = jnp.zeros_like(acc_ref)

    def accumulate(kc, vc):
        for j, w in enumerate(weights):
            s = jnp.dot(kc, w, preferred_element_type=F32)
            m_old = m_ref[j:j + 1, :]
            m_new = jnp.maximum(m_old, jnp.max(s, axis=0, keepdims=True))
            alpha = jnp.exp2(m_old - m_new)
            p = jnp.exp2(s - m_new)
            l_ref[j:j + 1, :] = alpha * l_ref[j:j + 1, :] + jnp.sum(p, axis=0, keepdims=True)
            pv = jnp.dot(vc, p.astype(BF16), preferred_element_type=F32)
            acc_ref[j] = alpha * acc_ref[j] + pv
            m_ref[j:j + 1, :] = m_new

    def lat_chunk(c, carry):
        start = pl.multiple_of(c * TK, TK)
        accumulate(k_ref[pl.ds(start, TK), :], v_chunk(start, TK))
        return carry

    n_lat = jnp.where(qi < N_LAT_TILES, N_K_CHUNKS, 0)
    lax.fori_loop(0, n_lat, lat_chunk, 0)
    accumulate(k_ref[SEQ:T_ALL, :], v_chunk(SEQ, CTX_LEN))
    return [acc_ref[j] * (1.0 / l_ref[j:j + 1, :]) for j in range(len(weights))]


def _da_kernel(lam_init, qt_ref, k_ref, vt_ref, lam_ref, g_ref, o_ref, m_ref, l_ref, acc_ref):
    qi = pl.program_id(1)
    qt = qt_ref[0]
    row = lax.broadcasted_iota(jnp.int32, qt.shape, 0)
    zero = jnp.zeros_like(qt)
    weights = [jnp.where(row < HEAD_DIM, qt, zero), jnp.where(row >= HEAD_DIM, qt, zero)]
    o1, o2 = _attend(k_ref, lambda s, n: vt_ref[0, :, pl.ds(s, n)], weights, qi,
                     m_ref, l_ref, acc_ref)
    lf = lam_ref[...]
    lam = (jnp.exp(jnp.sum(lf[0:1] * lf[1:2], axis=-1, keepdims=True))
           - jnp.exp(jnp.sum(lf[2:3] * lf[3:4], axis=-1, keepdims=True)) + lam_init)
    o = o1 - lam * o2
    ms = jnp.mean(o * o, axis=0, keepdims=True)
    on = (o * lax.rsqrt(ms + EPS)) * g_ref[...] * (1.0 - lam_init)
    o_ref[...] = on.T.astype(o_ref.dtype)


def _da_attention(qt, k, vt, da_lam, subln_g, lam_init, n_q):
    return pl.pallas_call(
        functools.partial(_da_kernel, lam_init),
        grid=(DA_HEADS, n_q),
        in_specs=[
            pl.BlockSpec((1, LANES, TQ), lambda h, i: (h, 0, i)),
            pl.BlockSpec((T_ALL, LANES), lambda h, i: (0, h)),
            pl.BlockSpec((1, LANES, T_ALL), lambda h, i: (h, 0, 0)),
            pl.BlockSpec((4, HEAD_DIM), lambda h, i: (0, 0)),
            pl.BlockSpec((LANES, 1), lambda h, i: (0, 0)),
        ],
        out_specs=pl.BlockSpec((TQ, LANES), lambda h, i: (i, h)),
        out_shape=jax.ShapeDtypeStruct((n_q * TQ, DA_HEADS * LANES), BF16),
        scratch_shapes=[pltpu.VMEM((8, TQ), F32), pltpu.VMEM((8, TQ), F32),
                        pltpu.VMEM((2, LANES, TQ), F32)],
        compiler_params=_params(2),
        name="da_attention",
    )(qt, k, vt, da_lam, subln_g)


def _gqa_kernel(qt_ref, k_ref, vt_ref, o_ref, m_ref, l_ref, acc_ref):
    g = pl.program_id(0)
    qi = pl.program_id(1)
    qt = qt_ref[0]
    zero = jnp.zeros((HEAD_DIM, TQ), qt.dtype)
    weights = []
    for j in range(2):
        qj = qt[j * HEAD_DIM:(j + 1) * HEAD_DIM]
        weights.append(jnp.where(g == 0, jnp.concatenate([qj, zero], axis=0),
                                 jnp.concatenate([zero, qj], axis=0)))
    o0, o1 = _attend(k_ref, lambda s, n: vt_ref[:, pl.ds(s, n)], weights, qi,
                     m_ref, l_ref, acc_ref)
    o_ref[...] = jnp.concatenate([o0, o1], axis=0).T.astype(o_ref.dtype)


def _gqa_attention(qt, k, vt, n_q):
    return pl.pallas_call(
        _gqa_kernel,
        grid=(GQA_KV_HEADS, n_q),
        in_specs=[
            pl.BlockSpec((1, LANES, TQ), lambda g, i: (g, 0, i)),
            pl.BlockSpec((T_ALL, LANES), lambda g, i: (0, 0)),
            pl.BlockSpec((HEAD_DIM, T_ALL), lambda g, i: (g, 0)),
        ],
        out_specs=pl.BlockSpec((TQ, LANES), lambda g, i: (i, g)),
        out_shape=jax.ShapeDtypeStruct((n_q * TQ, GQA_KV_HEADS * LANES), BF16),
        scratch_shapes=[pltpu.VMEM((8, TQ), F32), pltpu.VMEM((8, TQ), F32),
                        pltpu.VMEM((2, HEAD_DIM, TQ), F32)],
        compiler_params=_params(2),
        name="gqa_attention",
    )(qt, k, vt)


def _gelu_tanh(x):
    return 0.5 * x * (1.0 + jnp.tanh(math.sqrt(2.0 / math.pi) * (x + 0.044715 * (x * x * x))))


def _outproj_kernel(h_ref, mod_ref, a_ref, g_ref, hs_ref, gb_ref, w_ref, o_ref):
    mods = mod_ref[0]
    r = ((hs_ref[0] + hs_ref[1]) * _gelu_tanh(gb_ref[...])).astype(BF16)
    n_a = DA_HEADS * LANES
    n_g = n_a + GQA_KV_HEADS * LANES
    y = (jnp.dot(a_ref[...], w_ref[0:n_a, :], preferred_element_type=F32)
         + jnp.dot(g_ref[...], w_ref[n_a:n_g, :], preferred_element_type=F32)
         + jnp.dot(r, w_ref[n_g:, :], preferred_element_type=F32))
    o_ref[...] = h_ref[...] + mods[5:6] * y


def _outproj(h, mods, a, g, hs, gb, w_out, n_tiles):
    row_tile = lambda width: pl.BlockSpec((TM, width), lambda i: (i, 0))
    return pl.pallas_call(
        _outproj_kernel,
        grid=(n_tiles,),
        in_specs=[
            row_tile(D_MODEL),
            pl.BlockSpec((1, N_MOD, D_MODEL), lambda i: (i // N_LAT_TILES, 0, 0)),
            row_tile(DA_HEADS * LANES),
            row_tile(GQA_KV_HEADS * LANES),
            pl.BlockSpec((2, TM, LRU_WIDTH), lambda i: (0, i, 0)),
            row_tile(LRU_WIDTH),
            _resident((D_MODEL, D_MODEL)),
        ],
        out_specs=row_tile(D_MODEL),
        out_shape=jax.ShapeDtypeStruct((n_tiles * TM, D_MODEL), F32),
        compiler_params=_params(1),
        name="outproj",
    )(h, mods, a, g, hs, gb, w_out)


def _final_norm_kernel(h_ref, g_ref, o_ref):
    x = h_ref[...]
    ms = jnp.mean(x * x, axis=-1, keepdims=True)
    o_ref[...] = (x * lax.rsqrt(ms + EPS)) * g_ref[...]


def _final_norm(h, g):
    return pl.pallas_call(
        _final_norm_kernel,
        grid=(N_LAT_TILES,),
        in_specs=[pl.BlockSpec((TM, D_MODEL), lambda i: (i, 0)),
                  pl.BlockSpec((1, D_MODEL), lambda i: (0, 0))],
        out_specs=pl.BlockSpec((TM, D_MODEL), lambda i: (i, 0)),
        out_shape=jax.ShapeDtypeStruct((SEQ, D_MODEL), F32),
        compiler_params=_params(1),
        name="final_norm",
    )(h, g)


def _rope_tables():
    half = HEAD_DIM // 4
    t = jnp.arange(SEQ, dtype=jnp.int32)
    row = (t // GRID_W).astype(F32)
    col = (t % GRID_W).astype(F32)
    inv = ROPE_THETA ** (-jnp.arange(half, dtype=F32) / half)
    ang_r = row[:, None] * inv
    ang_c = col[:, None] * inv
    cos64 = jnp.concatenate([jnp.cos(ang_r)] * 2 + [jnp.cos(ang_c)] * 2, axis=-1)
    sin64 = jnp.concatenate([-jnp.sin(ang_r), jnp.sin(ang_r), -jnp.sin(ang_c), jnp.sin(ang_c)],
                            axis=-1)
    cos_t = jnp.concatenate([jnp.tile(cos64, (1, 2)), jnp.ones((CTX_LEN, LANES), F32)], axis=0)
    sin_t = jnp.concatenate([jnp.tile(sin64, (1, 2)), jnp.zeros((CTX_LEN, LANES), F32)], axis=0)
    return cos_t, sin_t


def _block_diag(w):
    eye = jnp.eye(LRU_BLOCKS, dtype=w.dtype)
    full = jnp.einsum('dkij,kl->dkilj', w, eye)
    return full.reshape(2, LRU_WIDTH, LRU_WIDTH).astype(BF16)


def _chunk_cols(w):
    return w.reshape(D_MODEL, N_FF_CHUNKS, FF_CHUNK).transpose(1, 0, 2).astype(BF16)


def kernel(x, c, ctx, c_ctx, ada_w, ada_b, norm_g, ffn1_w13, ffn1_w2, ffn2_w13, ffn2_w2,
           w_in, w_out, da_lam, da_subln_g, qk_norm_g, lru_conv_w, lru_conv_b,
           lru_wa, lru_ba, lru_wi, lru_bi, lru_lambda, final_g):
    assert x.shape == (1, SEQ, D_MODEL) and ctx.shape == (1, CTX_LEN, D_MODEL)
    cc = jnp.zeros((8, D_MODEL), F32).at[0].set(c[0]).at[1].set(c_ctx)
    mods_all = _ada_mods(cc, ada_w, ada_b).reshape(DEPTH, 8, N_MOD, D_MODEL)[:, :2]
    cos_t, sin_t = _rope_tables()

    h = jnp.concatenate([x[0], ctx[0]], axis=0)
    for l in range(DEPTH):
        last = l == DEPTH - 1
        lam_init = 0.8 - 0.6 * math.exp(-0.3 * l)
        mods = mods_all[l]

        def ffn_weights(w13, w2):
            return (_chunk_cols(w13[l][:, :D_FF]), _chunk_cols(w13[l][:, D_FF:]),
                    w2[l].reshape(N_FF_CHUNKS, FF_CHUNK, D_MODEL).astype(BF16))

        h = _ffn(h, mods, norm_g[l, 0:1], *ffn_weights(ffn1_w13, ffn1_w2), 0, N_TILES)

        qk_gain = jnp.tile(qk_norm_g[l], (1, 2))
        qt_da, k_da, vt_da, qt_g, k_g, vt_g, xb, gb = _inproj(
            h, mods, norm_g[l, 1:2], w_in[l].astype(BF16), cos_t, sin_t, qk_gain)

        hs = _lru(xb, lru_conv_w[l], lru_conv_b[l][None, :], _block_diag(lru_wa[l]),
                  _block_diag(lru_wi[l]), lru_ba[l][:, None, :], lru_bi[l][:, None, :],
                  lru_lambda[l][:, None, :])

        n_q = N_LAT_TILES if last else N_TILES
        a = _da_attention(qt_da, k_da, vt_da, da_lam[l], da_subln_g[l][:, None], lam_init, n_q)
        g = _gqa_attention(qt_g, k_g, vt_g, n_q)

        h = _outproj(h, mods, a, g, hs, gb, w_out[l].astype(BF16), n_q)
        h = _ffn(h, mods, norm_g[l, 2:3], *ffn_weights(ffn2_w13, ffn2_w2), 6, n_q)
    return _final_norm(h, final_g[None, :])[None]
```

```python
import functools
import math

import jax
import jax.numpy as jnp
from jax import lax
from jax.experimental import pallas as pl
from jax.experimental.pallas import tpu as pltpu

D_MODEL = 1024
SEQ = 16384
CTX_LEN = 256
T_ALL = SEQ + CTX_LEN
DEPTH = 4
GRID_W = 64
HEAD_DIM = 64
SCALE = HEAD_DIM ** -0.5
LOG2E = math.log2(math.e)
ROPE_THETA = 10000.0
DA_HEADS = 4
GQA_KV_HEADS = 2
LRU_WIDTH = 256
LRU_BLOCKS = 4
LRU_BLOCK = LRU_WIDTH // LRU_BLOCKS
LRU_C = 8.0
IN_WIDTH = 2560
D_FF = 2816
N_MOD = 9
EPS = 1e-6

LANES = 128
MXU_DIM = 256

TM = 256
N_TILES = T_ALL // TM
N_LAT_TILES = SEQ // TM
FF_CHUNK = MXU_DIM
N_FF_CHUNKS = D_FF // FF_CHUNK
TQ = 256
TK = 640
N_K_CHUNKS = T_ALL // TK
NEG_BIG = -1e30
VMEM_LIMIT = 48 * 1024 * 1024

F32 = jnp.float32
BF16 = jnp.bfloat16


def _params(n_axes):
    return pltpu.CompilerParams(dimension_semantics=("arbitrary",) * n_axes,
                                vmem_limit_bytes=VMEM_LIMIT)


def _resident(shape):
    zeros = (0,) * len(shape)
    return pl.BlockSpec(shape, lambda *_: zeros, pipeline_mode=pl.Buffered(1))


def _sigmoid(x):
    return 1.0 / (1.0 + jnp.exp(-x))


def _norm_mod(x, g, shift, scale):
    ms = jnp.mean(x * x, axis=-1, keepdims=True)
    y = (x * lax.rsqrt(ms + EPS)) * g
    return y * (1.0 + scale) + shift


ADA_COLS = 1152


def _ada_kernel(c_ref, w_ref, b_ref, o_ref):
    c = c_ref[...]
    s = (c * _sigmoid(c)).astype(BF16)
    o_ref[0] = jnp.dot(s, w_ref[0].astype(BF16), preferred_element_type=F32) + b_ref[0]


def _ada_mods(cc, ada_w, ada_b):
    width = N_MOD * D_MODEL
    return pl.pallas_call(
        _ada_kernel,
        grid=(DEPTH, width // ADA_COLS),
        in_specs=[
            pl.BlockSpec((8, D_MODEL), lambda l, j: (0, 0)),
            pl.BlockSpec((1, D_MODEL, ADA_COLS), lambda l, j: (l, 0, j)),
            pl.BlockSpec((1, 1, ADA_COLS), lambda l, j: (l, 0, j)),
        ],
        out_specs=pl.BlockSpec((1, 8, ADA_COLS), lambda l, j: (l, 0, j)),
        out_shape=jax.ShapeDtypeStruct((DEPTH, 8, width), F32),
        compiler_params=_params(2),
        name="ada_mods",
    )(cc, ada_w, ada_b.reshape(DEPTH, 1, width))


def _ffn_kernel(i0, h_ref, mod_ref, g_ref, w1_ref, w3_ref, w2_ref, o_ref, acc_ref):
    x = h_ref[...]
    mods = mod_ref[0]
    z = _norm_mod(x, g_ref[...], mods[i0:i0 + 1], mods[i0 + 1:i0 + 2]).astype(BF16)
    acc_ref[...] = jnp.zeros_like(acc_ref)

    def chunk(c, carry):
        a = jnp.dot(z, w1_ref[c], preferred_element_type=F32)
        b = jnp.dot(z, w3_ref[c], preferred_element_type=F32)
        gated = (a * _sigmoid(a) * b).astype(BF16)
        acc_ref[...] += jnp.dot(gated, w2_ref[c], preferred_element_type=F32)
        return carry

    lax.fori_loop(0, N_FF_CHUNKS, chunk, 0)
    o_ref[...] = x + (0.5 * mods[i0 + 2:i0 + 3]) * acc_ref[...]


def _ffn(h, mods, g, w1c, w3c, w2c, i0, n_tiles):
    return pl.pallas_call(
        functools.partial(_ffn_kernel, i0),
        grid=(n_tiles,),
        in_specs=[
            pl.BlockSpec((TM, D_MODEL), lambda i: (i, 0)),
            pl.BlockSpec((1, N_MOD, D_MODEL), lambda i: (i // N_LAT_TILES, 0, 0)),
            pl.BlockSpec((1, D_MODEL), lambda i: (0, 0)),
            _resident((N_FF_CHUNKS, D_MODEL, FF_CHUNK)),
            _resident((N_FF_CHUNKS, D_MODEL, FF_CHUNK)),
            _resident((N_FF_CHUNKS, FF_CHUNK, D_MODEL)),
        ],
        out_specs=pl.BlockSpec((TM, D_MODEL), lambda i: (i, 0)),
        out_shape=jax.ShapeDtypeStruct((n_tiles * TM, D_MODEL), F32),
        scratch_shapes=[pltpu.VMEM((TM, D_MODEL), F32)],
        compiler_params=_params(1),
        name="ffn",
    )(h, mods, g, w1c, w3c, w2c)


def _swap16(x):
    lane = lax.broadcasted_iota(jnp.int32, x.shape, 1)
    return jnp.where(lane % 32 < 16, pltpu.roll(x, LANES - 16, 1), pltpu.roll(x, 16, 1))


def _rope(x, cos, sin):
    return x * cos + _swap16(x) * sin


def _head_mean_sq(x):
    sq = x * x
    hi = sq.astype(BF16)
    lo = (sq - hi.astype(F32)).astype(BF16)
    r = lax.broadcasted_iota(jnp.int32, (LANES, LANES), 0) // HEAD_DIM
    c = lax.broadcasted_iota(jnp.int32, (LANES, LANES), 1) // HEAD_DIM
    ones = jnp.where(r == c, 1.0, 0.0).astype(BF16)
    tot = (jnp.dot(hi, ones, preferred_element_type=F32)
           + jnp.dot(lo, ones, preferred_element_type=F32))
    return tot * (1.0 / HEAD_DIM)


def _inproj_kernel(h_ref, mod_ref, g_ref, w_ref, cos_ref, sin_ref, qkg_ref,
                   qt_da_ref, k_da_ref, vt_da_ref, qt_g_ref, k_g_ref, vt_g_ref, xb_ref, gb_ref):
    mods = mod_ref[0]
    z = _norm_mod(h_ref[...], g_ref[...], mods[3:4], mods[4:5]).astype(BF16)
    proj = jnp.dot(z, w_ref[...], preferred_element_type=F32)
    cos = cos_ref[...]
    sin = sin_ref[...]
    q_scale = SCALE * LOG2E

    def slab(j):
        return proj[:, j * LANES:(j + 1) * LANES]

    for h in range(DA_HEADS):
        q = _rope(slab(h), cos, sin) * q_scale
        qt_da_ref[h] = q.T.astype(BF16)
        k_da_ref[:, h * LANES:(h + 1) * LANES] = _rope(slab(4 + h), cos, sin).astype(BF16)
        vt_da_ref[h] = slab(8 + h).T.astype(BF16)

    gq_gain = qkg_ref[0:1, :]
    gk_gain = qkg_ref[1:2, :]
    for g in range(GQA_KV_HEADS):
        q = slab(12 + g)
        q = (q * lax.rsqrt(_head_mean_sq(q) + EPS)) * gq_gain
        qt_g_ref[g] = (_rope(q, cos, sin) * q_scale).T.astype(BF16)
    k = slab(14)
    k = (k * lax.rsqrt(_head_mean_sq(k) + EPS)) * gk_gain
    k_g_ref[...] = _rope(k, cos, sin).astype(BF16)
    vt_g_ref[...] = slab(15).T.astype(BF16)
    xb_ref[...] = proj[:, 2048:2304]
    gb_ref[...] = proj[:, 2304:2560]


def _inproj(h, mods, g, w_in, cos_t, sin_t, qk_gain):
    row_tile = lambda width: pl.BlockSpec((TM, width), lambda i: (i, 0))
    return pl.pallas_call(
        _inproj_kernel,
        grid=(N_TILES,),
        in_specs=[
            row_tile(D_MODEL),
            pl.BlockSpec((1, N_MOD, D_MODEL), lambda i: (i // N_LAT_TILES, 0, 0)),
            pl.BlockSpec((1, D_MODEL), lambda i: (0, 0)),
            _resident((D_MODEL, IN_WIDTH)),
            row_tile(LANES),
            row_tile(LANES),
            pl.BlockSpec((2, LANES), lambda i: (0, 0)),
        ],
        out_specs=[
            pl.BlockSpec((DA_HEADS, LANES, TM), lambda i: (0, 0, i)),
            row_tile(DA_HEADS * LANES),
            pl.BlockSpec((DA_HEADS, LANES, TM), lambda i: (0, 0, i)),
            pl.BlockSpec((GQA_KV_HEADS, LANES, TM), lambda i: (0, 0, i)),
            row_tile(LANES),
            pl.BlockSpec((LANES, TM), lambda i: (0, i)),
            row_tile(LRU_WIDTH),
            row_tile(LRU_WIDTH),
        ],
        out_shape=[
            jax.ShapeDtypeStruct((DA_HEADS, LANES, T_ALL), BF16),
            jax.ShapeDtypeStruct((T_ALL, DA_HEADS * LANES), BF16),
            jax.ShapeDtypeStruct((DA_HEADS, LANES, T_ALL), BF16),
            jax.ShapeDtypeStruct((GQA_KV_HEADS, LANES, T_ALL), BF16),
            jax.ShapeDtypeStruct((T_ALL, LANES), BF16),
            jax.ShapeDtypeStruct((LANES, T_ALL), BF16),
            jax.ShapeDtypeStruct((T_ALL, LRU_WIDTH), F32),
            jax.ShapeDtypeStruct((T_ALL, LRU_WIDTH), F32),
        ],
        compiler_params=_params(1),
        name="inproj",
    )(h, mods, g, w_in, cos_t, sin_t, qk_gain)


def _lru_tile_index(d, j):
    lat = jnp.where(d == 0, j - 1, N_LAT_TILES - j)
    return jnp.where(j == 0, N_LAT_TILES, lat)


def _lru_kernel(x_ref, cw_ref, cb_ref, wa_ref, wi_ref, ba_ref, bi_ref, lam_ref, o_ref, carry_ref):
    d = pl.program_id(0)
    j = pl.program_id(1)
    ti = _lru_tile_index(d, j)
    t0 = pl.multiple_of(ti * TM, TM)

    @pl.when(j == 0)
    def _():
        carry_ref[...] = jnp.zeros_like(carry_ref)

    x = x_ref[pl.ds(t0, TM), :]
    prev8 = x_ref[pl.ds(pl.multiple_of(jnp.maximum(t0 - 8, 0), 8), 8), :]
    next8 = x_ref[pl.ds(pl.multiple_of(jnp.minimum(t0 + TM, T_ALL - 8), 8), 8), :]
    has_prev = jnp.logical_and(ti != 0, ti != N_LAT_TILES)
    has_next = jnp.logical_and(ti != N_LAT_TILES - 1, ti != N_LAT_TILES)
    prev_row = jnp.where(has_prev, prev8[7:8], 0.0)
    next_row0 = jnp.where(has_next, next8[0:1], 0.0)
    next_row1 = jnp.where(has_next, next8[1:2], 0.0)
    row = lax.broadcasted_iota(jnp.int32, (TM, LRU_WIDTH), 0)
    xm1 = jnp.where(row == 0, prev_row, pltpu.roll(x, 1, 0))
    xp1 = jnp.where(row == TM - 1, next_row0, pltpu.roll(x, TM - 1, 0))
    xp2 = jnp.where(row == TM - 2, next_row0,
                    jnp.where(row == TM - 1, next_row1, pltpu.roll(x, TM - 2, 0)))
    cw = cw_ref[...]
    xc = xm1 * cw[0:1] + x * cw[1:2] + xp1 * cw[2:3] + xp2 * cw[3:4] + cb_ref[...]

    xcb = xc.astype(BF16)
    r = _sigmoid(jnp.dot(xcb, wa_ref[0], preferred_element_type=F32) + ba_ref[0])
    gi = _sigmoid(jnp.dot(xcb, wi_ref[0], preferred_element_type=F32) + bi_ref[0])
    neg_lam = -lam_ref[0]
    softplus = jnp.maximum(neg_lam, 0.0) + jnp.log1p(jnp.exp(-jnp.abs(neg_lam)))
    log_a = (-LRU_C) * r * softplus
    a = jnp.exp(log_a)
    u = jnp.sqrt(-jnp.tanh(log_a) * (a * a + 1.0)) * (gi * xc)

    def scan(A, B, forward):
        k = 1
        while k < TM:
            if forward:
                keep = row >= k
                As = jnp.where(keep, pltpu.roll(A, k, 0), 1.0)
                Bs = jnp.where(keep, pltpu.roll(B, k, 0), 0.0)
            else:
                keep = row < TM - k
                As = jnp.where(keep, pltpu.roll(A, TM - k, 0), 1.0)
                Bs = jnp.where(keep, pltpu.roll(B, TM - k, 0), 0.0)
            B = A * Bs + B
            A = A * As
            k *= 2
        return A, B

    carry = carry_ref[0:1, :]

    @pl.when(d == 0)
    def _():
        A, B = scan(a, u, True)
        hs = B + A * carry
        o_ref[0] = hs
        carry_ref[0:1, :] = hs[TM - 1:TM]

    @pl.when(d == 1)
    def _():
        A, B = scan(a, u, False)
        hs = B + A * carry
        o_ref[0] = hs
        carry_ref[0:1, :] = hs[0:1]


def _lru(xb, conv_w, conv_b, wa, wi, ba, bi, lam):
    per_dir = lambda *tail: pl.BlockSpec((1,) + tail, lambda d, j: (d,) + (0,) * len(tail))
    return pl.pallas_call(
        _lru_kernel,
        grid=(2, N_TILES),
        in_specs=[
            _resident((T_ALL, LRU_WIDTH)),
            pl.BlockSpec((4, LRU_WIDTH), lambda d, j: (0, 0)),
            pl.BlockSpec((1, LRU_WIDTH), lambda d, j: (0, 0)),
            per_dir(LRU_WIDTH, LRU_WIDTH),
            per_dir(LRU_WIDTH, LRU_WIDTH),
            per_dir(1, LRU_WIDTH),
            per_dir(1, LRU_WIDTH),
            per_dir(1, LRU_WIDTH),
        ],
        out_specs=pl.BlockSpec((1, TM, LRU_WIDTH), lambda d, j: (d, _lru_tile_index(d, j), 0)),
        out_shape=jax.ShapeDtypeStruct((2, T_ALL, LRU_WIDTH), F32),
        scratch_shapes=[pltpu.VMEM((8, LRU_WIDTH), F32)],
        compiler_params=_params(2),
        name="rglru",
    )(xb, conv_w, conv_b, wa, wi, ba, bi, lam)


def _softmax_chunk(s, m_old, l_old):
    m_new = jnp.maximum(m_old, jnp.max(s, axis=0, keepdims=True))
    alpha = jnp.exp2(m_old - m_new)
    p = jnp.exp2(s - m_new)
    l_new = alpha * l_old + jnp.sum(p, axis=0, keepdims=True)
    return p.astype(BF16), m_new, l_new, alpha


def _attend(k_ref, v_chunk, weights, qi, l_ref, acc_ref, s_refs, p_refs):
    maps = range(len(weights))

    def scores(c, s_ref):
        kc = k_ref[pl.ds(pl.multiple_of(c * TK, TK), TK), :]
        for j in maps:
            s_ref[j] = jnp.dot(kc, weights[j], preferred_element_type=F32)

    def softmax(s_ref, p_ref, m, l):
        out = [_softmax_chunk(s_ref[j], m[j], l[j]) for j in maps]
        for j in maps:
            p_ref[j] = out[j][0]
        return ([o[1] for o in out], [o[2] for o in out], [o[3] for o in out])

    def values(c, p_ref, alpha):
        vc = v_chunk(pl.multiple_of(c * TK, TK), TK)
        for j in maps:
            acc_ref[j] = alpha[j] * acc_ref[j] + jnp.dot(vc, p_ref[j], preferred_element_type=F32)

    @pl.when(qi < N_LAT_TILES)
    def _():
        acc_ref[...] = jnp.zeros_like(acc_ref)
        p_refs[1][...] = jnp.zeros_like(p_refs[1])
        scores(0, s_refs[0])
        stat = lambda v: [jnp.full((1, TQ), v, F32) for _ in maps]

        def chunk_pair(i, carry):
            m, l, alpha_b = carry
            c0 = 2 * i
            scores(c0 + 1, s_refs[1])
            m, l, alpha_a = softmax(s_refs[0], p_refs[0], m, l)
            values(jnp.maximum(c0 - 1, 0), p_refs[1], alpha_b)
            scores(jnp.minimum(c0 + 2, N_K_CHUNKS - 1), s_refs[0])
            m, l, alpha_b = softmax(s_refs[1], p_refs[1], m, l)
            values(c0, p_refs[0], alpha_a)
            return m, l, alpha_b

        _, l, alpha_b = lax.fori_loop(0, N_K_CHUNKS // 2, chunk_pair,
                                      (stat(NEG_BIG), stat(0.0), stat(1.0)))
        values(N_K_CHUNKS - 1, p_refs[1], alpha_b)
        for j in maps:
            l_ref[j:j + 1, :] = l[j]

    @pl.when(qi == N_LAT_TILES)
    def _():
        kc = k_ref[SEQ:T_ALL, :]
        vc = v_chunk(SEQ, CTX_LEN)
        for j in maps:
            s = jnp.dot(kc, weights[j], preferred_element_type=F32)
            p, _, l, _ = _softmax_chunk(s, jnp.full((1, TQ), NEG_BIG, F32), jnp.zeros((1, TQ), F32))
            acc_ref[j] = jnp.dot(vc, p, preferred_element_type=F32)
            l_ref[j:j + 1, :] = l

    return [acc_ref[j] * (1.0 / l_ref[j:j + 1, :]) for j in maps]


def _attn_scratch(dv):
    return [pltpu.VMEM((8, TQ), F32), pltpu.VMEM((2, dv, TQ), F32),
            pltpu.VMEM((2, TK, TQ), F32), pltpu.VMEM((2, TK, TQ), F32),
            pltpu.VMEM((2, TK, TQ), BF16), pltpu.VMEM((2, TK, TQ), BF16)]


def _da_kernel(lam_init, qt_ref, k_ref, vt_ref, lam_ref, g_ref, o_ref,
               l_ref, acc_ref, s0_ref, s1_ref, p0_ref, p1_ref):
    qi = pl.program_id(1)
    qt = qt_ref[0]
    row = lax.broadcasted_iota(jnp.int32, qt.shape, 0)
    zero = jnp.zeros_like(qt)
    weights = [jnp.where(row < HEAD_DIM, qt, zero), jnp.where(row >= HEAD_DIM, qt, zero)]
    o1, o2 = _attend(k_ref, lambda s, n: vt_ref[0, :, pl.ds(s, n)], weights, qi,
                     l_ref, acc_ref, (s0_ref, s1_ref), (p0_ref, p1_ref))
    lf = lam_ref[...]
    lam = (jnp.exp(jnp.sum(lf[0:1] * lf[1:2], axis=-1, keepdims=True))
           - jnp.exp(jnp.sum(lf[2:3] * lf[3:4], axis=-1, keepdims=True)) + lam_init)
    o = o1 - lam * o2
    ms = jnp.mean(o * o, axis=0, keepdims=True)
    on = (o * lax.rsqrt(ms + EPS)) * g_ref[...] * (1.0 - lam_init)
    o_ref[...] = on.T.astype(o_ref.dtype)


def _da_attention(qt, k, vt, da_lam, subln_g, lam_init, n_q):
    return pl.pallas_call(
        functools.partial(_da_kernel, lam_init),
        grid=(DA_HEADS, n_q),
        in_specs=[
            pl.BlockSpec((1, LANES, TQ), lambda h, i: (h, 0, i)),
            pl.BlockSpec((T_ALL, LANES), lambda h, i: (0, h)),
            pl.BlockSpec((1, LANES, T_ALL), lambda h, i: (h, 0, 0)),
            pl.BlockSpec((4, HEAD_DIM), lambda h, i: (0, 0)),
            pl.BlockSpec((LANES, 1), lambda h, i: (0, 0)),
        ],
        out_specs=pl.BlockSpec((TQ, LANES), lambda h, i: (i, h)),
        out_shape=jax.ShapeDtypeStruct((n_q * TQ, DA_HEADS * LANES), BF16),
        scratch_shapes=_attn_scratch(LANES),
        compiler_params=_params(2),
        name="da_attention",
    )(qt, k, vt, da_lam, subln_g)


def _gqa_kernel(qt_ref, k_ref, vt_ref, o_ref, l_ref, acc_ref, s0_ref, s1_ref, p0_ref, p1_ref):
    g = pl.program_id(0)
    qi = pl.program_id(1)
    qt = qt_ref[0]
    zero = jnp.zeros((HEAD_DIM, TQ), qt.dtype)
    weights = []
    for j in range(2):
        qj = qt[j * HEAD_DIM:(j + 1) * HEAD_DIM]
        weights.append(jnp.where(g == 0, jnp.concatenate([qj, zero], axis=0),
                                 jnp.concatenate([zero, qj], axis=0)))
    o0, o1 = _attend(k_ref, lambda s, n: vt_ref[:, pl.ds(s, n)], weights, qi,
                     l_ref, acc_ref, (s0_ref, s1_ref), (p0_ref, p1_ref))
    o_ref[...] = jnp.concatenate([o0, o1], axis=0).T.astype(o_ref.dtype)


def _gqa_attention(qt, k, vt, n_q):
    return pl.pallas_call(
        _gqa_kernel,
        grid=(GQA_KV_HEADS, n_q),
        in_specs=[
            pl.BlockSpec((1, LANES, TQ), lambda g, i: (g, 0, i)),
            pl.BlockSpec((T_ALL, LANES), lambda g, i: (0, 0)),
            pl.BlockSpec((HEAD_DIM, T_ALL), lambda g, i: (g, 0)),
        ],
        out_specs=pl.BlockSpec((TQ, LANES), lambda g, i: (i, g)),
        out_shape=jax.ShapeDtypeStruct((n_q * TQ, GQA_KV_HEADS * LANES), BF16),
        scratch_shapes=_attn_scratch(HEAD_DIM),
        compiler_params=_params(2),
        name="gqa_attention",
    )(qt, k, vt)


def _gelu_tanh(x):
    return 0.5 * x * (1.0 + jnp.tanh(math.sqrt(2.0 / math.pi) * (x + 0.044715 * (x * x * x))))


def _outproj_kernel(h_ref, mod_ref, a_ref, g_ref, hs_ref, gb_ref, w_ref, o_ref):
    mods = mod_ref[0]
    r = ((hs_ref[0] + hs_ref[1]) * _gelu_tanh(gb_ref[...])).astype(BF16)
    n_a = DA_HEADS * LANES
    n_g = n_a + GQA_KV_HEADS * LANES
    y = (jnp.dot(a_ref[...], w_ref[0:n_a, :], preferred_element_type=F32)
         + jnp.dot(g_ref[...], w_ref[n_a:n_g, :], preferred_element_type=F32)
         + jnp.dot(r, w_ref[n_g:, :], preferred_element_type=F32))
    o_ref[...] = h_ref[...] + mods[5:6] * y


def _outproj(h, mods, a, g, hs, gb, w_out, n_tiles):
    row_tile = lambda width: pl.BlockSpec((TM, width), lambda i: (i, 0))
    return pl.pallas_call(
        _outproj_kernel,
        grid=(n_tiles,),
        in_specs=[
            row_tile(D_MODEL),
            pl.BlockSpec((1, N_MOD, D_MODEL), lambda i: (i // N_LAT_TILES, 0, 0)),
            row_tile(DA_HEADS * LANES),
            row_tile(GQA_KV_HEADS * LANES),
            pl.BlockSpec((2, TM, LRU_WIDTH), lambda i: (0, i, 0)),
            row_tile(LRU_WIDTH),
            _resident((D_MODEL, D_MODEL)),
        ],
        out_specs=row_tile(D_MODEL),
        out_shape=jax.ShapeDtypeStruct((n_tiles * TM, D_MODEL), F32),
        compiler_params=_params(1),
        name="outproj",
    )(h, mods, a, g, hs, gb, w_out)


def _final_norm_kernel(h_ref, g_ref, o_ref):
    x = h_ref[...]
    ms = jnp.mean(x * x, axis=-1, keepdims=True)
    o_ref[...] = (x * lax.rsqrt(ms + EPS)) * g_ref[...]


def _final_norm(h, g):
    return pl.pallas_call(
        _final_norm_kernel,
        grid=(N_LAT_TILES,),
        in_specs=[pl.BlockSpec((TM, D_MODEL), lambda i: (i, 0)),
                  pl.BlockSpec((1, D_MODEL), lambda i: (0, 0))],
        out_specs=pl.BlockSpec((TM, D_MODEL), lambda i: (i, 0)),
        out_shape=jax.ShapeDtypeStruct((SEQ, D_MODEL), F32),
        compiler_params=_params(1),
        name="final_norm",
    )(h, g)


def _rope_tables():
    half = HEAD_DIM // 4
    t = jnp.arange(SEQ, dtype=jnp.int32)
    row = (t // GRID_W).astype(F32)
    col = (t % GRID_W).astype(F32)
    inv = ROPE_THETA ** (-jnp.arange(half, dtype=F32) / half)
    ang_r = row[:, None] * inv
    ang_c = col[:, None] * inv
    cos64 = jnp.concatenate([jnp.cos(ang_r)] * 2 + [jnp.cos(ang_c)] * 2, axis=-1)
    sin64 = jnp.concatenate([-jnp.sin(ang_r), jnp.sin(ang_r), -jnp.sin(ang_c), jnp.sin(ang_c)],
                            axis=-1)
    cos_t = jnp.concatenate([jnp.tile(cos64, (1, 2)), jnp.ones((CTX_LEN, LANES), F32)], axis=0)
    sin_t = jnp.concatenate([jnp.tile(sin64, (1, 2)), jnp.zeros((CTX_LEN, LANES), F32)], axis=0)
    return cos_t, sin_t


def _block_diag(w):
    eye = jnp.eye(LRU_BLOCKS, dtype=w.dtype)
    full = jnp.einsum('dkij,kl->dkilj', w, eye)
    return full.reshape(2, LRU_WIDTH, LRU_WIDTH).astype(BF16)


def _chunk_cols(w):
    return w.reshape(D_MODEL, N_FF_CHUNKS, FF_CHUNK).transpose(1, 0, 2).astype(BF16)


def kernel(x, c, ctx, c_ctx, ada_w, ada_b, norm_g, ffn1_w13, ffn1_w2, ffn2_w13, ffn2_w2,
           w_in, w_out, da_lam, da_subln_g, qk_norm_g, lru_conv_w, lru_conv_b,
           lru_wa, lru_ba, lru_wi, lru_bi, lru_lambda, final_g):
    assert x.shape == (1, SEQ, D_MODEL) and ctx.shape == (1, CTX_LEN, D_MODEL)
    cc = jnp.zeros((8, D_MODEL), F32).at[0].set(c[0]).at[1].set(c_ctx)
    mods_all = _ada_mods(cc, ada_w, ada_b).reshape(DEPTH, 8, N_MOD, D_MODEL)[:, :2]
    cos_t, sin_t = _rope_tables()

    h = jnp.concatenate([x[0], ctx[0]], axis=0)
    for l in range(DEPTH):
        last = l == DEPTH - 1
        lam_init = 0.8 - 0.6 * math.exp(-0.3 * l)
        mods = mods_all[l]

        def ffn_weights(w13, w2):
            return (_chunk_cols(w13[l][:, :D_FF]), _chunk_cols(w13[l][:, D_FF:]),
                    w2[l].reshape(N_FF_CHUNKS, FF_CHUNK, D_MODEL).astype(BF16))

        h = _ffn(h, mods, norm_g[l, 0:1], *ffn_weights(ffn1_w13, ffn1_w2), 0, N_TILES)

        qk_gain = jnp.tile(qk_norm_g[l], (1, 2))
        qt_da, k_da, vt_da, qt_g, k_g, vt_g, xb, gb = _inproj(
            h, mods, norm_g[l, 1:2], w_in[l].astype(BF16), cos_t, sin_t, qk_gain)

        hs = _lru(xb, lru_conv_w[l], lru_conv_b[l][None, :], _block_diag(lru_wa[l]),
                  _block_diag(lru_wi[l]), lru_ba[l][:, None, :], lru_bi[l][:, None, :],
                  lru_lambda[l][:, None, :])

        n_q = N_LAT_TILES if last else N_TILES
        a = _da_attention(qt_da, k_da, vt_da, da_lam[l], da_subln_g[l][:, None], lam_init, n_q)
        g = _gqa_attention(qt_g, k_g, vt_g, n_q)

        h = _outproj(h, mods, a, g, hs, gb, w_out[l].astype(BF16), n_q)
        h = _ffn(h, mods, norm_g[l, 2:3], *ffn_weights(ffn2_w13, ffn2_w2), 6, n_q)
    return _final_norm(h, final_g[None, :])[None]
```

```python
import functools
import math

import jax
import jax.numpy as jnp
from jax import lax
from jax.experimental import pallas as pl
from jax.experimental.pallas import tpu as pltpu

D_MODEL = 1024
SEQ = 16384
CTX_LEN = 256
T_ALL = SEQ + CTX_LEN
DEPTH = 4
GRID_W = 64
HEAD_DIM = 64
SCALE = HEAD_DIM ** -0.5
LOG2E = math.log2(math.e)
ROPE_THETA = 10000.0
DA_HEADS = 4
GQA_KV_HEADS = 2
LRU_WIDTH = 256
LRU_BLOCKS = 4
LRU_BLOCK = LRU_WIDTH // LRU_BLOCKS
LRU_C = 8.0
IN_WIDTH = 2560
D_FF = 2816
N_MOD = 9
EPS = 1e-6

LANES = 128
MXU_DIM = 256

TM = 256
N_TILES = T_ALL // TM
N_LAT_TILES = SEQ // TM
FF_CHUNK = MXU_DIM
N_FF_CHUNKS = D_FF // FF_CHUNK
TQ = 512
TK = 640
N_K_CHUNKS = T_ALL // TK
NEG_BIG = -1e30
VMEM_LIMIT = 48 * 1024 * 1024

F32 = jnp.float32
BF16 = jnp.bfloat16


def _params(n_axes):
    return pltpu.CompilerParams(dimension_semantics=("arbitrary",) * n_axes,
                                vmem_limit_bytes=VMEM_LIMIT)


def _resident(shape):
    zeros = (0,) * len(shape)
    return pl.BlockSpec(shape, lambda *_: zeros, pipeline_mode=pl.Buffered(1))


def _sigmoid(x):
    return 1.0 / (1.0 + jnp.exp(-x))


def _norm_mod(x, g, shift, scale):
    ms = jnp.mean(x * x, axis=-1, keepdims=True)
    y = (x * lax.rsqrt(ms + EPS)) * g
    return y * (1.0 + scale) + shift


ADA_COLS = 1152


def _ada_kernel(c_ref, w_ref, b_ref, o_ref):
    c = c_ref[...]
    s = (c * _sigmoid(c)).astype(BF16)
    o_ref[0] = jnp.dot(s, w_ref[0].astype(BF16), preferred_element_type=F32) + b_ref[0]


def _ada_mods(cc, ada_w, ada_b):
    width = N_MOD * D_MODEL
    return pl.pallas_call(
        _ada_kernel,
        grid=(DEPTH, width // ADA_COLS),
        in_specs=[
            pl.BlockSpec((8, D_MODEL), lambda l, j: (0, 0)),
            pl.BlockSpec((1, D_MODEL, ADA_COLS), lambda l, j: (l, 0, j)),
            pl.BlockSpec((1, 1, ADA_COLS), lambda l, j: (l, 0, j)),
        ],
        out_specs=pl.BlockSpec((1, 8, ADA_COLS), lambda l, j: (l, 0, j)),
        out_shape=jax.ShapeDtypeStruct((DEPTH, 8, width), F32),
        compiler_params=_params(2),
        name="ada_mods",
    )(cc, ada_w, ada_b.reshape(DEPTH, 1, width))


def _ffn_kernel(i0, h_ref, mod_ref, g_ref, w1_ref, w3_ref, w2_ref, o_ref, acc_ref):
    x = h_ref[...]
    mods = mod_ref[0]
    z = _norm_mod(x, g_ref[...], mods[i0:i0 + 1], mods[i0 + 1:i0 + 2]).astype(BF16)
    acc_ref[...] = jnp.zeros_like(acc_ref)

    def chunk(c, carry):
        a = jnp.dot(z, w1_ref[c], preferred_element_type=F32)
        b = jnp.dot(z, w3_ref[c], preferred_element_type=F32)
        gated = (a * _sigmoid(a) * b).astype(BF16)
        acc_ref[...] += jnp.dot(gated, w2_ref[c], preferred_element_type=F32)
        return carry

    lax.fori_loop(0, N_FF_CHUNKS, chunk, 0)
    o_ref[...] = x + (0.5 * mods[i0 + 2:i0 + 3]) * acc_ref[...]


def _ffn(h, mods, g, w1c, w3c, w2c, i0, n_tiles):
    return pl.pallas_call(
        functools.partial(_ffn_kernel, i0),
        grid=(n_tiles,),
        in_specs=[
            pl.BlockSpec((TM, D_MODEL), lambda i: (i, 0)),
            pl.BlockSpec((1, N_MOD, D_MODEL), lambda i: (i // N_LAT_TILES, 0, 0)),
            pl.BlockSpec((1, D_MODEL), lambda i: (0, 0)),
            _resident((N_FF_CHUNKS, D_MODEL, FF_CHUNK)),
            _resident((N_FF_CHUNKS, D_MODEL, FF_CHUNK)),
            _resident((N_FF_CHUNKS, FF_CHUNK, D_MODEL)),
        ],
        out_specs=pl.BlockSpec((TM, D_MODEL), lambda i: (i, 0)),
        out_shape=jax.ShapeDtypeStruct((n_tiles * TM, D_MODEL), F32),
        scratch_shapes=[pltpu.VMEM((TM, D_MODEL), F32)],
        compiler_params=_params(1),
        name="ffn",
    )(h, mods, g, w1c, w3c, w2c)


def _swap16(x):
    lane = lax.broadcasted_iota(jnp.int32, x.shape, 1)
    return jnp.where(lane % 32 < 16, pltpu.roll(x, LANES - 16, 1), pltpu.roll(x, 16, 1))


def _rope(x, cos, sin):
    return x * cos + _swap16(x) * sin


def _head_mean_sq(x):
    sq = x * x
    hi = sq.astype(BF16)
    lo = (sq - hi.astype(F32)).astype(BF16)
    r = lax.broadcasted_iota(jnp.int32, (LANES, LANES), 0) // HEAD_DIM
    c = lax.broadcasted_iota(jnp.int32, (LANES, LANES), 1) // HEAD_DIM
    ones = jnp.where(r == c, 1.0, 0.0).astype(BF16)
    tot = (jnp.dot(hi, ones, preferred_element_type=F32)
           + jnp.dot(lo, ones, preferred_element_type=F32))
    return tot * (1.0 / HEAD_DIM)


def _inproj_kernel(h_ref, mod_ref, g_ref, w_ref, cos_ref, sin_ref, qkg_ref,
                   qt_da_ref, k_da_ref, vt_da_ref, qt_g_ref, k_g_ref, vt_g_ref, xb_ref, gb_ref):
    mods = mod_ref[0]
    z = _norm_mod(h_ref[...], g_ref[...], mods[3:4], mods[4:5]).astype(BF16)
    proj = jnp.dot(z, w_ref[...], preferred_element_type=F32)
    cos = cos_ref[...]
    sin = sin_ref[...]
    q_scale = SCALE * LOG2E

    def slab(j):
        return proj[:, j * LANES:(j + 1) * LANES]

    for h in range(DA_HEADS):
        q = _rope(slab(h), cos, sin) * q_scale
        qt_da_ref[h] = q.T.astype(BF16)
        k_da_ref[:, h * LANES:(h + 1) * LANES] = _rope(slab(4 + h), cos, sin).astype(BF16)
        vt_da_ref[h] = slab(8 + h).T.astype(BF16)

    gq_gain = qkg_ref[0:1, :]
    gk_gain = qkg_ref[1:2, :]
    for g in range(GQA_KV_HEADS):
        q = slab(12 + g)
        q = (q * lax.rsqrt(_head_mean_sq(q) + EPS)) * gq_gain
        qt_g_ref[g] = (_rope(q, cos, sin) * q_scale).T.astype(BF16)
    k = slab(14)
    k = (k * lax.rsqrt(_head_mean_sq(k) + EPS)) * gk_gain
    k_g_ref[...] = _rope(k, cos, sin).astype(BF16)
    vt_g_ref[...] = slab(15).T.astype(BF16)
    xb_ref[...] = proj[:, 2048:2304]
    gb_ref[...] = proj[:, 2304:2560]


def _inproj(h, mods, g, w_in, cos_t, sin_t, qk_gain):
    row_tile = lambda width: pl.BlockSpec((TM, width), lambda i: (i, 0))
    return pl.pallas_call(
        _inproj_kernel,
        grid=(N_TILES,),
        in_specs=[
            row_tile(D_MODEL),
            pl.BlockSpec((1, N_MOD, D_MODEL), lambda i: (i // N_LAT_TILES, 0, 0)),
            pl.BlockSpec((1, D_MODEL), lambda i: (0, 0)),
            _resident((D_MODEL, IN_WIDTH)),
            row_tile(LANES),
            row_tile(LANES),
            pl.BlockSpec((2, LANES), lambda i: (0, 0)),
        ],
        out_specs=[
            pl.BlockSpec((DA_HEADS, LANES, TM), lambda i: (0, 0, i)),
            row_tile(DA_HEADS * LANES),
            pl.BlockSpec((DA_HEADS, LANES, TM), lambda i: (0, 0, i)),
            pl.BlockSpec((GQA_KV_HEADS, LANES, TM), lambda i: (0, 0, i)),
            row_tile(LANES),
            pl.BlockSpec((LANES, TM), lambda i: (0, i)),
            row_tile(LRU_WIDTH),
            row_tile(LRU_WIDTH),
        ],
        out_shape=[
            jax.ShapeDtypeStruct((DA_HEADS, LANES, T_ALL), BF16),
            jax.ShapeDtypeStruct((T_ALL, DA_HEADS * LANES), BF16),
            jax.ShapeDtypeStruct((DA_HEADS, LANES, T_ALL), BF16),
            jax.ShapeDtypeStruct((GQA_KV_HEADS, LANES, T_ALL), BF16),
            jax.ShapeDtypeStruct((T_ALL, LANES), BF16),
            jax.ShapeDtypeStruct((LANES, T_ALL), BF16),
            jax.ShapeDtypeStruct((T_ALL, LRU_WIDTH), F32),
            jax.ShapeDtypeStruct((T_ALL, LRU_WIDTH), F32),
        ],
        compiler_params=_params(1),
        name="inproj",
    )(h, mods, g, w_in, cos_t, sin_t, qk_gain)


def _lru_tile_index(d, j):
    lat = jnp.where(d == 0, j - 1, N_LAT_TILES - j)
    return jnp.where(j == 0, N_LAT_TILES, lat)


def _lru_kernel(x_ref, cw_ref, cb_ref, wa_ref, wi_ref, ba_ref, bi_ref, lam_ref, o_ref, carry_ref):
    d = pl.program_id(0)
    j = pl.program_id(1)
    ti = _lru_tile_index(d, j)
    t0 = pl.multiple_of(ti * TM, TM)

    @pl.when(j == 0)
    def _():
        carry_ref[...] = jnp.zeros_like(carry_ref)

    x = x_ref[pl.ds(t0, TM), :]
    prev8 = x_ref[pl.ds(pl.multiple_of(jnp.maximum(t0 - 8, 0), 8), 8), :]
    next8 = x_ref[pl.ds(pl.multiple_of(jnp.minimum(t0 + TM, T_ALL - 8), 8), 8), :]
    has_prev = jnp.logical_and(ti != 0, ti != N_LAT_TILES)
    has_next = jnp.logical_and(ti != N_LAT_TILES - 1, ti != N_LAT_TILES)
    prev_row = jnp.where(has_prev, prev8[7:8], 0.0)
    next_row0 = jnp.where(has_next, next8[0:1], 0.0)
    next_row1 = jnp.where(has_next, next8[1:2], 0.0)
    row = lax.broadcasted_iota(jnp.int32, (TM, LRU_WIDTH), 0)
    xm1 = jnp.where(row == 0, prev_row, pltpu.roll(x, 1, 0))
    xp1 = jnp.where(row == TM - 1, next_row0, pltpu.roll(x, TM - 1, 0))
    xp2 = jnp.where(row == TM - 2, next_row0,
                    jnp.where(row == TM - 1, next_row1, pltpu.roll(x, TM - 2, 0)))
    cw = cw_ref[...]
    xc = xm1 * cw[0:1] + x * cw[1:2] + xp1 * cw[2:3] + xp2 * cw[3:4] + cb_ref[...]

    xcb = xc.astype(BF16)
    r = _sigmoid(jnp.dot(xcb, wa_ref[0], preferred_element_type=F32) + ba_ref[0])
    gi = _sigmoid(jnp.dot(xcb, wi_ref[0], preferred_element_type=F32) + bi_ref[0])
    neg_lam = -lam_ref[0]
    softplus = jnp.maximum(neg_lam, 0.0) + jnp.log1p(jnp.exp(-jnp.abs(neg_lam)))
    log_a = (-LRU_C) * r * softplus
    a = jnp.exp(log_a)
    u = jnp.sqrt(-jnp.tanh(log_a) * (a * a + 1.0)) * (gi * xc)

    def scan(A, B, forward):
        k = 1
        while k < TM:
            if forward:
                keep = row >= k
                As = jnp.where(keep, pltpu.roll(A, k, 0), 1.0)
                Bs = jnp.where(keep, pltpu.roll(B, k, 0), 0.0)
            else:
                keep = row < TM - k
                As = jnp.where(keep, pltpu.roll(A, TM - k, 0), 1.0)
                Bs = jnp.where(keep, pltpu.roll(B, TM - k, 0), 0.0)
            B = A * Bs + B
            A = A * As
            k *= 2
        return A, B

    carry = carry_ref[0:1, :]

    @pl.when(d == 0)
    def _():
        A, B = scan(a, u, True)
        hs = B + A * carry
        o_ref[0] = hs
        carry_ref[0:1, :] = hs[TM - 1:TM]

    @pl.when(d == 1)
    def _():
        A, B = scan(a, u, False)
        hs = B + A * carry
        o_ref[0] = hs
        carry_ref[0:1, :] = hs[0:1]


def _lru(xb, conv_w, conv_b, wa, wi, ba, bi, lam):
    per_dir = lambda *tail: pl.BlockSpec((1,) + tail, lambda d, j: (d,) + (0,) * len(tail))
    return pl.pallas_call(
        _lru_kernel,
        grid=(2, N_TILES),
        in_specs=[
            _resident((T_ALL, LRU_WIDTH)),
            pl.BlockSpec((4, LRU_WIDTH), lambda d, j: (0, 0)),
            pl.BlockSpec((1, LRU_WIDTH), lambda d, j: (0, 0)),
            per_dir(LRU_WIDTH, LRU_WIDTH),
            per_dir(LRU_WIDTH, LRU_WIDTH),
            per_dir(1, LRU_WIDTH),
            per_dir(1, LRU_WIDTH),
            per_dir(1, LRU_WIDTH),
        ],
        out_specs=pl.BlockSpec((1, TM, LRU_WIDTH), lambda d, j: (d, _lru_tile_index(d, j), 0)),
        out_shape=jax.ShapeDtypeStruct((2, T_ALL, LRU_WIDTH), F32),
        scratch_shapes=[pltpu.VMEM((8, LRU_WIDTH), F32)],
        compiler_params=_params(2),
        name="rglru",
    )(xb, conv_w, conv_b, wa, wi, ba, bi, lam)


def _softmax_chunk(s, m_old, l_old):
    m_new = jnp.maximum(m_old, jnp.max(s, axis=0, keepdims=True))
    alpha = jnp.exp2(m_old - m_new)
    p = jnp.exp2(s - m_new)
    l_new = alpha * l_old + jnp.sum(p, axis=0, keepdims=True)
    return p.astype(BF16), m_new, l_new, alpha


def _attend_ctx(kc, vc, weights):
    outs = []
    for w in weights:
        s = jnp.dot(kc, w, preferred_element_type=F32)
        p, _, l, _ = _softmax_chunk(s, jnp.full((1, CTX_LEN), NEG_BIG, F32),
                                    jnp.zeros((1, CTX_LEN), F32))
        outs.append(jnp.dot(vc, p, preferred_element_type=F32) * (1.0 / l))
    return outs


def _attend_lat(k_ref, v_chunk, weights, acc_ref, s_refs, p_refs):
    maps = range(len(weights))

    def scores(c, s_ref):
        kc = k_ref[pl.ds(pl.multiple_of(c * TK, TK), TK), :]
        for j in maps:
            s_ref[j] = jnp.dot(kc, weights[j], preferred_element_type=F32)

    def softmax(s_ref, p_ref, m, l):
        m_new, l_new, alpha = [], [], []
        for j in maps:
            m_j = jnp.maximum(m[j], jnp.max(s_ref[j], axis=0, keepdims=True))
            a_j = jnp.exp2(m[j] - m_j)
            p = jnp.exp2(s_ref[j] - m_j)
            l_new.append(a_j * l[j] + jnp.sum(p, axis=0, keepdims=True))
            p_ref[j] = p.astype(BF16)
            m_new.append(m_j)
            alpha.append(a_j)
        return m_new, l_new, alpha

    def values(c, p_ref, alpha):
        vc = v_chunk(pl.multiple_of(c * TK, TK), TK)
        for j in maps:
            acc_ref[j] = alpha[j] * acc_ref[j] + jnp.dot(vc, p_ref[j], preferred_element_type=F32)

    acc_ref[...] = jnp.zeros_like(acc_ref)
    p_refs[1][...] = jnp.zeros_like(p_refs[1])
    scores(0, s_refs[0])
    stat = lambda v: [jnp.full((1, TQ), v, F32) for _ in maps]

    def chunk_pair(i, carry):
        m, l, alpha_b = carry
        c0 = 2 * i
        scores(c0 + 1, s_refs[1])
        m, l, alpha_a = softmax(s_refs[0], p_refs[0], m, l)
        values(jnp.maximum(c0 - 1, 0), p_refs[1], alpha_b)
        scores(jnp.minimum(c0 + 2, N_K_CHUNKS - 1), s_refs[0])
        m, l, alpha_b = softmax(s_refs[1], p_refs[1], m, l)
        values(c0, p_refs[0], alpha_a)
        return m, l, alpha_b

    _, l, alpha_b = lax.fori_loop(0, N_K_CHUNKS // 2, chunk_pair,
                                  (stat(NEG_BIG), stat(0.0), stat(1.0)))
    values(N_K_CHUNKS - 1, p_refs[1], alpha_b)
    return [acc_ref[j] * (1.0 / l[j]) for j in maps]


def _attend(ctx_only, k_ref, vt_at, weights, scratch):
    if ctx_only:
        return _attend_ctx(k_ref[...], vt_at(0, CTX_LEN), weights)
    acc_ref, s0_ref, s1_ref, p0_ref, p1_ref = scratch
    return _attend_lat(k_ref, vt_at, weights, acc_ref, (s0_ref, s1_ref), (p0_ref, p1_ref))


def _attn_call(kernel, ctx_only, n_heads, dv, in_blocks, operands, out_width, name):
    if ctx_only:
        tq, n_q, n_keys, q0, k0 = CTX_LEN, 1, CTX_LEN, SEQ // CTX_LEN, SEQ // CTX_LEN
        scratch = []
    else:
        tq, n_q, n_keys, q0, k0 = TQ, SEQ // TQ, T_ALL, 0, 0
        scratch = [pltpu.VMEM((2, dv, TQ), F32),
                   pltpu.VMEM((2, TK, TQ), F32), pltpu.VMEM((2, TK, TQ), F32),
                   pltpu.VMEM((2, TK, TQ), BF16), pltpu.VMEM((2, TK, TQ), BF16)]
    return pl.pallas_call(
        functools.partial(kernel, ctx_only),
        grid=(n_heads, n_q),
        in_specs=in_blocks(tq, n_keys, q0, k0),
        out_specs=pl.BlockSpec((tq, LANES), lambda h, i: (i, h)),
        out_shape=jax.ShapeDtypeStruct((n_q * tq, out_width), BF16),
        scratch_shapes=scratch,
        compiler_params=_params(2),
        name=name + ("_ctx" if ctx_only else ""),
    )(*operands)


def _da_kernel(lam_init, ctx_only, qt_ref, k_ref, vt_ref, lam_ref, g_ref, o_ref, *scratch):
    qt = qt_ref[0]
    row = lax.broadcasted_iota(jnp.int32, qt.shape, 0)
    zero = jnp.zeros_like(qt)
    weights = [jnp.where(row < HEAD_DIM, qt, zero), jnp.where(row >= HEAD_DIM, qt, zero)]
    o1, o2 = _attend(ctx_only, k_ref, lambda s, n: vt_ref[0, :, pl.ds(s, n)], weights, scratch)
    lf = lam_ref[...]
    lam = (jnp.exp(jnp.sum(lf[0:1] * lf[1:2], axis=-1, keepdims=True))
           - jnp.exp(jnp.sum(lf[2:3] * lf[3:4], axis=-1, keepdims=True)) + lam_init)
    o = o1 - lam * o2
    ms = jnp.mean(o * o, axis=0, keepdims=True)
    on = (o * lax.rsqrt(ms + EPS)) * g_ref[...] * (1.0 - lam_init)
    o_ref[...] = on.T.astype(o_ref.dtype)


def _da_attention(qt, k, vt, da_lam, subln_g, lam_init, ctx_only):
    def in_blocks(tq, n_keys, q0, k0):
        return [
            pl.BlockSpec((1, LANES, tq), lambda h, i: (h, 0, i + q0)),
            pl.BlockSpec((n_keys, LANES), lambda h, i: (k0, h)),
            pl.BlockSpec((1, LANES, n_keys), lambda h, i: (h, 0, k0)),
            pl.BlockSpec((4, HEAD_DIM), lambda h, i: (0, 0)),
            pl.BlockSpec((LANES, 1), lambda h, i: (0, 0)),
        ]
    return _attn_call(functools.partial(_da_kernel, lam_init), ctx_only, DA_HEADS, LANES,
                      in_blocks, (qt, k, vt, da_lam, subln_g), DA_HEADS * LANES, "da_attention")


def _gqa_kernel(ctx_only, qt_ref, k_ref, vt_ref, o_ref, *scratch):
    g = pl.program_id(0)
    qt = qt_ref[0]
    zero = jnp.zeros((HEAD_DIM, qt.shape[1]), qt.dtype)
    weights = []
    for j in range(2):
        qj = qt[j * HEAD_DIM:(j + 1) * HEAD_DIM]
        weights.append(jnp.where(g == 0, jnp.concatenate([qj, zero], axis=0),
                                 jnp.concatenate([zero, qj], axis=0)))
    o0, o1 = _attend(ctx_only, k_ref, lambda s, n: vt_ref[:, pl.ds(s, n)], weights, scratch)
    o_ref[...] = jnp.concatenate([o0, o1], axis=0).T.astype(o_ref.dtype)


def _gqa_attention(qt, k, vt, ctx_only):
    def in_blocks(tq, n_keys, q0, k0):
        return [
            pl.BlockSpec((1, LANES, tq), lambda g, i: (g, 0, i + q0)),
            pl.BlockSpec((n_keys, LANES), lambda g, i: (k0, 0)),
            pl.BlockSpec((HEAD_DIM, n_keys), lambda g, i: (g, k0)),
        ]
    return _attn_call(_gqa_kernel, ctx_only, GQA_KV_HEADS, HEAD_DIM, in_blocks, (qt, k, vt),
                      GQA_KV_HEADS * LANES, "gqa_attention")


def _gelu_tanh(x):
    return 0.5 * x * (1.0 + jnp.tanh(math.sqrt(2.0 / math.pi) * (x + 0.044715 * (x * x * x))))


def _outproj_kernel(h_ref, mod_ref, a_ref, g_ref, hs_ref, gb_ref, w_ref, o_ref):
    mods = mod_ref[0]
    r = ((hs_ref[0] + hs_ref[1]) * _gelu_tanh(gb_ref[...])).astype(BF16)
    n_a = DA_HEADS * LANES
    n_g = n_a + GQA_KV_HEADS * LANES
    y = (jnp.dot(a_ref[...], w_ref[0:n_a, :], preferred_element_type=F32)
         + jnp.dot(g_ref[...], w_ref[n_a:n_g, :], preferred_element_type=F32)
         + jnp.dot(r, w_ref[n_g:, :], preferred_element_type=F32))
    o_ref[...] = h_ref[...] + mods[5:6] * y


def _outproj(h, mods, a, g, hs, gb, w_out, n_tiles):
    row_tile = lambda width: pl.BlockSpec((TM, width), lambda i: (i, 0))
    return pl.pallas_call(
        _outproj_kernel,
        grid=(n_tiles,),
        in_specs=[
            row_tile(D_MODEL),
            pl.BlockSpec((1, N_MOD, D_MODEL), lambda i: (i // N_LAT_TILES, 0, 0)),
            row_tile(DA_HEADS * LANES),
            row_tile(GQA_KV_HEADS * LANES),
            pl.BlockSpec((2, TM, LRU_WIDTH), lambda i: (0, i, 0)),
            row_tile(LRU_WIDTH),
            _resident((D_MODEL, D_MODEL)),
        ],
        out_specs=row_tile(D_MODEL),
        out_shape=jax.ShapeDtypeStruct((n_tiles * TM, D_MODEL), F32),
        compiler_params=_params(1),
        name="outproj",
    )(h, mods, a, g, hs, gb, w_out)


def _final_norm_kernel(h_ref, g_ref, o_ref):
    x = h_ref[...]
    ms = jnp.mean(x * x, axis=-1, keepdims=True)
    o_ref[...] = (x * lax.rsqrt(ms + EPS)) * g_ref[...]


def _final_norm(h, g):
    return pl.pallas_call(
        _final_norm_kernel,
        grid=(N_LAT_TILES,),
        in_specs=[pl.BlockSpec((TM, D_MODEL), lambda i: (i, 0)),
                  pl.BlockSpec((1, D_MODEL), lambda i: (0, 0))],
        out_specs=pl.BlockSpec((TM, D_MODEL), lambda i: (i, 0)),
        out_shape=jax.ShapeDtypeStruct((SEQ, D_MODEL), F32),
        compiler_params=_params(1),
        name="final_norm",
    )(h, g)


def _rope_tables():
    half = HEAD_DIM // 4
    t = jnp.arange(SEQ, dtype=jnp.int32)
    row = (t // GRID_W).astype(F32)
    col = (t % GRID_W).astype(F32)
    inv = ROPE_THETA ** (-jnp.arange(half, dtype=F32) / half)
    ang_r = row[:, None] * inv
    ang_c = col[:, None] * inv
    cos64 = jnp.concatenate([jnp.cos(ang_r)] * 2 + [jnp.cos(ang_c)] * 2, axis=-1)
    sin64 = jnp.concatenate([-jnp.sin(ang_r), jnp.sin(ang_r), -jnp.sin(ang_c), jnp.sin(ang_c)],
                            axis=-1)
    cos_t = jnp.concatenate([jnp.tile(cos64, (1, 2)), jnp.ones((CTX_LEN, LANES), F32)], axis=0)
    sin_t = jnp.concatenate([jnp.tile(sin64, (1, 2)), jnp.zeros((CTX_LEN, LANES), F32)], axis=0)
    return cos_t, sin_t


def _block_diag(w):
    eye = jnp.eye(LRU_BLOCKS, dtype=w.dtype)
    full = jnp.einsum('dkij,kl->dkilj', w, eye)
    return full.reshape(2, LRU_WIDTH, LRU_WIDTH).astype(BF16)


def _chunk_cols(w):
    return w.reshape(D_MODEL, N_FF_CHUNKS, FF_CHUNK).transpose(1, 0, 2).astype(BF16)


def kernel(x, c, ctx, c_ctx, ada_w, ada_b, norm_g, ffn1_w13, ffn1_w2, ffn2_w13, ffn2_w2,
           w_in, w_out, da_lam, da_subln_g, qk_norm_g, lru_conv_w, lru_conv_b,
           lru_wa, lru_ba, lru_wi, lru_bi, lru_lambda, final_g):
    assert x.shape == (1, SEQ, D_MODEL) and ctx.shape == (1, CTX_LEN, D_MODEL)
    cc = jnp.zeros((8, D_MODEL), F32).at[0].set(c[0]).at[1].set(c_ctx)
    mods_all = _ada_mods(cc, ada_w, ada_b).reshape(DEPTH, 8, N_MOD, D_MODEL)[:, :2]
    cos_t, sin_t = _rope_tables()

    h = jnp.concatenate([x[0], ctx[0]], axis=0)
    for l in range(DEPTH):
        last = l == DEPTH - 1
        lam_init = 0.8 - 0.6 * math.exp(-0.3 * l)
        mods = mods_all[l]

        def ffn_weights(w13, w2):
            return (_chunk_cols(w13[l][:, :D_FF]), _chunk_cols(w13[l][:, D_FF:]),
                    w2[l].reshape(N_FF_CHUNKS, FF_CHUNK, D_MODEL).astype(BF16))

        h = _ffn(h, mods, norm_g[l, 0:1], *ffn_weights(ffn1_w13, ffn1_w2), 0, N_TILES)

        qk_gain = jnp.tile(qk_norm_g[l], (1, 2))
        qt_da, k_da, vt_da, qt_g, k_g, vt_g, xb, gb = _inproj(
            h, mods, norm_g[l, 1:2], w_in[l].astype(BF16), cos_t, sin_t, qk_gain)

        hs = _lru(xb, lru_conv_w[l], lru_conv_b[l][None, :], _block_diag(lru_wa[l]),
                  _block_diag(lru_wi[l]), lru_ba[l][:, None, :], lru_bi[l][:, None, :],
                  lru_lambda[l][:, None, :])

        n_tiles = N_LAT_TILES if last else N_TILES
        da_args = (qt_da, k_da, vt_da, da_lam[l], da_subln_g[l][:, None], lam_init)
        a = _da_attention(*da_args, False)
        g = _gqa_attention(qt_g, k_g, vt_g, False)
        if not last:
            a = jnp.concatenate([a, _da_attention(*da_args, True)], axis=0)
            g = jnp.concatenate([g, _gqa_attention(qt_g, k_g, vt_g, True)], axis=0)

        h = _outproj(h, mods, a, g, hs, gb, w_out[l].astype(BF16), n_tiles)
        h = _ffn(h, mods, norm_g[l, 2:3], *ffn_weights(ffn2_w13, ffn2_w2), 6, n_tiles)
    return _final_norm(h, final_g[None, :])[None]
```

```python
import functools
import math

import jax
import jax.numpy as jnp
from jax import lax
from jax.experimental import pallas as pl
from jax.experimental.pallas import tpu as pltpu

D_MODEL = 1024
SEQ = 16384
CTX_LEN = 256
T_ALL = SEQ + CTX_LEN
DEPTH = 4
GRID_W = 64
HEAD_DIM = 64
SCALE = HEAD_DIM ** -0.5
LOG2E = math.log2(math.e)
ROPE_THETA = 10000.0
DA_HEADS = 4
GQA_KV_HEADS = 2
LRU_WIDTH = 256
LRU_BLOCKS = 4
LRU_BLOCK = LRU_WIDTH // LRU_BLOCKS
LRU_C = 8.0
IN_WIDTH = 2560
D_FF = 2816
N_MOD = 9
EPS = 1e-6

LANES = 128
MXU_DIM = 256

TM = 256
N_TILES = T_ALL // TM
N_LAT_TILES = SEQ // TM
FF_CHUNK = MXU_DIM
N_FF_CHUNKS = D_FF // FF_CHUNK
TQ = 1024
TK = 640
N_K_CHUNKS = T_ALL // TK
NEG_BIG = -1e30
VMEM_LIMIT = 48 * 1024 * 1024

F32 = jnp.float32
BF16 = jnp.bfloat16


def _params(n_axes):
    return pltpu.CompilerParams(dimension_semantics=("arbitrary",) * n_axes,
                                vmem_limit_bytes=VMEM_LIMIT)


def _resident(shape):
    zeros = (0,) * len(shape)
    return pl.BlockSpec(shape, lambda *_: zeros, pipeline_mode=pl.Buffered(1))


def _sigmoid(x):
    return 1.0 / (1.0 + jnp.exp(-x))


def _norm_mod(x, g, shift, scale):
    ms = jnp.mean(x * x, axis=-1, keepdims=True)
    y = (x * lax.rsqrt(ms + EPS)) * g
    return y * (1.0 + scale) + shift


ADA_COLS = 1152


def _ada_kernel(c_ref, w_ref, b_ref, o_ref):
    c = c_ref[...]
    s = (c * _sigmoid(c)).astype(BF16)
    o_ref[0] = jnp.dot(s, w_ref[0].astype(BF16), preferred_element_type=F32) + b_ref[0]


def _ada_mods(cc, ada_w, ada_b):
    width = N_MOD * D_MODEL
    return pl.pallas_call(
        _ada_kernel,
        grid=(DEPTH, width // ADA_COLS),
        in_specs=[
            pl.BlockSpec((8, D_MODEL), lambda l, j: (0, 0)),
            pl.BlockSpec((1, D_MODEL, ADA_COLS), lambda l, j: (l, 0, j)),
            pl.BlockSpec((1, 1, ADA_COLS), lambda l, j: (l, 0, j)),
        ],
        out_specs=pl.BlockSpec((1, 8, ADA_COLS), lambda l, j: (l, 0, j)),
        out_shape=jax.ShapeDtypeStruct((DEPTH, 8, width), F32),
        compiler_params=_params(2),
        name="ada_mods",
    )(cc, ada_w, ada_b.reshape(DEPTH, 1, width))


def _ffn_kernel(i0, h_ref, mod_ref, g_ref, w1_ref, w3_ref, w2_ref, o_ref, acc_ref):
    x = h_ref[...]
    mods = mod_ref[0]
    z = _norm_mod(x, g_ref[...], mods[i0:i0 + 1], mods[i0 + 1:i0 + 2]).astype(BF16)
    acc_ref[...] = jnp.zeros_like(acc_ref)

    def chunk(c, carry):
        a = jnp.dot(z, w1_ref[c], preferred_element_type=F32)
        b = jnp.dot(z, w3_ref[c], preferred_element_type=F32)
        gated = (a * _sigmoid(a) * b).astype(BF16)
        acc_ref[...] += jnp.dot(gated, w2_ref[c], preferred_element_type=F32)
        return carry

    lax.fori_loop(0, N_FF_CHUNKS, chunk, 0)
    o_ref[...] = x + (0.5 * mods[i0 + 2:i0 + 3]) * acc_ref[...]


def _ffn(h, mods, g, w1c, w3c, w2c, i0, n_tiles):
    return pl.pallas_call(
        functools.partial(_ffn_kernel, i0),
        grid=(n_tiles,),
        in_specs=[
            pl.BlockSpec((TM, D_MODEL), lambda i: (i, 0)),
            pl.BlockSpec((1, N_MOD, D_MODEL), lambda i: (i // N_LAT_TILES, 0, 0)),
            pl.BlockSpec((1, D_MODEL), lambda i: (0, 0)),
            _resident((N_FF_CHUNKS, D_MODEL, FF_CHUNK)),
            _resident((N_FF_CHUNKS, D_MODEL, FF_CHUNK)),
            _resident((N_FF_CHUNKS, FF_CHUNK, D_MODEL)),
        ],
        out_specs=pl.BlockSpec((TM, D_MODEL), lambda i: (i, 0)),
        out_shape=jax.ShapeDtypeStruct((n_tiles * TM, D_MODEL), F32),
        scratch_shapes=[pltpu.VMEM((TM, D_MODEL), F32)],
        compiler_params=_params(1),
        name="ffn",
    )(h, mods, g, w1c, w3c, w2c)


def _swap16(x):
    lane = lax.broadcasted_iota(jnp.int32, x.shape, 1)
    return jnp.where(lane % 32 < 16, pltpu.roll(x, LANES - 16, 1), pltpu.roll(x, 16, 1))


def _rope(x, cos, sin):
    return x * cos + _swap16(x) * sin


def _head_mean_sq(x):
    sq = x * x
    hi = sq.astype(BF16)
    lo = (sq - hi.astype(F32)).astype(BF16)
    r = lax.broadcasted_iota(jnp.int32, (LANES, LANES), 0) // HEAD_DIM
    c = lax.broadcasted_iota(jnp.int32, (LANES, LANES), 1) // HEAD_DIM
    ones = jnp.where(r == c, 1.0, 0.0).astype(BF16)
    tot = (jnp.dot(hi, ones, preferred_element_type=F32)
           + jnp.dot(lo, ones, preferred_element_type=F32))
    return tot * (1.0 / HEAD_DIM)


def _inproj_kernel(h_ref, mod_ref, g_ref, w_ref, cos_ref, sin_ref, qkg_ref,
                   qt_da_ref, k_da_ref, vt_da_ref, qt_g_ref, k_g_ref, vt_g_ref, xb_ref, gb_ref):
    mods = mod_ref[0]
    z = _norm_mod(h_ref[...], g_ref[...], mods[3:4], mods[4:5]).astype(BF16)
    proj = jnp.dot(z, w_ref[...], preferred_element_type=F32)
    cos = cos_ref[...]
    sin = sin_ref[...]
    q_scale = SCALE * LOG2E

    def slab(j):
        return proj[:, j * LANES:(j + 1) * LANES]

    for h in range(DA_HEADS):
        q = _rope(slab(h), cos, sin) * q_scale
        qt_da_ref[h] = q.T.astype(BF16)
        k_da_ref[:, h * LANES:(h + 1) * LANES] = _rope(slab(4 + h), cos, sin).astype(BF16)
        vt_da_ref[h] = slab(8 + h).T.astype(BF16)

    gq_gain = qkg_ref[0:1, :]
    gk_gain = qkg_ref[1:2, :]
    for g in range(GQA_KV_HEADS):
        q = slab(12 + g)
        q = (q * lax.rsqrt(_head_mean_sq(q) + EPS)) * gq_gain
        qt_g_ref[g] = (_rope(q, cos, sin) * q_scale).T.astype(BF16)
    k = slab(14)
    k = (k * lax.rsqrt(_head_mean_sq(k) + EPS)) * gk_gain
    k_g_ref[...] = _rope(k, cos, sin).astype(BF16)
    vt_g_ref[...] = slab(15).T.astype(BF16)
    xb_ref[...] = proj[:, 2048:2304]
    gb_ref[...] = proj[:, 2304:2560]


def _inproj(h, mods, g, w_in, cos_t, sin_t, qk_gain):
    row_tile = lambda width: pl.BlockSpec((TM, width), lambda i: (i, 0))
    return pl.pallas_call(
        _inproj_kernel,
        grid=(N_TILES,),
        in_specs=[
            row_tile(D_MODEL),
            pl.BlockSpec((1, N_MOD, D_MODEL), lambda i: (i // N_LAT_TILES, 0, 0)),
            pl.BlockSpec((1, D_MODEL), lambda i: (0, 0)),
            _resident((D_MODEL, IN_WIDTH)),
            row_tile(LANES),
            row_tile(LANES),
            pl.BlockSpec((2, LANES), lambda i: (0, 0)),
        ],
        out_specs=[
            pl.BlockSpec((DA_HEADS, LANES, TM), lambda i: (0, 0, i)),
            row_tile(DA_HEADS * LANES),
            pl.BlockSpec((DA_HEADS, LANES, TM), lambda i: (0, 0, i)),
            pl.BlockSpec((GQA_KV_HEADS, LANES, TM), lambda i: (0, 0, i)),
            row_tile(LANES),
            pl.BlockSpec((LANES, TM), lambda i: (0, i)),
            row_tile(LRU_WIDTH),
            row_tile(LRU_WIDTH),
        ],
        out_shape=[
            jax.ShapeDtypeStruct((DA_HEADS, LANES, T_ALL), BF16),
            jax.ShapeDtypeStruct((T_ALL, DA_HEADS * LANES), BF16),
            jax.ShapeDtypeStruct((DA_HEADS, LANES, T_ALL), BF16),
            jax.ShapeDtypeStruct((GQA_KV_HEADS, LANES, T_ALL), BF16),
            jax.ShapeDtypeStruct((T_ALL, LANES), BF16),
            jax.ShapeDtypeStruct((LANES, T_ALL), BF16),
            jax.ShapeDtypeStruct((T_ALL, LRU_WIDTH), F32),
            jax.ShapeDtypeStruct((T_ALL, LRU_WIDTH), F32),
        ],
        compiler_params=_params(1),
        name="inproj",
    )(h, mods, g, w_in, cos_t, sin_t, qk_gain)


def _lru_tile_index(d, j):
    lat = jnp.where(d == 0, j - 1, N_LAT_TILES - j)
    return jnp.where(j == 0, N_LAT_TILES, lat)


def _lru_kernel(x_ref, cw_ref, cb_ref, wa_ref, wi_ref, ba_ref, bi_ref, lam_ref, o_ref, carry_ref):
    d = pl.program_id(0)
    j = pl.program_id(1)
    ti = _lru_tile_index(d, j)
    t0 = pl.multiple_of(ti * TM, TM)

    @pl.when(j == 0)
    def _():
        carry_ref[...] = jnp.zeros_like(carry_ref)

    x = x_ref[pl.ds(t0, TM), :]
    prev8 = x_ref[pl.ds(pl.multiple_of(jnp.maximum(t0 - 8, 0), 8), 8), :]
    next8 = x_ref[pl.ds(pl.multiple_of(jnp.minimum(t0 + TM, T_ALL - 8), 8), 8), :]
    has_prev = jnp.logical_and(ti != 0, ti != N_LAT_TILES)
    has_next = jnp.logical_and(ti != N_LAT_TILES - 1, ti != N_LAT_TILES)
    prev_row = jnp.where(has_prev, prev8[7:8], 0.0)
    next_row0 = jnp.where(has_next, next8[0:1], 0.0)
    next_row1 = jnp.where(has_next, next8[1:2], 0.0)
    row = lax.broadcasted_iota(jnp.int32, (TM, LRU_WIDTH), 0)
    xm1 = jnp.where(row == 0, prev_row, pltpu.roll(x, 1, 0))
    xp1 = jnp.where(row == TM - 1, next_row0, pltpu.roll(x, TM - 1, 0))
    xp2 = jnp.where(row == TM - 2, next_row0,
                    jnp.where(row == TM - 1, next_row1, pltpu.roll(x, TM - 2, 0)))
    cw = cw_ref[...]
    xc = xm1 * cw[0:1] + x * cw[1:2] + xp1 * cw[2:3] + xp2 * cw[3:4] + cb_ref[...]

    xcb = xc.astype(BF16)
    r = _sigmoid(jnp.dot(xcb, wa_ref[0], preferred_element_type=F32) + ba_ref[0])
    gi = _sigmoid(jnp.dot(xcb, wi_ref[0], preferred_element_type=F32) + bi_ref[0])
    neg_lam = -lam_ref[0]
    softplus = jnp.maximum(neg_lam, 0.0) + jnp.log1p(jnp.exp(-jnp.abs(neg_lam)))
    log_a = (-LRU_C) * r * softplus
    a = jnp.exp(log_a)
    u = jnp.sqrt(-jnp.tanh(log_a) * (a * a + 1.0)) * (gi * xc)

    def scan(A, B, forward):
        k = 1
        while k < TM:
            if forward:
                keep = row >= k
                As = jnp.where(keep, pltpu.roll(A, k, 0), 1.0)
                Bs = jnp.where(keep, pltpu.roll(B, k, 0), 0.0)
            else:
                keep = row < TM - k
                As = jnp.where(keep, pltpu.roll(A, TM - k, 0), 1.0)
                Bs = jnp.where(keep, pltpu.roll(B, TM - k, 0), 0.0)
            B = A * Bs + B
            A = A * As
            k *= 2
        return A, B

    carry = carry_ref[0:1, :]

    @pl.when(d == 0)
    def _():
        A, B = scan(a, u, True)
        hs = B + A * carry
        o_ref[0] = hs
        carry_ref[0:1, :] = hs[TM - 1:TM]

    @pl.when(d == 1)
    def _():
        A, B = scan(a, u, False)
        hs = B + A * carry
        o_ref[0] = hs
        carry_ref[0:1, :] = hs[0:1]


def _lru(xb, conv_w, conv_b, wa, wi, ba, bi, lam):
    per_dir = lambda *tail: pl.BlockSpec((1,) + tail, lambda d, j: (d,) + (0,) * len(tail))
    return pl.pallas_call(
        _lru_kernel,
        grid=(2, N_TILES),
        in_specs=[
            _resident((T_ALL, LRU_WIDTH)),
            pl.BlockSpec((4, LRU_WIDTH), lambda d, j: (0, 0)),
            pl.BlockSpec((1, LRU_WIDTH), lambda d, j: (0, 0)),
            per_dir(LRU_WIDTH, LRU_WIDTH),
            per_dir(LRU_WIDTH, LRU_WIDTH),
            per_dir(1, LRU_WIDTH),
            per_dir(1, LRU_WIDTH),
            per_dir(1, LRU_WIDTH),
        ],
        out_specs=pl.BlockSpec((1, TM, LRU_WIDTH), lambda d, j: (d, _lru_tile_index(d, j), 0)),
        out_shape=jax.ShapeDtypeStruct((2, T_ALL, LRU_WIDTH), F32),
        scratch_shapes=[pltpu.VMEM((8, LRU_WIDTH), F32)],
        compiler_params=_params(2),
        name="rglru",
    )(xb, conv_w, conv_b, wa, wi, ba, bi, lam)


def _softmax_chunk(s, m_old, l_old):
    m_new = jnp.maximum(m_old, jnp.max(s, axis=0, keepdims=True))
    alpha = jnp.exp2(m_old - m_new)
    p = jnp.exp2(s - m_new)
    l_new = alpha * l_old + jnp.sum(p, axis=0, keepdims=True)
    return p.astype(BF16), m_new, l_new, alpha


def _attend_ctx(kc, vc, weights):
    outs = []
    for w in weights:
        s = jnp.dot(kc, w, preferred_element_type=F32)
        p, _, l, _ = _softmax_chunk(s, jnp.full((1, CTX_LEN), NEG_BIG, F32),
                                    jnp.zeros((1, CTX_LEN), F32))
        outs.append(jnp.dot(vc, p, preferred_element_type=F32) * (1.0 / l))
    return outs


def _attend_lat(k_ref, v_chunk, weights, acc_ref, s_refs, p_refs):
    maps = range(len(weights))

    def scores(c, s_ref):
        kc = k_ref[pl.ds(pl.multiple_of(c * TK, TK), TK), :]
        for j in maps:
            s_ref[j] = jnp.dot(kc, weights[j], preferred_element_type=F32)

    def softmax(s_ref, p_ref, m, l):
        m_new, l_new, alpha = [], [], []
        for j in maps:
            m_j = jnp.maximum(m[j], jnp.max(s_ref[j], axis=0, keepdims=True))
            a_j = jnp.exp2(m[j] - m_j)
            p = jnp.exp2(s_ref[j] - m_j)
            l_new.append(a_j * l[j] + jnp.sum(p, axis=0, keepdims=True))
            p_ref[j] = p.astype(BF16)
            m_new.append(m_j)
            alpha.append(a_j)
        return m_new, l_new, alpha

    def values(c, p_ref, alpha):
        vc = v_chunk(pl.multiple_of(c * TK, TK), TK)
        for j in maps:
            acc_ref[j] = alpha[j] * acc_ref[j] + jnp.dot(vc, p_ref[j], preferred_element_type=F32)

    acc_ref[...] = jnp.zeros_like(acc_ref)
    p_refs[1][...] = jnp.zeros_like(p_refs[1])
    scores(0, s_refs[0])
    stat = lambda v: [jnp.full((1, TQ), v, F32) for _ in maps]

    def chunk_pair(i, carry):
        m, l, alpha_b = carry
        c0 = 2 * i
        scores(c0 + 1, s_refs[1])
        m, l, alpha_a = softmax(s_refs[0], p_refs[0], m, l)
        values(jnp.maximum(c0 - 1, 0), p_refs[1], alpha_b)
        scores(jnp.minimum(c0 + 2, N_K_CHUNKS - 1), s_refs[0])
        m, l, alpha_b = softmax(s_refs[1], p_refs[1], m, l)
        values(c0, p_refs[0], alpha_a)
        return m, l, alpha_b

    _, l, alpha_b = lax.fori_loop(0, N_K_CHUNKS // 2, chunk_pair,
                                  (stat(NEG_BIG), stat(0.0), stat(1.0)))
    values(N_K_CHUNKS - 1, p_refs[1], alpha_b)
    return [acc_ref[j] * (1.0 / l[j]) for j in maps]


def _attend(ctx_only, k_ref, vt_at, weights, scratch):
    if ctx_only:
        return _attend_ctx(k_ref[...], vt_at(0, CTX_LEN), weights)
    acc_ref, s0_ref, s1_ref, p0_ref, p1_ref = scratch
    return _attend_lat(k_ref, vt_at, weights, acc_ref, (s0_ref, s1_ref), (p0_ref, p1_ref))


def _attn_call(kernel, ctx_only, n_heads, dv, in_blocks, operands, out_width, name):
    if ctx_only:
        tq, n_q, n_keys, q0, k0 = CTX_LEN, 1, CTX_LEN, SEQ // CTX_LEN, SEQ // CTX_LEN
        scratch = []
    else:
        tq, n_q, n_keys, q0, k0 = TQ, SEQ // TQ, T_ALL, 0, 0
        scratch = [pltpu.VMEM((2, dv, TQ), F32),
                   pltpu.VMEM((2, TK, TQ), F32), pltpu.VMEM((2, TK, TQ), F32),
                   pltpu.VMEM((2, TK, TQ), BF16), pltpu.VMEM((2, TK, TQ), BF16)]
    return pl.pallas_call(
        functools.partial(kernel, ctx_only),
        grid=(n_heads, n_q),
        in_specs=in_blocks(tq, n_keys, q0, k0),
        out_specs=pl.BlockSpec((tq, LANES), lambda h, i: (i, h)),
        out_shape=jax.ShapeDtypeStruct((n_q * tq, out_width), BF16),
        scratch_shapes=scratch,
        compiler_params=_params(2),
        name=name + ("_ctx" if ctx_only else ""),
    )(*operands)


def _da_kernel(lam_init, ctx_only, qt_ref, k_ref, vt_ref, lam_ref, g_ref, o_ref, *scratch):
    qt = qt_ref[0]
    row = lax.broadcasted_iota(jnp.int32, qt.shape, 0)
    zero = jnp.zeros_like(qt)
    weights = [jnp.where(row < HEAD_DIM, qt, zero), jnp.where(row >= HEAD_DIM, qt, zero)]
    o1, o2 = _attend(ctx_only, k_ref, lambda s, n: vt_ref[0, :, pl.ds(s, n)], weights, scratch)
    lf = lam_ref[...]
    lam = (jnp.exp(jnp.sum(lf[0:1] * lf[1:2], axis=-1, keepdims=True))
           - jnp.exp(jnp.sum(lf[2:3] * lf[3:4], axis=-1, keepdims=True)) + lam_init)
    o = o1 - lam * o2
    ms = jnp.mean(o * o, axis=0, keepdims=True)
    on = (o * lax.rsqrt(ms + EPS)) * g_ref[...] * (1.0 - lam_init)
    o_ref[...] = on.T.astype(o_ref.dtype)


def _da_attention(qt, k, vt, da_lam, subln_g, lam_init, ctx_only):
    def in_blocks(tq, n_keys, q0, k0):
        return [
            pl.BlockSpec((1, LANES, tq), lambda h, i: (h, 0, i + q0)),
            pl.BlockSpec((n_keys, LANES), lambda h, i: (k0, h)),
            pl.BlockSpec((1, LANES, n_keys), lambda h, i: (h, 0, k0)),
            pl.BlockSpec((4, HEAD_DIM), lambda h, i: (0, 0)),
            pl.BlockSpec((LANES, 1), lambda h, i: (0, 0)),
        ]
    return _attn_call(functools.partial(_da_kernel, lam_init), ctx_only, DA_HEADS, LANES,
                      in_blocks, (qt, k, vt, da_lam, subln_g), DA_HEADS * LANES, "da_attention")


def _gqa_kernel(ctx_only, qt_ref, k_ref, vt_ref, o_ref, *scratch):
    g = pl.program_id(0)
    qt = qt_ref[0]
    zero = jnp.zeros((HEAD_DIM, qt.shape[1]), qt.dtype)
    weights = []
    for j in range(2):
        qj = qt[j * HEAD_DIM:(j + 1) * HEAD_DIM]
        weights.append(jnp.where(g == 0, jnp.concatenate([qj, zero], axis=0),
                                 jnp.concatenate([zero, qj], axis=0)))
    o0, o1 = _attend(ctx_only, k_ref, lambda s, n: vt_ref[:, pl.ds(s, n)], weights, scratch)
    o_ref[...] = jnp.concatenate([o0, o1], axis=0).T.astype(o_ref.dtype)


def _gqa_attention(qt, k, vt, ctx_only):
    def in_blocks(tq, n_keys, q0, k0):
        return [
            pl.BlockSpec((1, LANES, tq), lambda g, i: (g, 0, i + q0)),
            pl.BlockSpec((n_keys, LANES), lambda g, i: (k0, 0)),
            pl.BlockSpec((HEAD_DIM, n_keys), lambda g, i: (g, k0)),
        ]
    return _attn_call(_gqa_kernel, ctx_only, GQA_KV_HEADS, HEAD_DIM, in_blocks, (qt, k, vt),
                      GQA_KV_HEADS * LANES, "gqa_attention")


def _gelu_tanh(x):
    return 0.5 * x * (1.0 + jnp.tanh(math.sqrt(2.0 / math.pi) * (x + 0.044715 * (x * x * x))))


def _outproj_kernel(h_ref, mod_ref, a_ref, g_ref, hs_ref, gb_ref, w_ref, o_ref):
    mods = mod_ref[0]
    r = ((hs_ref[0] + hs_ref[1]) * _gelu_tanh(gb_ref[...])).astype(BF16)
    n_a = DA_HEADS * LANES
    n_g = n_a + GQA_KV_HEADS * LANES
    y = (jnp.dot(a_ref[...], w_ref[0:n_a, :], preferred_element_type=F32)
         + jnp.dot(g_ref[...], w_ref[n_a:n_g, :], preferred_element_type=F32)
         + jnp.dot(r, w_ref[n_g:, :], preferred_element_type=F32))
    o_ref[...] = h_ref[...] + mods[5:6] * y


def _outproj(h, mods, a, g, hs, gb, w_out, n_tiles):
    row_tile = lambda width: pl.BlockSpec((TM, width), lambda i: (i, 0))
    return pl.pallas_call(
        _outproj_kernel,
        grid=(n_tiles,),
        in_specs=[
            row_tile(D_MODEL),
            pl.BlockSpec((1, N_MOD, D_MODEL), lambda i: (i // N_LAT_TILES, 0, 0)),
            row_tile(DA_HEADS * LANES),
            row_tile(GQA_KV_HEADS * LANES),
            pl.BlockSpec((2, TM, LRU_WIDTH), lambda i: (0, i, 0)),
            row_tile(LRU_WIDTH),
            _resident((D_MODEL, D_MODEL)),
        ],
        out_specs=row_tile(D_MODEL),
        out_shape=jax.ShapeDtypeStruct((n_tiles * TM, D_MODEL), F32),
        compiler_params=_params(1),
        name="outproj",
    )(h, mods, a, g, hs, gb, w_out)


def _final_norm_kernel(h_ref, g_ref, o_ref):
    x = h_ref[...]
    ms = jnp.mean(x * x, axis=-1, keepdims=True)
    o_ref[...] = (x * lax.rsqrt(ms + EPS)) * g_ref[...]


def _final_norm(h, g):
    return pl.pallas_call(
        _final_norm_kernel,
        grid=(N_LAT_TILES,),
        in_specs=[pl.BlockSpec((TM, D_MODEL), lambda i: (i, 0)),
                  pl.BlockSpec((1, D_MODEL), lambda i: (0, 0))],
        out_specs=pl.BlockSpec((TM, D_MODEL), lambda i: (i, 0)),
        out_shape=jax.ShapeDtypeStruct((SEQ, D_MODEL), F32),
        compiler_params=_params(1),
        name="final_norm",
    )(h, g)


def _rope_tables():
    half = HEAD_DIM // 4
    t = jnp.arange(SEQ, dtype=jnp.int32)
    row = (t // GRID_W).astype(F32)
    col = (t % GRID_W).astype(F32)
    inv = ROPE_THETA ** (-jnp.arange(half, dtype=F32) / half)
    ang_r = row[:, None] * inv
    ang_c = col[:, None] * inv
    cos64 = jnp.concatenate([jnp.cos(ang_r)] * 2 + [jnp.cos(ang_c)] * 2, axis=-1)
    sin64 = jnp.concatenate([-jnp.sin(ang_r), jnp.sin(ang_r), -jnp.sin(ang_c), jnp.sin(ang_c)],
                            axis=-1)
    cos_t = jnp.concatenate([jnp.tile(cos64, (1, 2)), jnp.ones((CTX_LEN, LANES), F32)], axis=0)
    sin_t = jnp.concatenate([jnp.tile(sin64, (1, 2)), jnp.zeros((CTX_LEN, LANES), F32)], axis=0)
    return cos_t, sin_t


def _block_diag(w):
    eye = jnp.eye(LRU_BLOCKS, dtype=w.dtype)
    full = jnp.einsum('dkij,kl->dkilj', w, eye)
    return full.reshape(2, LRU_WIDTH, LRU_WIDTH).astype(BF16)


def _chunk_cols(w):
    return w.reshape(D_MODEL, N_FF_CHUNKS, FF_CHUNK).transpose(1, 0, 2).astype(BF16)


def kernel(x, c, ctx, c_ctx, ada_w, ada_b, norm_g, ffn1_w13, ffn1_w2, ffn2_w13, ffn2_w2,
           w_in, w_out, da_lam, da_subln_g, qk_norm_g, lru_conv_w, lru_conv_b,
           lru_wa, lru_ba, lru_wi, lru_bi, lru_lambda, final_g):
    assert x.shape == (1, SEQ, D_MODEL) and ctx.shape == (1, CTX_LEN, D_MODEL)
    cc = jnp.zeros((8, D_MODEL), F32).at[0].set(c[0]).at[1].set(c_ctx)
    mods_all = _ada_mods(cc, ada_w, ada_b).reshape(DEPTH, 8, N_MOD, D_MODEL)[:, :2]
    cos_t, sin_t = _rope_tables()

    h = jnp.concatenate([x[0], ctx[0]], axis=0)
    for l in range(DEPTH):
        last = l == DEPTH - 1
        lam_init = 0.8 - 0.6 * math.exp(-0.3 * l)
        mods = mods_all[l]

        def ffn_weights(w13, w2):
            return (_chunk_cols(w13[l][:, :D_FF]), _chunk_cols(w13[l][:, D_FF:]),
                    w2[l].reshape(N_FF_CHUNKS, FF_CHUNK, D_MODEL).astype(BF16))

        h = _ffn(h, mods, norm_g[l, 0:1], *ffn_weights(ffn1_w13, ffn1_w2), 0, N_TILES)

        qk_gain = jnp.tile(qk_norm_g[l], (1, 2))
        qt_da, k_da, vt_da, qt_g, k_g, vt_g, xb, gb = _inproj(
            h, mods, norm_g[l, 1:2], w_in[l].astype(BF16), cos_t, sin_t, qk_gain)

        hs = _lru(xb, lru_conv_w[l], lru_conv_b[l][None, :], _block_diag(lru_wa[l]),
                  _block_diag(lru_wi[l]), lru_ba[l][:, None, :], lru_bi[l][:, None, :],
                  lru_lambda[l][:, None, :])

        n_tiles = N_LAT_TILES if last else N_TILES
        da_args = (qt_da, k_da, vt_da, da_lam[l], da_subln_g[l][:, None], lam_init)
        a = _da_attention(*da_args, False)
        g = _gqa_attention(qt_g, k_g, vt_g, False)
        if not last:
            a = jnp.concatenate([a, _da_attention(*da_args, True)], axis=0)
            g = jnp.concatenate([g, _gqa_attention(qt_g, k_g, vt_g, True)], axis=0)

        h = _outproj(h, mods, a, g, hs, gb, w_out[l].astype(BF16), n_tiles)
        h = _ffn(h, mods, norm_g[l, 2:3], *ffn_weights(ffn2_w13, ffn2_w2), 6, n_tiles)
    return _final_norm(h, final_g[None, :])[None]
```

```python
import functools
import math

import jax
import jax.numpy as jnp
from jax import lax
from jax.experimental import pallas as pl
from jax.experimental.pallas import tpu as pltpu

D_MODEL = 1024
SEQ = 16384
CTX_LEN = 256
T_ALL = SEQ + CTX_LEN
DEPTH = 4
GRID_W = 64
HEAD_DIM = 64
SCALE = HEAD_DIM ** -0.5
LOG2E = math.log2(math.e)
ROPE_THETA = 10000.0
DA_HEADS = 4
GQA_KV_HEADS = 2
LRU_WIDTH = 256
LRU_BLOCKS = 4
LRU_BLOCK = LRU_WIDTH // LRU_BLOCKS
LRU_C = 8.0
IN_WIDTH = 2560
D_FF = 2816
N_MOD = 9
EPS = 1e-6

LANES = 128
MXU_DIM = 256

TM = 256
N_TILES = T_ALL // TM
N_LAT_TILES = SEQ // TM
FF_CHUNK = MXU_DIM
N_FF_CHUNKS = D_FF // FF_CHUNK
TQ = 1024
TK = 640
N_K_CHUNKS = T_ALL // TK
NEG_BIG = -1e30
SHIFT_GUARD = 32.0
BOUND_SLACK = 1.02
L_MIN = 2.0 ** -60
L_MAX = 2.0 ** 100
VMEM_LIMIT = 48 * 1024 * 1024

F32 = jnp.float32
BF16 = jnp.bfloat16


def _params(n_axes):
    return pltpu.CompilerParams(dimension_semantics=("arbitrary",) * n_axes,
                                vmem_limit_bytes=VMEM_LIMIT)


def _resident(shape):
    zeros = (0,) * len(shape)
    return pl.BlockSpec(shape, lambda *_: zeros, pipeline_mode=pl.Buffered(1))


def _sigmoid(x):
    return 1.0 / (1.0 + jnp.exp(-x))


def _norm_mod(x, g, shift, scale):
    ms = jnp.mean(x * x, axis=-1, keepdims=True)
    y = (x * lax.rsqrt(ms + EPS)) * g
    return y * (1.0 + scale) + shift


ADA_COLS = 1152


def _ada_kernel(c_ref, w_ref, b_ref, o_ref):
    c = c_ref[...]
    s = (c * _sigmoid(c)).astype(BF16)
    o_ref[0] = jnp.dot(s, w_ref[0].astype(BF16), preferred_element_type=F32) + b_ref[0]


def _ada_mods(cc, ada_w, ada_b):
    width = N_MOD * D_MODEL
    return pl.pallas_call(
        _ada_kernel,
        grid=(DEPTH, width // ADA_COLS),
        in_specs=[
            pl.BlockSpec((8, D_MODEL), lambda l, j: (0, 0)),
            pl.BlockSpec((1, D_MODEL, ADA_COLS), lambda l, j: (l, 0, j)),
            pl.BlockSpec((1, 1, ADA_COLS), lambda l, j: (l, 0, j)),
        ],
        out_specs=pl.BlockSpec((1, 8, ADA_COLS), lambda l, j: (l, 0, j)),
        out_shape=jax.ShapeDtypeStruct((DEPTH, 8, width), F32),
        compiler_params=_params(2),
        name="ada_mods",
    )(cc, ada_w, ada_b.reshape(DEPTH, 1, width))


def _ffn_kernel(i0, h_ref, mod_ref, g_ref, w1_ref, w3_ref, w2_ref, o_ref, acc_ref):
    x = h_ref[...]
    mods = mod_ref[0]
    z = _norm_mod(x, g_ref[...], mods[i0:i0 + 1], mods[i0 + 1:i0 + 2]).astype(BF16)
    acc_ref[...] = jnp.zeros_like(acc_ref)

    def chunk(c, carry):
        a = jnp.dot(z, w1_ref[c], preferred_element_type=F32)
        b = jnp.dot(z, w3_ref[c], preferred_element_type=F32)
        gated = (a * _sigmoid(a) * b).astype(BF16)
        acc_ref[...] += jnp.dot(gated, w2_ref[c], preferred_element_type=F32)
        return carry

    lax.fori_loop(0, N_FF_CHUNKS, chunk, 0)
    o_ref[...] = x + (0.5 * mods[i0 + 2:i0 + 3]) * acc_ref[...]


def _ffn(h, mods, g, w1c, w3c, w2c, i0, n_tiles):
    return pl.pallas_call(
        functools.partial(_ffn_kernel, i0),
        grid=(n_tiles,),
        in_specs=[
            pl.BlockSpec((TM, D_MODEL), lambda i: (i, 0)),
            pl.BlockSpec((1, N_MOD, D_MODEL), lambda i: (i // N_LAT_TILES, 0, 0)),
            pl.BlockSpec((1, D_MODEL), lambda i: (0, 0)),
            _resident((N_FF_CHUNKS, D_MODEL, FF_CHUNK)),
            _resident((N_FF_CHUNKS, D_MODEL, FF_CHUNK)),
            _resident((N_FF_CHUNKS, FF_CHUNK, D_MODEL)),
        ],
        out_specs=pl.BlockSpec((TM, D_MODEL), lambda i: (i, 0)),
        out_shape=jax.ShapeDtypeStruct((n_tiles * TM, D_MODEL), F32),
        scratch_shapes=[pltpu.VMEM((TM, D_MODEL), F32)],
        compiler_params=_params(1),
        name="ffn",
    )(h, mods, g, w1c, w3c, w2c)


def _swap16(x):
    lane = lax.broadcasted_iota(jnp.int32, x.shape, 1)
    return jnp.where(lane % 32 < 16, pltpu.roll(x, LANES - 16, 1), pltpu.roll(x, 16, 1))


def _rope(x, cos, sin):
    return x * cos + _swap16(x) * sin


def _head_mean_sq(x):
    sq = x * x
    hi = sq.astype(BF16)
    lo = (sq - hi.astype(F32)).astype(BF16)
    r = lax.broadcasted_iota(jnp.int32, (LANES, LANES), 0) // HEAD_DIM
    c = lax.broadcasted_iota(jnp.int32, (LANES, LANES), 1) // HEAD_DIM
    ones = jnp.where(r == c, 1.0, 0.0).astype(BF16)
    tot = (jnp.dot(hi, ones, preferred_element_type=F32)
           + jnp.dot(lo, ones, preferred_element_type=F32))
    return tot * (1.0 / HEAD_DIM)


def _inproj_kernel(h_ref, mod_ref, g_ref, w_ref, cos_ref, sin_ref, qkg_ref,
                   qt_da_ref, k_da_ref, vt_da_ref, qt_g_ref, k_g_ref, vt_g_ref, xb_ref, gb_ref):
    mods = mod_ref[0]
    z = _norm_mod(h_ref[...], g_ref[...], mods[3:4], mods[4:5]).astype(BF16)
    proj = jnp.dot(z, w_ref[...], preferred_element_type=F32)
    cos = cos_ref[...]
    sin = sin_ref[...]
    q_scale = SCALE * LOG2E

    def slab(j):
        return proj[:, j * LANES:(j + 1) * LANES]

    for h in range(DA_HEADS):
        q = _rope(slab(h), cos, sin) * q_scale
        qt_da_ref[h] = q.T.astype(BF16)
        k_da_ref[:, h * LANES:(h + 1) * LANES] = _rope(slab(4 + h), cos, sin).astype(BF16)
        vt_da_ref[h] = slab(8 + h).T.astype(BF16)

    gq_gain = qkg_ref[0:1, :]
    gk_gain = qkg_ref[1:2, :]
    for g in range(GQA_KV_HEADS):
        q = slab(12 + g)
        q = (q * lax.rsqrt(_head_mean_sq(q) + EPS)) * gq_gain
        qt_g_ref[g] = (_rope(q, cos, sin) * q_scale).T.astype(BF16)
    k = slab(14)
    k = (k * lax.rsqrt(_head_mean_sq(k) + EPS)) * gk_gain
    k_g_ref[...] = _rope(k, cos, sin).astype(BF16)
    vt_g_ref[...] = slab(15).T.astype(BF16)
    xb_ref[...] = proj[:, 2048:2304]
    gb_ref[...] = proj[:, 2304:2560]


def _inproj(h, mods, g, w_in, cos_t, sin_t, qk_gain):
    row_tile = lambda width: pl.BlockSpec((TM, width), lambda i: (i, 0))
    return pl.pallas_call(
        _inproj_kernel,
        grid=(N_TILES,),
        in_specs=[
            row_tile(D_MODEL),
            pl.BlockSpec((1, N_MOD, D_MODEL), lambda i: (i // N_LAT_TILES, 0, 0)),
            pl.BlockSpec((1, D_MODEL), lambda i: (0, 0)),
            _resident((D_MODEL, IN_WIDTH)),
            row_tile(LANES),
            row_tile(LANES),
            pl.BlockSpec((2, LANES), lambda i: (0, 0)),
        ],
        out_specs=[
            pl.BlockSpec((DA_HEADS, LANES, TM), lambda i: (0, 0, i)),
            row_tile(DA_HEADS * LANES),
            pl.BlockSpec((DA_HEADS, LANES, TM), lambda i: (0, 0, i)),
            pl.BlockSpec((GQA_KV_HEADS, LANES, TM), lambda i: (0, 0, i)),
            row_tile(LANES),
            pl.BlockSpec((LANES, TM), lambda i: (0, i)),
            row_tile(LRU_WIDTH),
            row_tile(LRU_WIDTH),
        ],
        out_shape=[
            jax.ShapeDtypeStruct((DA_HEADS, LANES, T_ALL), BF16),
            jax.ShapeDtypeStruct((T_ALL, DA_HEADS * LANES), BF16),
            jax.ShapeDtypeStruct((DA_HEADS, LANES, T_ALL), BF16),
            jax.ShapeDtypeStruct((GQA_KV_HEADS, LANES, T_ALL), BF16),
            jax.ShapeDtypeStruct((T_ALL, LANES), BF16),
            jax.ShapeDtypeStruct((LANES, T_ALL), BF16),
            jax.ShapeDtypeStruct((T_ALL, LRU_WIDTH), F32),
            jax.ShapeDtypeStruct((T_ALL, LRU_WIDTH), F32),
        ],
        compiler_params=_params(1),
        name="inproj",
    )(h, mods, g, w_in, cos_t, sin_t, qk_gain)


def _lru_tile_index(d, j):
    lat = jnp.where(d == 0, j - 1, N_LAT_TILES - j)
    return jnp.where(j == 0, N_LAT_TILES, lat)


def _lru_kernel(x_ref, cw_ref, cb_ref, wa_ref, wi_ref, ba_ref, bi_ref, lam_ref, o_ref, carry_ref):
    d = pl.program_id(0)
    j = pl.program_id(1)
    ti = _lru_tile_index(d, j)
    t0 = pl.multiple_of(ti * TM, TM)

    @pl.when(j == 0)
    def _():
        carry_ref[...] = jnp.zeros_like(carry_ref)

    x = x_ref[pl.ds(t0, TM), :]
    prev8 = x_ref[pl.ds(pl.multiple_of(jnp.maximum(t0 - 8, 0), 8), 8), :]
    next8 = x_ref[pl.ds(pl.multiple_of(jnp.minimum(t0 + TM, T_ALL - 8), 8), 8), :]
    has_prev = jnp.logical_and(ti != 0, ti != N_LAT_TILES)
    has_next = jnp.logical_and(ti != N_LAT_TILES - 1, ti != N_LAT_TILES)
    prev_row = jnp.where(has_prev, prev8[7:8], 0.0)
    next_row0 = jnp.where(has_next, next8[0:1], 0.0)
    next_row1 = jnp.where(has_next, next8[1:2], 0.0)
    row = lax.broadcasted_iota(jnp.int32, (TM, LRU_WIDTH), 0)
    xm1 = jnp.where(row == 0, prev_row, pltpu.roll(x, 1, 0))
    xp1 = jnp.where(row == TM - 1, next_row0, pltpu.roll(x, TM - 1, 0))
    xp2 = jnp.where(row == TM - 2, next_row0,
                    jnp.where(row == TM - 1, next_row1, pltpu.roll(x, TM - 2, 0)))
    cw = cw_ref[...]
    xc = xm1 * cw[0:1] + x * cw[1:2] + xp1 * cw[2:3] + xp2 * cw[3:4] + cb_ref[...]

    xcb = xc.astype(BF16)
    r = _sigmoid(jnp.dot(xcb, wa_ref[0], preferred_element_type=F32) + ba_ref[0])
    gi = _sigmoid(jnp.dot(xcb, wi_ref[0], preferred_element_type=F32) + bi_ref[0])
    neg_lam = -lam_ref[0]
    softplus = jnp.maximum(neg_lam, 0.0) + jnp.log1p(jnp.exp(-jnp.abs(neg_lam)))
    log_a = (-LRU_C) * r * softplus
    a = jnp.exp(log_a)
    u = jnp.sqrt(-jnp.tanh(log_a) * (a * a + 1.0)) * (gi * xc)

    def scan(A, B, forward):
        k = 1
        while k < TM:
            if forward:
                keep = row >= k
                As = jnp.where(keep, pltpu.roll(A, k, 0), 1.0)
                Bs = jnp.where(keep, pltpu.roll(B, k, 0), 0.0)
            else:
                keep = row < TM - k
                As = jnp.where(keep, pltpu.roll(A, TM - k, 0), 1.0)
                Bs = jnp.where(keep, pltpu.roll(B, TM - k, 0), 0.0)
            B = A * Bs + B
            A = A * As
            k *= 2
        return A, B

    carry = carry_ref[0:1, :]

    @pl.when(d == 0)
    def _():
        A, B = scan(a, u, True)
        hs = B + A * carry
        o_ref[0] = hs
        carry_ref[0:1, :] = hs[TM - 1:TM]

    @pl.when(d == 1)
    def _():
        A, B = scan(a, u, False)
        hs = B + A * carry
        o_ref[0] = hs
        carry_ref[0:1, :] = hs[0:1]


def _lru(xb, conv_w, conv_b, wa, wi, ba, bi, lam):
    per_dir = lambda *tail: pl.BlockSpec((1,) + tail, lambda d, j: (d,) + (0,) * len(tail))
    return pl.pallas_call(
        _lru_kernel,
        grid=(2, N_TILES),
        in_specs=[
            _resident((T_ALL, LRU_WIDTH)),
            pl.BlockSpec((4, LRU_WIDTH), lambda d, j: (0, 0)),
            pl.BlockSpec((1, LRU_WIDTH), lambda d, j: (0, 0)),
            per_dir(LRU_WIDTH, LRU_WIDTH),
            per_dir(LRU_WIDTH, LRU_WIDTH),
            per_dir(1, LRU_WIDTH),
            per_dir(1, LRU_WIDTH),
            per_dir(1, LRU_WIDTH),
        ],
        out_specs=pl.BlockSpec((1, TM, LRU_WIDTH), lambda d, j: (d, _lru_tile_index(d, j), 0)),
        out_shape=jax.ShapeDtypeStruct((2, T_ALL, LRU_WIDTH), F32),
        scratch_shapes=[pltpu.VMEM((8, LRU_WIDTH), F32)],
        compiler_params=_params(2),
        name="rglru",
    )(xb, conv_w, conv_b, wa, wi, ba, bi, lam)


def _softmax_chunk(s, m_old, l_old):
    m_new = jnp.maximum(m_old, jnp.max(s, axis=0, keepdims=True))
    alpha = jnp.exp2(m_old - m_new)
    p = jnp.exp2(s - m_new)
    l_new = alpha * l_old + jnp.sum(p, axis=0, keepdims=True)
    return p.astype(BF16), m_new, l_new, alpha


def _attend_ctx(kc, vc, weights):
    outs = []
    for w in weights:
        s = jnp.dot(kc, w, preferred_element_type=F32)
        p, _, l, _ = _softmax_chunk(s, jnp.full((1, CTX_LEN), NEG_BIG, F32),
                                    jnp.zeros((1, CTX_LEN), F32))
        outs.append(jnp.dot(vc, p, preferred_element_type=F32) * (1.0 / l))
    return outs


def _key_norm_bound(k_ref, kmax_ref):
    r = lax.broadcasted_iota(jnp.int32, (LANES, LANES), 0) // HEAD_DIM
    c = lax.broadcasted_iota(jnp.int32, (LANES, LANES), 1) // HEAD_DIM
    ones = jnp.where(r == c, 1.0, 0.0).astype(BF16)

    def chunk(i, best):
        kf = k_ref[pl.ds(pl.multiple_of(i * TK, TK), TK), :].astype(F32)
        n2 = jnp.dot((kf * kf).astype(BF16), ones, preferred_element_type=F32)
        return jnp.maximum(best, jnp.max(n2, axis=0, keepdims=True))

    kmax_ref[0:1, :] = lax.fori_loop(0, N_K_CHUNKS, chunk, jnp.zeros((1, LANES), F32))


def _attend_lat_fast(k_ref, v_chunk, weights, shift, acc_ref, p_refs):
    maps = range(len(weights))

    def scores_exp(c, p_ref, l):
        kc = k_ref[pl.ds(pl.multiple_of(c * TK, TK), TK), :]
        l_new = []
        for j in maps:
            p = jnp.exp2(jnp.dot(kc, weights[j], preferred_element_type=F32) - shift[j])
            l_new.append(l[j] + jnp.sum(p, axis=0, keepdims=True))
            p_ref[j] = p.astype(BF16)
        return tuple(l_new)

    def values(c, p_ref):
        vc = v_chunk(pl.multiple_of(c * TK, TK), TK)
        for j in maps:
            acc_ref[j] += jnp.dot(vc, p_ref[j], preferred_element_type=F32)

    acc_ref[...] = jnp.zeros_like(acc_ref)
    l = scores_exp(0, p_refs[0], tuple(jnp.zeros((1, TQ), F32) for _ in maps))

    def chunk_pair(i, l):
        c0 = 2 * i
        l = scores_exp(c0 + 1, p_refs[1], l)
        values(c0, p_refs[0])
        l = scores_exp(c0 + 2, p_refs[0], l)
        values(c0 + 1, p_refs[1])
        return l

    l = lax.fori_loop(0, N_K_CHUNKS // 2 - 1, chunk_pair, l)
    l = scores_exp(N_K_CHUNKS - 1, p_refs[1], l)
    values(N_K_CHUNKS - 2, p_refs[0])
    values(N_K_CHUNKS - 1, p_refs[1])
    return l


def _attend_lat_exact(k_ref, v_chunk, weights, acc_ref, s_refs, p_refs):
    maps = range(len(weights))

    def scores(c, s_ref):
        kc = k_ref[pl.ds(pl.multiple_of(c * TK, TK), TK), :]
        for j in maps:
            s_ref[j] = jnp.dot(kc, weights[j], preferred_element_type=F32)

    def softmax(s_ref, p_ref, m, l):
        m_new, l_new, alpha = [], [], []
        for j in maps:
            m_j = jnp.maximum(m[j], jnp.max(s_ref[j], axis=0, keepdims=True))
            a_j = jnp.exp2(m[j] - m_j)
            p = jnp.exp2(s_ref[j] - m_j)
            l_new.append(a_j * l[j] + jnp.sum(p, axis=0, keepdims=True))
            p_ref[j] = p.astype(BF16)
            m_new.append(m_j)
            alpha.append(a_j)
        return m_new, l_new, alpha

    def values(c, p_ref, alpha):
        vc = v_chunk(pl.multiple_of(c * TK, TK), TK)
        for j in maps:
            acc_ref[j] = alpha[j] * acc_ref[j] + jnp.dot(vc, p_ref[j], preferred_element_type=F32)

    acc_ref[...] = jnp.zeros_like(acc_ref)
    p_refs[1][...] = jnp.zeros_like(p_refs[1])
    scores(0, s_refs[0])
    stat = lambda v: [jnp.full((1, TQ), v, F32) for _ in maps]

    def chunk_pair(i, carry):
        m, l, alpha_b = carry
        c0 = 2 * i
        scores(c0 + 1, s_refs[1])
        m, l, alpha_a = softmax(s_refs[0], p_refs[0], m, l)
        values(jnp.maximum(c0 - 1, 0), p_refs[1], alpha_b)
        scores(jnp.minimum(c0 + 2, N_K_CHUNKS - 1), s_refs[0])
        m, l, alpha_b = softmax(s_refs[1], p_refs[1], m, l)
        values(c0, p_refs[0], alpha_a)
        return m, l, alpha_b

    _, l, alpha_b = lax.fori_loop(0, N_K_CHUNKS // 2, chunk_pair,
                                  (stat(NEG_BIG), stat(0.0), stat(1.0)))
    values(N_K_CHUNKS - 1, p_refs[1], alpha_b)
    return l


def _attend(ctx_only, k_ref, vt_at, weights, halves, scratch):
    if ctx_only:
        return _attend_ctx(k_ref[...], vt_at(0, CTX_LEN), weights)
    acc_ref, l_ref, kmax_ref, s0_ref, s1_ref, p0_ref, p1_ref = scratch
    maps = range(len(weights))

    @pl.when(pl.program_id(1) == 0)
    def _():
        _key_norm_bound(k_ref, kmax_ref)

    lane = lax.broadcasted_iota(jnp.int32, (1, LANES), 1)
    shift = []
    for j in maps:
        k2 = jnp.max(jnp.where(lane // HEAD_DIM == halves[j], kmax_ref[0:1, :], 0.0),
                     axis=-1, keepdims=True)
        wf = weights[j].astype(F32)
        q2 = jnp.sum(wf * wf, axis=0, keepdims=True)
        shift.append(jnp.sqrt(q2 * k2) * BOUND_SLACK - SHIFT_GUARD)

    l = _attend_lat_fast(k_ref, vt_at, weights, shift, acc_ref, (p0_ref, p1_ref))
    n_bad = jnp.zeros((), F32)
    for j in maps:
        l_ref[j:j + 1, :] = l[j]
        ok = jnp.logical_and(l[j] >= L_MIN, l[j] <= L_MAX)
        n_bad = n_bad + jnp.sum(jnp.where(ok, 0.0, 1.0))

    @pl.when(n_bad > 0.0)
    def _():
        l = _attend_lat_exact(k_ref, vt_at, weights, acc_ref, (s0_ref, s1_ref), (p0_ref, p1_ref))
        for j in maps:
            l_ref[j:j + 1, :] = l[j]

    return [acc_ref[j] * (1.0 / l_ref[j:j + 1, :]) for j in maps]


def _attn_call(kernel, ctx_only, n_heads, dv, in_blocks, operands, out_width, name):
    if ctx_only:
        tq, n_q, n_keys, q0, k0 = CTX_LEN, 1, CTX_LEN, SEQ // CTX_LEN, SEQ // CTX_LEN
        scratch = []
    else:
        tq, n_q, n_keys, q0, k0 = TQ, SEQ // TQ, T_ALL, 0, 0
        scratch = [pltpu.VMEM((2, dv, TQ), F32), pltpu.VMEM((8, TQ), F32),
                   pltpu.VMEM((8, LANES), F32),
                   pltpu.VMEM((2, TK, TQ), F32), pltpu.VMEM((2, TK, TQ), F32),
                   pltpu.VMEM((2, TK, TQ), BF16), pltpu.VMEM((2, TK, TQ), BF16)]
    return pl.pallas_call(
        functools.partial(kernel, ctx_only),
        grid=(n_heads, n_q),
        in_specs=in_blocks(tq, n_keys, q0, k0),
        out_specs=pl.BlockSpec((tq, LANES), lambda h, i: (i, h)),
        out_shape=jax.ShapeDtypeStruct((n_q * tq, out_width), BF16),
        scratch_shapes=scratch,
        compiler_params=_params(2),
        name=name + ("_ctx" if ctx_only else ""),
    )(*operands)


def _da_kernel(lam_init, ctx_only, qt_ref, k_ref, vt_ref, lam_ref, g_ref, o_ref, *scratch):
    qt = qt_ref[0]
    row = lax.broadcasted_iota(jnp.int32, qt.shape, 0)
    zero = jnp.zeros_like(qt)
    weights = [jnp.where(row < HEAD_DIM, qt, zero), jnp.where(row >= HEAD_DIM, qt, zero)]
    o1, o2 = _attend(ctx_only, k_ref, lambda s, n: vt_ref[0, :, pl.ds(s, n)], weights, (0, 1),
                     scratch)
    lf = lam_ref[...]
    lam = (jnp.exp(jnp.sum(lf[0:1] * lf[1:2], axis=-1, keepdims=True))
           - jnp.exp(jnp.sum(lf[2:3] * lf[3:4], axis=-1, keepdims=True)) + lam_init)
    o = o1 - lam * o2
    ms = jnp.mean(o * o, axis=0, keepdims=True)
    on = (o * lax.rsqrt(ms + EPS)) * g_ref[...] * (1.0 - lam_init)
    o_ref[...] = on.T.astype(o_ref.dtype)


def _da_attention(qt, k, vt, da_lam, subln_g, lam_init, ctx_only):
    def in_blocks(tq, n_keys, q0, k0):
        return [
            pl.BlockSpec((1, LANES, tq), lambda h, i: (h, 0, i + q0)),
            pl.BlockSpec((n_keys, LANES), lambda h, i: (k0, h)),
            pl.BlockSpec((1, LANES, n_keys), lambda h, i: (h, 0, k0)),
            pl.BlockSpec((4, HEAD_DIM), lambda h, i: (0, 0)),
            pl.BlockSpec((LANES, 1), lambda h, i: (0, 0)),
        ]
    return _attn_call(functools.partial(_da_kernel, lam_init), ctx_only, DA_HEADS, LANES,
                      in_blocks, (qt, k, vt, da_lam, subln_g), DA_HEADS * LANES, "da_attention")


def _gqa_kernel(ctx_only, qt_ref, k_ref, vt_ref, o_ref, *scratch):
    g = pl.program_id(0)
    qt = qt_ref[0]
    zero = jnp.zeros((HEAD_DIM, qt.shape[1]), qt.dtype)
    weights = []
    for j in range(2):
        qj = qt[j * HEAD_DIM:(j + 1) * HEAD_DIM]
        weights.append(jnp.where(g == 0, jnp.concatenate([qj, zero], axis=0),
                                 jnp.concatenate([zero, qj], axis=0)))
    o0, o1 = _attend(ctx_only, k_ref, lambda s, n: vt_ref[:, pl.ds(s, n)], weights, (g, g),
                     scratch)
    o_ref[...] = jnp.concatenate([o0, o1], axis=0).T.astype(o_ref.dtype)


def _gqa_attention(qt, k, vt, ctx_only):
    def in_blocks(tq, n_keys, q0, k0):
        return [
            pl.BlockSpec((1, LANES, tq), lambda g, i: (g, 0, i + q0)),
            pl.BlockSpec((n_keys, LANES), lambda g, i: (k0, 0)),
            pl.BlockSpec((HEAD_DIM, n_keys), lambda g, i: (g, k0)),
        ]
    return _attn_call(_gqa_kernel, ctx_only, GQA_KV_HEADS, HEAD_DIM, in_blocks, (qt, k, vt),
                      GQA_KV_HEADS * LANES, "gqa_attention")


def _gelu_tanh(x):
    return 0.5 * x * (1.0 + jnp.tanh(math.sqrt(2.0 / math.pi) * (x + 0.044715 * (x * x * x))))


def _outproj_kernel(h_ref, mod_ref, a_ref, g_ref, hs_ref, gb_ref, w_ref, o_ref):
    mods = mod_ref[0]
    r = ((hs_ref[0] + hs_ref[1]) * _gelu_tanh(gb_ref[...])).astype(BF16)
    n_a = DA_HEADS * LANES
    n_g = n_a + GQA_KV_HEADS * LANES
    y = (jnp.dot(a_ref[...], w_ref[0:n_a, :], preferred_element_type=F32)
         + jnp.dot(g_ref[...], w_ref[n_a:n_g, :], preferred_element_type=F32)
         + jnp.dot(r, w_ref[n_g:, :], preferred_element_type=F32))
    o_ref[...] = h_ref[...] + mods[5:6] * y


def _outproj(h, mods, a, g, hs, gb, w_out, n_tiles):
    row_tile = lambda width: pl.BlockSpec((TM, width), lambda i: (i, 0))
    return pl.pallas_call(
        _outproj_kernel,
        grid=(n_tiles,),
        in_specs=[
            row_tile(D_MODEL),
            pl.BlockSpec((1, N_MOD, D_MODEL), lambda i: (i // N_LAT_TILES, 0, 0)),
            row_tile(DA_HEADS * LANES),
            row_tile(GQA_KV_HEADS * LANES),
            pl.BlockSpec((2, TM, LRU_WIDTH), lambda i: (0, i, 0)),
            row_tile(LRU_WIDTH),
            _resident((D_MODEL, D_MODEL)),
        ],
        out_specs=row_tile(D_MODEL),
        out_shape=jax.ShapeDtypeStruct((n_tiles * TM, D_MODEL), F32),
        compiler_params=_params(1),
        name="outproj",
    )(h, mods, a, g, hs, gb, w_out)


def _final_norm_kernel(h_ref, g_ref, o_ref):
    x = h_ref[...]
    ms = jnp.mean(x * x, axis=-1, keepdims=True)
    o_ref[...] = (x * lax.rsqrt(ms + EPS)) * g_ref[...]


def _final_norm(h, g):
    return pl.pallas_call(
        _final_norm_kernel,
        grid=(N_LAT_TILES,),
        in_specs=[pl.BlockSpec((TM, D_MODEL), lambda i: (i, 0)),
                  pl.BlockSpec((1, D_MODEL), lambda i: (0, 0))],
        out_specs=pl.BlockSpec((TM, D_MODEL), lambda i: (i, 0)),
        out_shape=jax.ShapeDtypeStruct((SEQ, D_MODEL), F32),
        compiler_params=_params(1),
        name="final_norm",
    )(h, g)


def _rope_tables():
    half = HEAD_DIM // 4
    t = jnp.arange(SEQ, dtype=jnp.int32)
    row = (t // GRID_W).astype(F32)
    col = (t % GRID_W).astype(F32)
    inv = ROPE_THETA ** (-jnp.arange(half, dtype=F32) / half)
    ang_r = row[:, None] * inv
    ang_c = col[:, None] * inv
    cos64 = jnp.concatenate([jnp.cos(ang_r)] * 2 + [jnp.cos(ang_c)] * 2, axis=-1)
    sin64 = jnp.concatenate([-jnp.sin(ang_r), jnp.sin(ang_r), -jnp.sin(ang_c), jnp.sin(ang_c)],
                            axis=-1)
    cos_t = jnp.concatenate([jnp.tile(cos64, (1, 2)), jnp.ones((CTX_LEN, LANES), F32)], axis=0)
    sin_t = jnp.concatenate([jnp.tile(sin64, (1, 2)), jnp.zeros((CTX_LEN, LANES), F32)], axis=0)
    return cos_t, sin_t


def _block_diag(w):
    eye = jnp.eye(LRU_BLOCKS, dtype=w.dtype)
    full = jnp.einsum('dkij,kl->dkilj', w, eye)
    return full.reshape(2, LRU_WIDTH, LRU_WIDTH).astype(BF16)


def _chunk_cols(w):
    return w.reshape(D_MODEL, N_FF_CHUNKS, FF_CHUNK).transpose(1, 0, 2).astype(BF16)


def kernel(x, c, ctx, c_ctx, ada_w, ada_b, norm_g, ffn1_w13, ffn1_w2, ffn2_w13, ffn2_w2,
           w_in, w_out, da_lam, da_subln_g, qk_norm_g, lru_conv_w, lru_conv_b,
           lru_wa, lru_ba, lru_wi, lru_bi, lru_lambda, final_g):
    assert x.shape == (1, SEQ, D_MODEL) and ctx.shape == (1, CTX_LEN, D_MODEL)
    cc = jnp.zeros((8, D_MODEL), F32).at[0].set(c[0]).at[1].set(c_ctx)
    mods_all = _ada_mods(cc, ada_w, ada_b).reshape(DEPTH, 8, N_MOD, D_MODEL)[:, :2]
    cos_t, sin_t = _rope_tables()

    h = jnp.concatenate([x[0], ctx[0]], axis=0)
    for l in range(DEPTH):
        last = l == DEPTH - 1
        lam_init = 0.8 - 0.6 * math.exp(-0.3 * l)
        mods = mods_all[l]

        def ffn_weights(w13, w2):
            return (_chunk_cols(w13[l][:, :D_FF]), _chunk_cols(w13[l][:, D_FF:]),
                    w2[l].reshape(N_FF_CHUNKS, FF_CHUNK, D_MODEL).astype(BF16))

        h = _ffn(h, mods, norm_g[l, 0:1], *ffn_weights(ffn1_w13, ffn1_w2), 0, N_TILES)

        qk_gain = jnp.tile(qk_norm_g[l], (1, 2))
        qt_da, k_da, vt_da, qt_g, k_g, vt_g, xb, gb = _inproj(
            h, mods, norm_g[l, 1:2], w_in[l].astype(BF16), cos_t, sin_t, qk_gain)

        hs = _lru(xb, lru_conv_w[l], lru_conv_b[l][None, :], _block_diag(lru_wa[l]),
                  _block_diag(lru_wi[l]), lru_ba[l][:, None, :], lru_bi[l][:, None, :],
                  lru_lambda[l][:, None, :])

        n_tiles = N_LAT_TILES if last else N_TILES
        da_args = (qt_da, k_da, vt_da, da_lam[l], da_subln_g[l][:, None], lam_init)
        a = _da_attention(*da_args, False)
        g = _gqa_attention(qt_g, k_g, vt_g, False)
        if not last:
            a = jnp.concatenate([a, _da_attention(*da_args, True)], axis=0)
            g = jnp.concatenate([g, _gqa_attention(qt_g, k_g, vt_g, True)], axis=0)

        h = _outproj(h, mods, a, g, hs, gb, w_out[l].astype(BF16), n_tiles)
        h = _ffn(h, mods, norm_g[l, 2:3], *ffn_weights(ffn2_w13, ffn2_w2), 6, n_tiles)
    return _final_norm(h, final_g[None, :])[None]
```

```python
import functools
import math

import jax
import jax.numpy as jnp
from jax import lax
from jax.experimental import pallas as pl
from jax.experimental.pallas import tpu as pltpu

D_MODEL = 1024
SEQ = 16384
CTX_LEN = 256
T_ALL = SEQ + CTX_LEN
DEPTH = 4
GRID_W = 64
HEAD_DIM = 64
SCALE = HEAD_DIM ** -0.5
LOG2E = math.log2(math.e)
ROPE_THETA = 10000.0
DA_HEADS = 4
GQA_KV_HEADS = 2
LRU_WIDTH = 256
LRU_BLOCKS = 4
LRU_BLOCK = LRU_WIDTH // LRU_BLOCKS
LRU_C = 8.0
IN_WIDTH = 2560
D_FF = 2816
N_MOD = 9
EPS = 1e-6

LANES = 128
MXU_DIM = 256

TM = 256
N_TILES = T_ALL // TM
N_LAT_TILES = SEQ // TM
FF_CHUNK = MXU_DIM
N_FF_CHUNKS = D_FF // FF_CHUNK
FFN_UNROLL = 10
TQ = 1024
TK = 640
N_K_CHUNKS = T_ALL // TK
TK_FAST = 1280
N_K_FAST = T_ALL // TK_FAST
NEG_BIG = -1e30
SHIFT_GUARD = 32.0
BOUND_SLACK = 1.02
L_MIN = 2.0 ** -60
L_MAX = 2.0 ** 100
VMEM_LIMIT = 48 * 1024 * 1024

F32 = jnp.float32
BF16 = jnp.bfloat16


def _params(n_axes):
    return pltpu.CompilerParams(dimension_semantics=("arbitrary",) * n_axes,
                                vmem_limit_bytes=VMEM_LIMIT)


def _resident(shape):
    zeros = (0,) * len(shape)
    return pl.BlockSpec(shape, lambda *_: zeros, pipeline_mode=pl.Buffered(1))


def _sigmoid(x):
    return 1.0 / (1.0 + jnp.exp(-x))


def _norm_mod(x, g, shift, scale):
    ms = jnp.mean(x * x, axis=-1, keepdims=True)
    y = (x * lax.rsqrt(ms + EPS)) * g
    return y * (1.0 + scale) + shift


ADA_COLS = 1152


def _ada_kernel(c_ref, w_ref, b_ref, o_ref):
    c = c_ref[...]
    s = (c * _sigmoid(c)).astype(BF16)
    o_ref[0] = jnp.dot(s, w_ref[0].astype(BF16), preferred_element_type=F32) + b_ref[0]


def _ada_mods(cc, ada_w, ada_b):
    width = N_MOD * D_MODEL
    return pl.pallas_call(
        _ada_kernel,
        grid=(DEPTH, width // ADA_COLS),
        in_specs=[
            pl.BlockSpec((8, D_MODEL), lambda l, j: (0, 0)),
            pl.BlockSpec((1, D_MODEL, ADA_COLS), lambda l, j: (l, 0, j)),
            pl.BlockSpec((1, 1, ADA_COLS), lambda l, j: (l, 0, j)),
        ],
        out_specs=pl.BlockSpec((1, 8, ADA_COLS), lambda l, j: (l, 0, j)),
        out_shape=jax.ShapeDtypeStruct((DEPTH, 8, width), F32),
        compiler_params=_params(2),
        name="ada_mods",
    )(cc, ada_w, ada_b.reshape(DEPTH, 1, width))


def _ffn_kernel(i0, h_ref, mod_ref, g_ref, w1_ref, w3_ref, w2_ref, o_ref, acc_ref, gate_ref):
    x = h_ref[...]
    mods = mod_ref[0]
    z = _norm_mod(x, g_ref[...], mods[i0:i0 + 1], mods[i0 + 1:i0 + 2]).astype(BF16)
    acc_ref[...] = jnp.zeros_like(acc_ref)

    def gate(c):
        a = jnp.dot(z, w1_ref[c], preferred_element_type=F32)
        b = jnp.dot(z, w3_ref[c], preferred_element_type=F32)
        return (a * _sigmoid(a) * b).astype(BF16)

    def down(c):
        acc_ref[...] += jnp.dot(gate_ref[...], w2_ref[c], preferred_element_type=F32)

    gate_ref[...] = gate(0)

    def chunk(c, carry):
        down(c - 1)
        gate_ref[...] = gate(c)
        return carry

    lax.fori_loop(1, N_FF_CHUNKS, chunk, 0, unroll=FFN_UNROLL)
    down(N_FF_CHUNKS - 1)
    o_ref[...] = x + (0.5 * mods[i0 + 2:i0 + 3]) * acc_ref[...]


def _ffn(h, mods, g, w1c, w3c, w2c, i0, n_tiles):
    return pl.pallas_call(
        functools.partial(_ffn_kernel, i0),
        grid=(n_tiles,),
        in_specs=[
            pl.BlockSpec((TM, D_MODEL), lambda i: (i, 0)),
            pl.BlockSpec((1, N_MOD, D_MODEL), lambda i: (i // N_LAT_TILES, 0, 0)),
            pl.BlockSpec((1, D_MODEL), lambda i: (0, 0)),
            _resident((N_FF_CHUNKS, D_MODEL, FF_CHUNK)),
            _resident((N_FF_CHUNKS, D_MODEL, FF_CHUNK)),
            _resident((N_FF_CHUNKS, FF_CHUNK, D_MODEL)),
        ],
        out_specs=pl.BlockSpec((TM, D_MODEL), lambda i: (i, 0)),
        out_shape=jax.ShapeDtypeStruct((n_tiles * TM, D_MODEL), F32),
        scratch_shapes=[pltpu.VMEM((TM, D_MODEL), F32), pltpu.VMEM((TM, FF_CHUNK), BF16)],
        compiler_params=_params(1),
        name="ffn",
    )(h, mods, g, w1c, w3c, w2c)


def _swap16(x):
    lane = lax.broadcasted_iota(jnp.int32, x.shape, 1)
    return jnp.where(lane % 32 < 16, pltpu.roll(x, LANES - 16, 1), pltpu.roll(x, 16, 1))


def _rope(x, cos, sin):
    return x * cos + _swap16(x) * sin


def _head_mean_sq(x):
    sq = x * x
    hi = sq.astype(BF16)
    lo = (sq - hi.astype(F32)).astype(BF16)
    r = lax.broadcasted_iota(jnp.int32, (LANES, LANES), 0) // HEAD_DIM
    c = lax.broadcasted_iota(jnp.int32, (LANES, LANES), 1) // HEAD_DIM
    ones = jnp.where(r == c, 1.0, 0.0).astype(BF16)
    tot = (jnp.dot(hi, ones, preferred_element_type=F32)
           + jnp.dot(lo, ones, preferred_element_type=F32))
    return tot * (1.0 / HEAD_DIM)


def _inproj_kernel(h_ref, mod_ref, g_ref, w_ref, cos_ref, sin_ref, qkg_ref,
                   qt_da_ref, k_da_ref, vt_da_ref, qt_g_ref, k_g_ref, vt_g_ref, xb_ref, gb_ref):
    mods = mod_ref[0]
    z = _norm_mod(h_ref[...], g_ref[...], mods[3:4], mods[4:5]).astype(BF16)
    proj = jnp.dot(z, w_ref[...], preferred_element_type=F32)
    cos = cos_ref[...]
    sin = sin_ref[...]
    q_scale = SCALE * LOG2E

    def slab(j):
        return proj[:, j * LANES:(j + 1) * LANES]

    for h in range(DA_HEADS):
        q = _rope(slab(h), cos, sin) * q_scale
        qt_da_ref[h] = q.T.astype(BF16)
        k_da_ref[:, h * LANES:(h + 1) * LANES] = _rope(slab(4 + h), cos, sin).astype(BF16)
        vt_da_ref[h] = slab(8 + h).T.astype(BF16)

    gq_gain = qkg_ref[0:1, :]
    gk_gain = qkg_ref[1:2, :]
    for g in range(GQA_KV_HEADS):
        q = slab(12 + g)
        q = (q * lax.rsqrt(_head_mean_sq(q) + EPS)) * gq_gain
        qt_g_ref[g] = (_rope(q, cos, sin) * q_scale).T.astype(BF16)
    k = slab(14)
    k = (k * lax.rsqrt(_head_mean_sq(k) + EPS)) * gk_gain
    k_g_ref[...] = _rope(k, cos, sin).astype(BF16)
    vt_g_ref[...] = slab(15).T.astype(BF16)
    xb_ref[...] = proj[:, 2048:2304]
    gb_ref[...] = proj[:, 2304:2560]


def _inproj(h, mods, g, w_in, cos_t, sin_t, qk_gain):
    row_tile = lambda width: pl.BlockSpec((TM, width), lambda i: (i, 0))
    return pl.pallas_call(
        _inproj_kernel,
        grid=(N_TILES,),
        in_specs=[
            row_tile(D_MODEL),
            pl.BlockSpec((1, N_MOD, D_MODEL), lambda i: (i // N_LAT_TILES, 0, 0)),
            pl.BlockSpec((1, D_MODEL), lambda i: (0, 0)),
            _resident((D_MODEL, IN_WIDTH)),
            row_tile(LANES),
            row_tile(LANES),
            pl.BlockSpec((2, LANES), lambda i: (0, 0)),
        ],
        out_specs=[
            pl.BlockSpec((DA_HEADS, LANES, TM), lambda i: (0, 0, i)),
            row_tile(DA_HEADS * LANES),
            pl.BlockSpec((DA_HEADS, LANES, TM), lambda i: (0, 0, i)),
            pl.BlockSpec((GQA_KV_HEADS, LANES, TM), lambda i: (0, 0, i)),
            row_tile(LANES),
            pl.BlockSpec((LANES, TM), lambda i: (0, i)),
            row_tile(LRU_WIDTH),
            row_tile(LRU_WIDTH),
        ],
        out_shape=[
            jax.ShapeDtypeStruct((DA_HEADS, LANES, T_ALL), BF16),
            jax.ShapeDtypeStruct((T_ALL, DA_HEADS * LANES), BF16),
            jax.ShapeDtypeStruct((DA_HEADS, LANES, T_ALL), BF16),
            jax.ShapeDtypeStruct((GQA_KV_HEADS, LANES, T_ALL), BF16),
            jax.ShapeDtypeStruct((T_ALL, LANES), BF16),
            jax.ShapeDtypeStruct((LANES, T_ALL), BF16),
            jax.ShapeDtypeStruct((T_ALL, LRU_WIDTH), F32),
            jax.ShapeDtypeStruct((T_ALL, LRU_WIDTH), F32),
        ],
        compiler_params=_params(1),
        name="inproj",
    )(h, mods, g, w_in, cos_t, sin_t, qk_gain)


def _lru_tile_index(d, j):
    lat = jnp.where(d == 0, j - 1, N_LAT_TILES - j)
    return jnp.where(j == 0, N_LAT_TILES, lat)


def _lru_kernel(x_ref, cw_ref, cb_ref, wa_ref, wi_ref, ba_ref, bi_ref, lam_ref, o_ref, carry_ref):
    d = pl.program_id(0)
    j = pl.program_id(1)
    ti = _lru_tile_index(d, j)
    t0 = pl.multiple_of(ti * TM, TM)

    @pl.when(j == 0)
    def _():
        carry_ref[...] = jnp.zeros_like(carry_ref)

    x = x_ref[pl.ds(t0, TM), :]
    prev8 = x_ref[pl.ds(pl.multiple_of(jnp.maximum(t0 - 8, 0), 8), 8), :]
    next8 = x_ref[pl.ds(pl.multiple_of(jnp.minimum(t0 + TM, T_ALL - 8), 8), 8), :]
    has_prev = jnp.logical_and(ti != 0, ti != N_LAT_TILES)
    has_next = jnp.logical_and(ti != N_LAT_TILES - 1, ti != N_LAT_TILES)
    prev_row = jnp.where(has_prev, prev8[7:8], 0.0)
    next_row0 = jnp.where(has_next, next8[0:1], 0.0)
    next_row1 = jnp.where(has_next, next8[1:2], 0.0)
    row = lax.broadcasted_iota(jnp.int32, (TM, LRU_WIDTH), 0)
    xm1 = jnp.where(row == 0, prev_row, pltpu.roll(x, 1, 0))
    xp1 = jnp.where(row == TM - 1, next_row0, pltpu.roll(x, TM - 1, 0))
    xp2 = jnp.where(row == TM - 2, next_row0,
                    jnp.where(row == TM - 1, next_row1, pltpu.roll(x, TM - 2, 0)))
    cw = cw_ref[...]
    xc = xm1 * cw[0:1] + x * cw[1:2] + xp1 * cw[2:3] + xp2 * cw[3:4] + cb_ref[...]

    xcb = xc.astype(BF16)
    r = _sigmoid(jnp.dot(xcb, wa_ref[0], preferred_element_type=F32) + ba_ref[0])
    gi = _sigmoid(jnp.dot(xcb, wi_ref[0], preferred_element_type=F32) + bi_ref[0])
    neg_lam = -lam_ref[0]
    softplus = jnp.maximum(neg_lam, 0.0) + jnp.log1p(jnp.exp(-jnp.abs(neg_lam)))
    log_a = (-LRU_C) * r * softplus
    a = jnp.exp(log_a)
    u = jnp.sqrt(-jnp.tanh(log_a) * (a * a + 1.0)) * (gi * xc)

    def scan(A, B, forward):
        k = 1
        while k < TM:
            if forward:
                keep = row >= k
                As = jnp.where(keep, pltpu.roll(A, k, 0), 1.0)
                Bs = jnp.where(keep, pltpu.roll(B, k, 0), 0.0)
            else:
                keep = row < TM - k
                As = jnp.where(keep, pltpu.roll(A, TM - k, 0), 1.0)
                Bs = jnp.where(keep, pltpu.roll(B, TM - k, 0), 0.0)
            B = A * Bs + B
            A = A * As
            k *= 2
        return A, B

    carry = carry_ref[0:1, :]

    @pl.when(d == 0)
    def _():
        A, B = scan(a, u, True)
        hs = B + A * carry
        o_ref[0] = hs
        carry_ref[0:1, :] = hs[TM - 1:TM]

    @pl.when(d == 1)
    def _():
        A, B = scan(a, u, False)
        hs = B + A * carry
        o_ref[0] = hs
        carry_ref[0:1, :] = hs[0:1]


def _lru(xb, conv_w, conv_b, wa, wi, ba, bi, lam):
    per_dir = lambda *tail: pl.BlockSpec((1,) + tail, lambda d, j: (d,) + (0,) * len(tail))
    return pl.pallas_call(
        _lru_kernel,
        grid=(2, N_TILES),
        in_specs=[
            _resident((T_ALL, LRU_WIDTH)),
            pl.BlockSpec((4, LRU_WIDTH), lambda d, j: (0, 0)),
            pl.BlockSpec((1, LRU_WIDTH), lambda d, j: (0, 0)),
            per_dir(LRU_WIDTH, LRU_WIDTH),
            per_dir(LRU_WIDTH, LRU_WIDTH),
            per_dir(1, LRU_WIDTH),
            per_dir(1, LRU_WIDTH),
            per_dir(1, LRU_WIDTH),
        ],
        out_specs=pl.BlockSpec((1, TM, LRU_WIDTH), lambda d, j: (d, _lru_tile_index(d, j), 0)),
        out_shape=jax.ShapeDtypeStruct((2, T_ALL, LRU_WIDTH), F32),
        scratch_shapes=[pltpu.VMEM((8, LRU_WIDTH), F32)],
        compiler_params=_params(2),
        name="rglru",
    )(xb, conv_w, conv_b, wa, wi, ba, bi, lam)


def _softmax_chunk(s, m_old, l_old):
    m_new = jnp.maximum(m_old, jnp.max(s, axis=0, keepdims=True))
    alpha = jnp.exp2(m_old - m_new)
    p = jnp.exp2(s - m_new)
    l_new = alpha * l_old + jnp.sum(p, axis=0, keepdims=True)
    return p.astype(BF16), m_new, l_new, alpha


def _attend_ctx(kc, vc, weights):
    outs = []
    for w in weights:
        s = jnp.dot(kc, w, preferred_element_type=F32)
        p, _, l, _ = _softmax_chunk(s, jnp.full((1, CTX_LEN), NEG_BIG, F32),
                                    jnp.zeros((1, CTX_LEN), F32))
        outs.append(jnp.dot(vc, p, preferred_element_type=F32) * (1.0 / l))
    return outs


def _key_norm_bound(k_ref, kmax_ref):
    r = lax.broadcasted_iota(jnp.int32, (LANES, LANES), 0) // HEAD_DIM
    c = lax.broadcasted_iota(jnp.int32, (LANES, LANES), 1) // HEAD_DIM
    ones = jnp.where(r == c, 1.0, 0.0).astype(BF16)

    def chunk(i, best):
        kf = k_ref[pl.ds(pl.multiple_of(i * TK, TK), TK), :].astype(F32)
        n2 = jnp.dot((kf * kf).astype(BF16), ones, preferred_element_type=F32)
        return jnp.maximum(best, jnp.max(n2, axis=0, keepdims=True))

    kmax_ref[0:1, :] = lax.fori_loop(0, N_K_CHUNKS, chunk, jnp.zeros((1, LANES), F32))


def _attend_lat_fast(k_ref, v_chunk, weights, shift, acc_ref, p_refs):
    maps = range(len(weights))

    def scores_exp(c, p_ref, l):
        kc = k_ref[pl.ds(pl.multiple_of(c * TK_FAST, TK_FAST), TK_FAST), :]
        l_new = []
        for j in maps:
            p = jnp.exp2(jnp.dot(kc, weights[j], preferred_element_type=F32) - shift[j])
            l_new.append(l[j] + jnp.sum(p, axis=0, keepdims=True))
            p_ref[j] = p.astype(BF16)
        return tuple(l_new)

    def values(c, p_ref):
        vc = v_chunk(pl.multiple_of(c * TK_FAST, TK_FAST), TK_FAST)
        for j in maps:
            acc_ref[j] += jnp.dot(vc, p_ref[j], preferred_element_type=F32)

    acc_ref[...] = jnp.zeros_like(acc_ref)
    l = scores_exp(0, p_refs[0], tuple(jnp.zeros((1, TQ), F32) for _ in maps))

    def chunk_pair(i, l):
        c0 = 2 * i
        l = scores_exp(c0 + 1, p_refs[1], l)
        values(c0, p_refs[0])
        l = scores_exp(c0 + 2, p_refs[0], l)
        values(c0 + 1, p_refs[1])
        return l

    n_pairs = (N_K_FAST - 1) // 2
    l = lax.fori_loop(0, n_pairs, chunk_pair, l)
    if N_K_FAST % 2 == 0:
        l = scores_exp(N_K_FAST - 1, p_refs[1], l)
        values(N_K_FAST - 2, p_refs[0])
        values(N_K_FAST - 1, p_refs[1])
    else:
        values(N_K_FAST - 1, p_refs[0])
    return l


def _attend_lat_exact(k_ref, v_chunk, weights, acc_ref, s_refs, p_refs):
    maps = range(len(weights))

    def scores(c, s_ref):
        kc = k_ref[pl.ds(pl.multiple_of(c * TK, TK), TK), :]
        for j in maps:
            s_ref[j] = jnp.dot(kc, weights[j], preferred_element_type=F32)

    def softmax(s_ref, p_ref, m, l):
        m_new, l_new, alpha = [], [], []
        for j in maps:
            m_j = jnp.maximum(m[j], jnp.max(s_ref[j], axis=0, keepdims=True))
            a_j = jnp.exp2(m[j] - m_j)
            p = jnp.exp2(s_ref[j] - m_j)
            l_new.append(a_j * l[j] + jnp.sum(p, axis=0, keepdims=True))
            p_ref[j, 0:TK, :] = p.astype(BF16)
            m_new.append(m_j)
            alpha.append(a_j)
        return m_new, l_new, alpha

    def values(c, p_ref, alpha):
        vc = v_chunk(pl.multiple_of(c * TK, TK), TK)
        for j in maps:
            acc_ref[j] = alpha[j] * acc_ref[j] + jnp.dot(vc, p_ref[j, 0:TK, :],
                                                         preferred_element_type=F32)

    acc_ref[...] = jnp.zeros_like(acc_ref)
    p_refs[1][...] = jnp.zeros_like(p_refs[1])
    scores(0, s_refs[0])
    stat = lambda v: [jnp.full((1, TQ), v, F32) for _ in maps]

    def chunk_pair(i, carry):
        m, l, alpha_b = carry
        c0 = 2 * i
        scores(c0 + 1, s_refs[1])
        m, l, alpha_a = softmax(s_refs[0], p_refs[0], m, l)
        values(jnp.maximum(c0 - 1, 0), p_refs[1], alpha_b)
        scores(jnp.minimum(c0 + 2, N_K_CHUNKS - 1), s_refs[0])
        m, l, alpha_b = softmax(s_refs[1], p_refs[1], m, l)
        values(c0, p_refs[0], alpha_a)
        return m, l, alpha_b

    _, l, alpha_b = lax.fori_loop(0, N_K_CHUNKS // 2, chunk_pair,
                                  (stat(NEG_BIG), stat(0.0), stat(1.0)))
    values(N_K_CHUNKS - 1, p_refs[1], alpha_b)
    return l


def _attend(ctx_only, k_ref, vt_at, weights, halves, scratch):
    if ctx_only:
        return _attend_ctx(k_ref[...], vt_at(0, CTX_LEN), weights)
    acc_ref, l_ref, kmax_ref, s0_ref, s1_ref, p0_ref, p1_ref = scratch
    maps = range(len(weights))

    @pl.when(pl.program_id(1) == 0)
    def _():
        _key_norm_bound(k_ref, kmax_ref)

    lane = lax.broadcasted_iota(jnp.int32, (1, LANES), 1)
    shift = []
    for j in maps:
        k2 = jnp.max(jnp.where(lane // HEAD_DIM == halves[j], kmax_ref[0:1, :], 0.0),
                     axis=-1, keepdims=True)
        wf = weights[j].astype(F32)
        q2 = jnp.sum(wf * wf, axis=0, keepdims=True)
        shift.append(jnp.sqrt(q2 * k2) * BOUND_SLACK - SHIFT_GUARD)

    l = _attend_lat_fast(k_ref, vt_at, weights, shift, acc_ref, (p0_ref, p1_ref))
    n_bad = jnp.zeros((), F32)
    for j in maps:
        l_ref[j:j + 1, :] = l[j]
        ok = jnp.logical_and(l[j] >= L_MIN, l[j] <= L_MAX)
        n_bad = n_bad + jnp.sum(jnp.where(ok, 0.0, 1.0))

    @pl.when(n_bad > 0.0)
    def _():
        l = _attend_lat_exact(k_ref, vt_at, weights, acc_ref, (s0_ref, s1_ref), (p0_ref, p1_ref))
        for j in maps:
            l_ref[j:j + 1, :] = l[j]

    return [acc_ref[j] * (1.0 / l_ref[j:j + 1, :]) for j in maps]


def _attn_call(kernel, ctx_only, n_heads, dv, in_blocks, operands, out_width, name):
    if ctx_only:
        tq, n_q, n_keys, q0, k0 = CTX_LEN, 1, CTX_LEN, SEQ // CTX_LEN, SEQ // CTX_LEN
        scratch = []
    else:
        tq, n_q, n_keys, q0, k0 = TQ, SEQ // TQ, T_ALL, 0, 0
        scratch = [pltpu.VMEM((2, dv, TQ), F32), pltpu.VMEM((8, TQ), F32),
                   pltpu.VMEM((8, LANES), F32),
                   pltpu.VMEM((2, TK, TQ), F32), pltpu.VMEM((2, TK, TQ), F32),
                   pltpu.VMEM((2, TK_FAST, TQ), BF16), pltpu.VMEM((2, TK_FAST, TQ), BF16)]
    return pl.pallas_call(
        functools.partial(kernel, ctx_only),
        grid=(n_heads, n_q),
        in_specs=in_blocks(tq, n_keys, q0, k0),
        out_specs=pl.BlockSpec((tq, LANES), lambda h, i: (i, h)),
        out_shape=jax.ShapeDtypeStruct((n_q * tq, out_width), BF16),
        scratch_shapes=scratch,
        compiler_params=_params(2),
        name=name + ("_ctx" if ctx_only else ""),
    )(*operands)


def _da_kernel(lam_init, ctx_only, qt_ref, k_ref, vt_ref, lam_ref, g_ref, o_ref, *scratch):
    qt = qt_ref[0]
    row = lax.broadcasted_iota(jnp.int32, qt.shape, 0)
    zero = jnp.zeros_like(qt)
    weights = [jnp.where(row < HEAD_DIM, qt, zero), jnp.where(row >= HEAD_DIM, qt, zero)]
    o1, o2 = _attend(ctx_only, k_ref, lambda s, n: vt_ref[0, :, pl.ds(s, n)], weights, (0, 1),
                     scratch)
    lf = lam_ref[...]
    lam = (jnp.exp(jnp.sum(lf[0:1] * lf[1:2], axis=-1, keepdims=True))
           - jnp.exp(jnp.sum(lf[2:3] * lf[3:4], axis=-1, keepdims=True)) + lam_init)
    o = o1 - lam * o2
    ms = jnp.mean(o * o, axis=0, keepdims=True)
    on = (o * lax.rsqrt(ms + EPS)) * g_ref[...] * (1.0 - lam_init)
    o_ref[...] = on.T.astype(o_ref.dtype)


def _da_attention(qt, k, vt, da_lam, subln_g, lam_init, ctx_only):
    def in_blocks(tq, n_keys, q0, k0):
        return [
            pl.BlockSpec((1, LANES, tq), lambda h, i: (h, 0, i + q0)),
            pl.BlockSpec((n_keys, LANES), lambda h, i: (k0, h)),
            pl.BlockSpec((1, LANES, n_keys), lambda h, i: (h, 0, k0)),
            pl.BlockSpec((4, HEAD_DIM), lambda h, i: (0, 0)),
            pl.BlockSpec((LANES, 1), lambda h, i: (0, 0)),
        ]
    return _attn_call(functools.partial(_da_kernel, lam_init), ctx_only, DA_HEADS, LANES,
                      in_blocks, (qt, k, vt, da_lam, subln_g), DA_HEADS * LANES, "da_attention")


def _gqa_kernel(ctx_only, qt_ref, k_ref, vt_ref, o_ref, *scratch):
    g = pl.program_id(0)
    qt = qt_ref[0]
    zero = jnp.zeros((HEAD_DIM, qt.shape[1]), qt.dtype)
    weights = []
    for j in range(2):
        qj = qt[j * HEAD_DIM:(j + 1) * HEAD_DIM]
        weights.append(jnp.where(g == 0, jnp.concatenate([qj, zero], axis=0),
                                 jnp.concatenate([zero, qj], axis=0)))
    o0, o1 = _attend(ctx_only, k_ref, lambda s, n: vt_ref[:, pl.ds(s, n)], weights, (g, g),
                     scratch)
    o_ref[...] = jnp.concatenate([o0, o1], axis=0).T.astype(o_ref.dtype)


def _gqa_attention(qt, k, vt, ctx_only):
    def in_blocks(tq, n_keys, q0, k0):
        return [
            pl.BlockSpec((1, LANES, tq), lambda g, i: (g, 0, i + q0)),
            pl.BlockSpec((n_keys, LANES), lambda g, i: (k0, 0)),
            pl.BlockSpec((HEAD_DIM, n_keys), lambda g, i: (g, k0)),
        ]
    return _attn_call(_gqa_kernel, ctx_only, GQA_KV_HEADS, HEAD_DIM, in_blocks, (qt, k, vt),
                      GQA_KV_HEADS * LANES, "gqa_attention")


def _gelu_tanh(x):
    return 0.5 * x * (1.0 + jnp.tanh(math.sqrt(2.0 / math.pi) * (x + 0.044715 * (x * x * x))))


def _outproj_kernel(h_ref, mod_ref, a_ref, g_ref, hs_ref, gb_ref, w_ref, o_ref):
    mods = mod_ref[0]
    r = ((hs_ref[0] + hs_ref[1]) * _gelu_tanh(gb_ref[...])).astype(BF16)
    n_a = DA_HEADS * LANES
    n_g = n_a + GQA_KV_HEADS * LANES
    y = (jnp.dot(a_ref[...], w_ref[0:n_a, :], preferred_element_type=F32)
         + jnp.dot(g_ref[...], w_ref[n_a:n_g, :], preferred_element_type=F32)
         + jnp.dot(r, w_ref[n_g:, :], preferred_element_type=F32))
    o_ref[...] = h_ref[...] + mods[5:6] * y


def _outproj(h, mods, a, g, hs, gb, w_out, n_tiles):
    row_tile = lambda width: pl.BlockSpec((TM, width), lambda i: (i, 0))
    return pl.pallas_call(
        _outproj_kernel,
        grid=(n_tiles,),
        in_specs=[
            row_tile(D_MODEL),
            pl.BlockSpec((1, N_MOD, D_MODEL), lambda i: (i // N_LAT_TILES, 0, 0)),
            row_tile(DA_HEADS * LANES),
            row_tile(GQA_KV_HEADS * LANES),
            pl.BlockSpec((2, TM, LRU_WIDTH), lambda i: (0, i, 0)),
            row_tile(LRU_WIDTH),
            _resident((D_MODEL, D_MODEL)),
        ],
        out_specs=row_tile(D_MODEL),
        out_shape=jax.ShapeDtypeStruct((n_tiles * TM, D_MODEL), F32),
        compiler_params=_params(1),
        name="outproj",
    )(h, mods, a, g, hs, gb, w_out)


def _final_norm_kernel(h_ref, g_ref, o_ref):
    x = h_ref[...]
    ms = jnp.mean(x * x, axis=-1, keepdims=True)
    o_ref[...] = (x * lax.rsqrt(ms + EPS)) * g_ref[...]


def _final_norm(h, g):
    return pl.pallas_call(
        _final_norm_kernel,
        grid=(N_LAT_TILES,),
        in_specs=[pl.BlockSpec((TM, D_MODEL), lambda i: (i, 0)),
                  pl.BlockSpec((1, D_MODEL), lambda i: (0, 0))],
        out_specs=pl.BlockSpec((TM, D_MODEL), lambda i: (i, 0)),
        out_shape=jax.ShapeDtypeStruct((SEQ, D_MODEL), F32),
        compiler_params=_params(1),
        name="final_norm",
    )(h, g)


def _rope_tables():
    half = HEAD_DIM // 4
    t = jnp.arange(SEQ, dtype=jnp.int32)
    row = (t // GRID_W).astype(F32)
    col = (t % GRID_W).astype(F32)
    inv = ROPE_THETA ** (-jnp.arange(half, dtype=F32) / half)
    ang_r = row[:, None] * inv
    ang_c = col[:, None] * inv
    cos64 = jnp.concatenate([jnp.cos(ang_r)] * 2 + [jnp.cos(ang_c)] * 2, axis=-1)
    sin64 = jnp.concatenate([-jnp.sin(ang_r), jnp.sin(ang_r), -jnp.sin(ang_c), jnp.sin(ang_c)],
                            axis=-1)
    cos_t = jnp.concatenate([jnp.tile(cos64, (1, 2)), jnp.ones((CTX_LEN, LANES), F32)], axis=0)
    sin_t = jnp.concatenate([jnp.tile(sin64, (1, 2)), jnp.zeros((CTX_LEN, LANES), F32)], axis=0)
    return cos_t, sin_t


def _block_diag(w):
    eye = jnp.eye(LRU_BLOCKS, dtype=w.dtype)
    full = jnp.einsum('dkij,kl->dkilj', w, eye)
    return full.reshape(2, LRU_WIDTH, LRU_WIDTH).astype(BF16)


def _chunk_cols(w):
    return w.reshape(D_MODEL, N_FF_CHUNKS, FF_CHUNK).transpose(1, 0, 2).astype(BF16)


def kernel(x, c, ctx, c_ctx, ada_w, ada_b, norm_g, ffn1_w13, ffn1_w2, ffn2_w13, ffn2_w2,
           w_in, w_out, da_lam, da_subln_g, qk_norm_g, lru_conv_w, lru_conv_b,
           lru_wa, lru_ba, lru_wi, lru_bi, lru_lambda, final_g):
    assert x.shape == (1, SEQ, D_MODEL) and ctx.shape == (1, CTX_LEN, D_MODEL)
    cc = jnp.zeros((8, D_MODEL), F32).at[0].set(c[0]).at[1].set(c_ctx)
    mods_all = _ada_mods(cc, ada_w, ada_b).reshape(DEPTH, 8, N_MOD, D_MODEL)[:, :2]
    cos_t, sin_t = _rope_tables()

    h = jnp.concatenate([x[0], ctx[0]], axis=0)
    for l in range(DEPTH):
        last = l == DEPTH - 1
        lam_init = 0.8 - 0.6 * math.exp(-0.3 * l)
        mods = mods_all[l]

        def ffn_weights(w13, w2):
            return (_chunk_cols(w13[l][:, :D_FF]), _chunk_cols(w13[l][:, D_FF:]),
                    w2[l].reshape(N_FF_CHUNKS, FF_CHUNK, D_MODEL).astype(BF16))

        h = _ffn(h, mods, norm_g[l, 0:1], *ffn_weights(ffn1_w13, ffn1_w2), 0, N_TILES)

        qk_gain = jnp.tile(qk_norm_g[l], (1, 2))
        qt_da, k_da, vt_da, qt_g, k_g, vt_g, xb, gb = _inproj(
            h, mods, norm_g[l, 1:2], w_in[l].astype(BF16), cos_t, sin_t, qk_gain)

        hs = _lru(xb, lru_conv_w[l], lru_conv_b[l][None, :], _block_diag(lru_wa[l]),
                  _block_diag(lru_wi[l]), lru_ba[l][:, None, :], lru_bi[l][:, None, :],
                  lru_lambda[l][:, None, :])

        n_tiles = N_LAT_TILES if last else N_TILES
        da_args = (qt_da, k_da, vt_da, da_lam[l], da_subln_g[l][:, None], lam_init)
        a = _da_attention(*da_args, False)
        g = _gqa_attention(qt_g, k_g, vt_g, False)
        if not last:
            a = jnp.concatenate([a, _da_attention(*da_args, True)], axis=0)
            g = jnp.concatenate([g, _gqa_attention(qt_g, k_g, vt_g, True)], axis=0)

        h = _outproj(h, mods, a, g, hs, gb, w_out[l].astype(BF16), n_tiles)
        h = _ffn(h, mods, norm_g[l, 2:3], *ffn_weights(ffn2_w13, ffn2_w2), 6, n_tiles)
    return _final_norm(h, final_g[None, :])[None]
```

```python
import functools
import math

import jax
import jax.numpy as jnp
from jax import lax
from jax.experimental import pallas as pl
from jax.experimental.pallas import tpu as pltpu

D_MODEL = 1024
SEQ = 16384
CTX_LEN = 256
T_ALL = SEQ + CTX_LEN
DEPTH = 4
GRID_W = 64
HEAD_DIM = 64
SCALE = HEAD_DIM ** -0.5
LOG2E = math.log2(math.e)
ROPE_THETA = 10000.0
DA_HEADS = 4
GQA_KV_HEADS = 2
LRU_WIDTH = 256
LRU_BLOCKS = 4
LRU_BLOCK = LRU_WIDTH // LRU_BLOCKS
LRU_C = 8.0
IN_WIDTH = 2560
D_FF = 2816
N_MOD = 9
EPS = 1e-6

LANES = 128
MXU_DIM = 256

TM = 256
N_TILES = T_ALL // TM
N_LAT_TILES = SEQ // TM
FF_CHUNK = MXU_DIM
N_FF_CHUNKS = D_FF // FF_CHUNK
TQ = 1024
TK = 640
N_K_CHUNKS = T_ALL // TK
TK_FAST = 1280
N_K_FAST = T_ALL // TK_FAST
NEG_BIG = -1e30
SHIFT_GUARD = 32.0
BOUND_SLACK = 1.02
L_MIN = 2.0 ** -60
L_MAX = 2.0 ** 100
VMEM_LIMIT = 48 * 1024 * 1024

F32 = jnp.float32
BF16 = jnp.bfloat16


def _params(n_axes):
    return pltpu.CompilerParams(dimension_semantics=("arbitrary",) * n_axes,
                                vmem_limit_bytes=VMEM_LIMIT)


def _resident(shape):
    zeros = (0,) * len(shape)
    return pl.BlockSpec(shape, lambda *_: zeros, pipeline_mode=pl.Buffered(1))


def _sigmoid(x):
    return 1.0 / (1.0 + jnp.exp(-x))


def _norm_mod(x, g, shift, scale):
    ms = jnp.mean(x * x, axis=-1, keepdims=True)
    y = (x * lax.rsqrt(ms + EPS)) * g
    return y * (1.0 + scale) + shift


ADA_COLS = 1152


def _ada_kernel(c_ref, w_ref, b_ref, o_ref):
    c = c_ref[...]
    s = (c * _sigmoid(c)).astype(BF16)
    o_ref[0] = jnp.dot(s, w_ref[0].astype(BF16), preferred_element_type=F32) + b_ref[0]


def _ada_mods(cc, ada_w, ada_b):
    width = N_MOD * D_MODEL
    return pl.pallas_call(
        _ada_kernel,
        grid=(DEPTH, width // ADA_COLS),
        in_specs=[
            pl.BlockSpec((8, D_MODEL), lambda l, j: (0, 0)),
            pl.BlockSpec((1, D_MODEL, ADA_COLS), lambda l, j: (l, 0, j)),
            pl.BlockSpec((1, 1, ADA_COLS), lambda l, j: (l, 0, j)),
        ],
        out_specs=pl.BlockSpec((1, 8, ADA_COLS), lambda l, j: (l, 0, j)),
        out_shape=jax.ShapeDtypeStruct((DEPTH, 8, width), F32),
        compiler_params=_params(2),
        name="ada_mods",
    )(cc, ada_w, ada_b.reshape(DEPTH, 1, width))


def _ffn_kernel(i0, h_ref, mod_ref, g_ref, w13_ref, w2_ref, o_ref, acc_ref, gate_ref):
    x = h_ref[...]
    mods = mod_ref[0]
    z = _norm_mod(x, g_ref[...], mods[i0:i0 + 1], mods[i0 + 1:i0 + 2]).astype(BF16)
    acc_ref[...] = jnp.zeros_like(acc_ref)

    def gate(c):
        lo = c * FF_CHUNK
        a = jnp.dot(z, w13_ref[0, :, lo:lo + FF_CHUNK], preferred_element_type=F32)
        b = jnp.dot(z, w13_ref[0, :, D_FF + lo:D_FF + lo + FF_CHUNK], preferred_element_type=F32)
        return (a * _sigmoid(a) * b).astype(BF16)

    def down(c):
        acc_ref[...] += jnp.dot(gate_ref[...], w2_ref[0, c * FF_CHUNK:(c + 1) * FF_CHUNK, :],
                                preferred_element_type=F32)

    gate_ref[...] = gate(0)
    for c in range(1, N_FF_CHUNKS):
        down(c - 1)
        gate_ref[...] = gate(c)
    down(N_FF_CHUNKS - 1)
    o_ref[...] = x + (0.5 * mods[i0 + 2:i0 + 3]) * acc_ref[...]


def _layer_resident(shape, layer):
    index = (layer,) + (0,) * len(shape)
    return pl.BlockSpec((1,) + shape, lambda *_: index, pipeline_mode=pl.Buffered(1))


def _ffn(h, mods, g, w13, w2, layer, i0, n_tiles):
    return pl.pallas_call(
        functools.partial(_ffn_kernel, i0),
        grid=(n_tiles,),
        in_specs=[
            pl.BlockSpec((TM, D_MODEL), lambda i: (i, 0)),
            pl.BlockSpec((1, N_MOD, D_MODEL), lambda i: (i // N_LAT_TILES, 0, 0)),
            pl.BlockSpec((1, D_MODEL), lambda i: (0, 0)),
            _layer_resident((D_MODEL, 2 * D_FF), layer),
            _layer_resident((D_FF, D_MODEL), layer),
        ],
        out_specs=pl.BlockSpec((TM, D_MODEL), lambda i: (i, 0)),
        out_shape=jax.ShapeDtypeStruct((n_tiles * TM, D_MODEL), F32),
        scratch_shapes=[pltpu.VMEM((TM, D_MODEL), F32), pltpu.VMEM((TM, FF_CHUNK), BF16)],
        compiler_params=_params(1),
        name="ffn",
    )(h, mods, g, w13, w2)


def _swap16(x):
    lane = lax.broadcasted_iota(jnp.int32, x.shape, 1)
    return jnp.where(lane % 32 < 16, pltpu.roll(x, LANES - 16, 1), pltpu.roll(x, 16, 1))


def _rope(x, cos, sin):
    return x * cos + _swap16(x) * sin


def _head_mean_sq(x):
    sq = x * x
    hi = sq.astype(BF16)
    lo = (sq - hi.astype(F32)).astype(BF16)
    r = lax.broadcasted_iota(jnp.int32, (LANES, LANES), 0) // HEAD_DIM
    c = lax.broadcasted_iota(jnp.int32, (LANES, LANES), 1) // HEAD_DIM
    ones = jnp.where(r == c, 1.0, 0.0).astype(BF16)
    tot = (jnp.dot(hi, ones, preferred_element_type=F32)
           + jnp.dot(lo, ones, preferred_element_type=F32))
    return tot * (1.0 / HEAD_DIM)


def _inproj_kernel(h_ref, mod_ref, g_ref, w_ref, cos_ref, sin_ref, qkg_ref,
                   qt_da_ref, k_da_ref, vt_da_ref, qt_g_ref, k_g_ref, vt_g_ref, xb_ref, gb_ref):
    mods = mod_ref[0]
    z = _norm_mod(h_ref[...], g_ref[...], mods[3:4], mods[4:5]).astype(BF16)
    proj = jnp.dot(z, w_ref[0], preferred_element_type=F32)
    cos = cos_ref[...]
    sin = sin_ref[...]
    q_scale = SCALE * LOG2E

    def slab(j):
        return proj[:, j * LANES:(j + 1) * LANES]

    for h in range(DA_HEADS):
        q = _rope(slab(h), cos, sin) * q_scale
        qt_da_ref[h] = q.T.astype(BF16)
        k_da_ref[:, h * LANES:(h + 1) * LANES] = _rope(slab(4 + h), cos, sin).astype(BF16)
        vt_da_ref[h] = slab(8 + h).T.astype(BF16)

    gq_gain = qkg_ref[0:1, :]
    gk_gain = qkg_ref[1:2, :]
    for g in range(GQA_KV_HEADS):
        q = slab(12 + g)
        q = (q * lax.rsqrt(_head_mean_sq(q) + EPS)) * gq_gain
        qt_g_ref[g] = (_rope(q, cos, sin) * q_scale).T.astype(BF16)
    k = slab(14)
    k = (k * lax.rsqrt(_head_mean_sq(k) + EPS)) * gk_gain
    k_g_ref[...] = _rope(k, cos, sin).astype(BF16)
    vt_g_ref[...] = slab(15).T.astype(BF16)
    xb_ref[...] = proj[:, 2048:2304]
    gb_ref[...] = proj[:, 2304:2560]


def _inproj(h, mods, g, w_in, layer, cos_t, sin_t, qk_gain):
    row_tile = lambda width: pl.BlockSpec((TM, width), lambda i: (i, 0))
    return pl.pallas_call(
        _inproj_kernel,
        grid=(N_TILES,),
        in_specs=[
            row_tile(D_MODEL),
            pl.BlockSpec((1, N_MOD, D_MODEL), lambda i: (i // N_LAT_TILES, 0, 0)),
            pl.BlockSpec((1, D_MODEL), lambda i: (0, 0)),
            _layer_resident((D_MODEL, IN_WIDTH), layer),
            row_tile(LANES),
            row_tile(LANES),
            pl.BlockSpec((2, LANES), lambda i: (0, 0)),
        ],
        out_specs=[
            pl.BlockSpec((DA_HEADS, LANES, TM), lambda i: (0, 0, i)),
            row_tile(DA_HEADS * LANES),
            pl.BlockSpec((DA_HEADS, LANES, TM), lambda i: (0, 0, i)),
            pl.BlockSpec((GQA_KV_HEADS, LANES, TM), lambda i: (0, 0, i)),
            row_tile(LANES),
            pl.BlockSpec((LANES, TM), lambda i: (0, i)),
            row_tile(LRU_WIDTH),
            row_tile(LRU_WIDTH),
        ],
        out_shape=[
            jax.ShapeDtypeStruct((DA_HEADS, LANES, T_ALL), BF16),
            jax.ShapeDtypeStruct((T_ALL, DA_HEADS * LANES), BF16),
            jax.ShapeDtypeStruct((DA_HEADS, LANES, T_ALL), BF16),
            jax.ShapeDtypeStruct((GQA_KV_HEADS, LANES, T_ALL), BF16),
            jax.ShapeDtypeStruct((T_ALL, LANES), BF16),
            jax.ShapeDtypeStruct((LANES, T_ALL), BF16),
            jax.ShapeDtypeStruct((T_ALL, LRU_WIDTH), F32),
            jax.ShapeDtypeStruct((T_ALL, LRU_WIDTH), F32),
        ],
        compiler_params=_params(1),
        name="inproj",
    )(h, mods, g, w_in, cos_t, sin_t, qk_gain)


def _lru_tile_index(d, j):
    lat = jnp.where(d == 0, j - 1, N_LAT_TILES - j)
    return jnp.where(j == 0, N_LAT_TILES, lat)


def _lru_kernel(x_ref, cw_ref, cb_ref, wa_ref, wi_ref, ba_ref, bi_ref, lam_ref, o_ref, carry_ref):
    d = pl.program_id(0)
    j = pl.program_id(1)
    ti = _lru_tile_index(d, j)
    t0 = pl.multiple_of(ti * TM, TM)

    @pl.when(j == 0)
    def _():
        carry_ref[...] = jnp.zeros_like(carry_ref)

    x = x_ref[pl.ds(t0, TM), :]
    prev8 = x_ref[pl.ds(pl.multiple_of(jnp.maximum(t0 - 8, 0), 8), 8), :]
    next8 = x_ref[pl.ds(pl.multiple_of(jnp.minimum(t0 + TM, T_ALL - 8), 8), 8), :]
    has_prev = jnp.logical_and(ti != 0, ti != N_LAT_TILES)
    has_next = jnp.logical_and(ti != N_LAT_TILES - 1, ti != N_LAT_TILES)
    prev_row = jnp.where(has_prev, prev8[7:8], 0.0)
    next_row0 = jnp.where(has_next, next8[0:1], 0.0)
    next_row1 = jnp.where(has_next, next8[1:2], 0.0)
    row = lax.broadcasted_iota(jnp.int32, (TM, LRU_WIDTH), 0)
    xm1 = jnp.where(row == 0, prev_row, pltpu.roll(x, 1, 0))
    xp1 = jnp.where(row == TM - 1, next_row0, pltpu.roll(x, TM - 1, 0))
    xp2 = jnp.where(row == TM - 2, next_row0,
                    jnp.where(row == TM - 1, next_row1, pltpu.roll(x, TM - 2, 0)))
    cw = cw_ref[...]
    xc = xm1 * cw[0:1] + x * cw[1:2] + xp1 * cw[2:3] + xp2 * cw[3:4] + cb_ref[...]

    xcb = xc.astype(BF16)
    r = _sigmoid(jnp.dot(xcb, wa_ref[0], preferred_element_type=F32) + ba_ref[0])
    gi = _sigmoid(jnp.dot(xcb, wi_ref[0], preferred_element_type=F32) + bi_ref[0])
    neg_lam = -lam_ref[0]
    softplus = jnp.maximum(neg_lam, 0.0) + jnp.log1p(jnp.exp(-jnp.abs(neg_lam)))
    log_a = (-LRU_C) * r * softplus
    a = jnp.exp(log_a)
    u = jnp.sqrt(-jnp.tanh(log_a) * (a * a + 1.0)) * (gi * xc)

    def scan(A, B, forward):
        k = 1
        while k < TM:
            if forward:
                keep = row >= k
                As = jnp.where(keep, pltpu.roll(A, k, 0), 1.0)
                Bs = jnp.where(keep, pltpu.roll(B, k, 0), 0.0)
            else:
                keep = row < TM - k
                As = jnp.where(keep, pltpu.roll(A, TM - k, 0), 1.0)
                Bs = jnp.where(keep, pltpu.roll(B, TM - k, 0), 0.0)
            B = A * Bs + B
            A = A * As
            k *= 2
        return A, B

    carry = carry_ref[0:1, :]

    @pl.when(d == 0)
    def _():
        A, B = scan(a, u, True)
        hs = B + A * carry
        o_ref[0] = hs
        carry_ref[0:1, :] = hs[TM - 1:TM]

    @pl.when(d == 1)
    def _():
        A, B = scan(a, u, False)
        hs = B + A * carry
        o_ref[0] = hs
        carry_ref[0:1, :] = hs[0:1]


def _lru(xb, conv_w, conv_b, wa, wi, ba, bi, lam):
    per_dir = lambda *tail: pl.BlockSpec((1,) + tail, lambda d, j: (d,) + (0,) * len(tail))
    return pl.pallas_call(
        _lru_kernel,
        grid=(2, N_TILES),
        in_specs=[
            _resident((T_ALL, LRU_WIDTH)),
            pl.BlockSpec((4, LRU_WIDTH), lambda d, j: (0, 0)),
            pl.BlockSpec((1, LRU_WIDTH), lambda d, j: (0, 0)),
            per_dir(LRU_WIDTH, LRU_WIDTH),
            per_dir(LRU_WIDTH, LRU_WIDTH),
            per_dir(1, LRU_WIDTH),
            per_dir(1, LRU_WIDTH),
            per_dir(1, LRU_WIDTH),
        ],
        out_specs=pl.BlockSpec((1, TM, LRU_WIDTH), lambda d, j: (d, _lru_tile_index(d, j), 0)),
        out_shape=jax.ShapeDtypeStruct((2, T_ALL, LRU_WIDTH), F32),
        scratch_shapes=[pltpu.VMEM((8, LRU_WIDTH), F32)],
        compiler_params=_params(2),
        name="rglru",
    )(xb, conv_w, conv_b, wa, wi, ba, bi, lam)


def _softmax_chunk(s, m_old, l_old):
    m_new = jnp.maximum(m_old, jnp.max(s, axis=0, keepdims=True))
    alpha = jnp.exp2(m_old - m_new)
    p = jnp.exp2(s - m_new)
    l_new = alpha * l_old + jnp.sum(p, axis=0, keepdims=True)
    return p.astype(BF16), m_new, l_new, alpha


def _attend_ctx(kc, vc, weights):
    outs = []
    for w in weights:
        s = jnp.dot(kc, w, preferred_element_type=F32)
        p, _, l, _ = _softmax_chunk(s, jnp.full((1, CTX_LEN), NEG_BIG, F32),
                                    jnp.zeros((1, CTX_LEN), F32))
        outs.append(jnp.dot(vc, p, preferred_element_type=F32) * (1.0 / l))
    return outs


def _key_norm_bound(k_ref, kmax_ref):
    r = lax.broadcasted_iota(jnp.int32, (LANES, LANES), 0) // HEAD_DIM
    c = lax.broadcasted_iota(jnp.int32, (LANES, LANES), 1) // HEAD_DIM
    ones = jnp.where(r == c, 1.0, 0.0).astype(BF16)

    def chunk(i, best):
        kf = k_ref[pl.ds(pl.multiple_of(i * TK, TK), TK), :].astype(F32)
        n2 = jnp.dot((kf * kf).astype(BF16), ones, preferred_element_type=F32)
        return jnp.maximum(best, jnp.max(n2, axis=0, keepdims=True))

    kmax_ref[0:1, :] = lax.fori_loop(0, N_K_CHUNKS, chunk, jnp.zeros((1, LANES), F32))


def _attend_lat_fast(k_ref, v_chunk, weights, shift, acc_ref, p_refs):
    maps = range(len(weights))

    def scores_exp(c, p_ref, l):
        kc = k_ref[pl.ds(pl.multiple_of(c * TK_FAST, TK_FAST), TK_FAST), :]
        l_new = []
        for j in maps:
            p = jnp.exp2(jnp.dot(kc, weights[j], preferred_element_type=F32) - shift[j])
            l_new.append(l[j] + jnp.sum(p, axis=0, keepdims=True))
            p_ref[j] = p.astype(BF16)
        return tuple(l_new)

    def values(c, p_ref):
        vc = v_chunk(pl.multiple_of(c * TK_FAST, TK_FAST), TK_FAST)
        for j in maps:
            acc_ref[j] += jnp.dot(vc, p_ref[j], preferred_element_type=F32)

    acc_ref[...] = jnp.zeros_like(acc_ref)
    l = scores_exp(0, p_refs[0], tuple(jnp.zeros((1, TQ), F32) for _ in maps))

    def chunk_pair(i, l):
        c0 = 2 * i
        l = scores_exp(c0 + 1, p_refs[1], l)
        values(c0, p_refs[0])
        l = scores_exp(c0 + 2, p_refs[0], l)
        values(c0 + 1, p_refs[1])
        return l

    n_pairs = (N_K_FAST - 1) // 2
    l = lax.fori_loop(0, n_pairs, chunk_pair, l)
    if N_K_FAST % 2 == 0:
        l = scores_exp(N_K_FAST - 1, p_refs[1], l)
        values(N_K_FAST - 2, p_refs[0])
        values(N_K_FAST - 1, p_refs[1])
    else:
        values(N_K_FAST - 1, p_refs[0])
    return l


def _attend_lat_exact(k_ref, v_chunk, weights, acc_ref, s_refs, p_refs):
    maps = range(len(weights))

    def scores(c, s_ref):
        kc = k_ref[pl.ds(pl.multiple_of(c * TK, TK), TK), :]
        for j in maps:
            s_ref[j] = jnp.dot(kc, weights[j], preferred_element_type=F32)

    def softmax(s_ref, p_ref, m, l):
        m_new, l_new, alpha = [], [], []
        for j in maps:
            m_j = jnp.maximum(m[j], jnp.max(s_ref[j], axis=0, keepdims=True))
            a_j = jnp.exp2(m[j] - m_j)
            p = jnp.exp2(s_ref[j] - m_j)
            l_new.append(a_j * l[j] + jnp.sum(p, axis=0, keepdims=True))
            p_ref[j, 0:TK, :] = p.astype(BF16)
            m_new.append(m_j)
            alpha.append(a_j)
        return m_new, l_new, alpha

    def values(c, p_ref, alpha):
        vc = v_chunk(pl.multiple_of(c * TK, TK), TK)
        for j in maps:
            acc_ref[j] = alpha[j] * acc_ref[j] + jnp.dot(vc, p_ref[j, 0:TK, :],
                                                         preferred_element_type=F32)

    acc_ref[...] = jnp.zeros_like(acc_ref)
    p_refs[1][...] = jnp.zeros_like(p_refs[1])
    scores(0, s_refs[0])
    stat = lambda v: [jnp.full((1, TQ), v, F32) for _ in maps]

    def chunk_pair(i, carry):
        m, l, alpha_b = carry
        c0 = 2 * i
        scores(c0 + 1, s_refs[1])
        m, l, alpha_a = softmax(s_refs[0], p_refs[0], m, l)
        values(jnp.maximum(c0 - 1, 0), p_refs[1], alpha_b)
        scores(jnp.minimum(c0 + 2, N_K_CHUNKS - 1), s_refs[0])
        m, l, alpha_b = softmax(s_refs[1], p_refs[1], m, l)
        values(c0, p_refs[0], alpha_a)
        return m, l, alpha_b

    _, l, alpha_b = lax.fori_loop(0, N_K_CHUNKS // 2, chunk_pair,
                                  (stat(NEG_BIG), stat(0.0), stat(1.0)))
    values(N_K_CHUNKS - 1, p_refs[1], alpha_b)
    return l


def _attend(ctx_only, k_ref, vt_at, weights, halves, scratch):
    if ctx_only:
        return _attend_ctx(k_ref[...], vt_at(0, CTX_LEN), weights)
    acc_ref, l_ref, kmax_ref, s0_ref, s1_ref, p0_ref, p1_ref = scratch
    maps = range(len(weights))

    @pl.when(pl.program_id(1) == 0)
    def _():
        _key_norm_bound(k_ref, kmax_ref)

    lane = lax.broadcasted_iota(jnp.int32, (1, LANES), 1)
    shift = []
    for j in maps:
        k2 = jnp.max(jnp.where(lane // HEAD_DIM == halves[j], kmax_ref[0:1, :], 0.0),
                     axis=-1, keepdims=True)
        wf = weights[j].astype(F32)
        q2 = jnp.sum(wf * wf, axis=0, keepdims=True)
        shift.append(jnp.sqrt(q2 * k2) * BOUND_SLACK - SHIFT_GUARD)

    l = _attend_lat_fast(k_ref, vt_at, weights, shift, acc_ref, (p0_ref, p1_ref))
    n_bad = jnp.zeros((), F32)
    for j in maps:
        l_ref[j:j + 1, :] = l[j]
        ok = jnp.logical_and(l[j] >= L_MIN, l[j] <= L_MAX)
        n_bad = n_bad + jnp.sum(jnp.where(ok, 0.0, 1.0))

    @pl.when(n_bad > 0.0)
    def _():
        l = _attend_lat_exact(k_ref, vt_at, weights, acc_ref, (s0_ref, s1_ref), (p0_ref, p1_ref))
        for j in maps:
            l_ref[j:j + 1, :] = l[j]

    return [acc_ref[j] * (1.0 / l_ref[j:j + 1, :]) for j in maps]


def _attn_call(kernel, ctx_only, n_heads, dv, in_blocks, operands, out_width, name):
    if ctx_only:
        tq, n_q, n_keys, q0, k0 = CTX_LEN, 1, CTX_LEN, SEQ // CTX_LEN, SEQ // CTX_LEN
        scratch = []
    else:
        tq, n_q, n_keys, q0, k0 = TQ, SEQ // TQ, T_ALL, 0, 0
        scratch = [pltpu.VMEM((2, dv, TQ), F32), pltpu.VMEM((8, TQ), F32),
                   pltpu.VMEM((8, LANES), F32),
                   pltpu.VMEM((2, TK, TQ), F32), pltpu.VMEM((2, TK, TQ), F32),
                   pltpu.VMEM((2, TK_FAST, TQ), BF16), pltpu.VMEM((2, TK_FAST, TQ), BF16)]
    return pl.pallas_call(
        functools.partial(kernel, ctx_only),
        grid=(n_heads, n_q),
        in_specs=in_blocks(tq, n_keys, q0, k0),
        out_specs=pl.BlockSpec((tq, LANES), lambda h, i: (i, h)),
        out_shape=jax.ShapeDtypeStruct((n_q * tq, out_width), BF16),
        scratch_shapes=scratch,
        compiler_params=_params(2),
        name=name + ("_ctx" if ctx_only else ""),
    )(*operands)


def _da_kernel(lam_init, ctx_only, qt_ref, k_ref, vt_ref, lam_ref, g_ref, o_ref, *scratch):
    qt = qt_ref[0]
    row = lax.broadcasted_iota(jnp.int32, qt.shape, 0)
    zero = jnp.zeros_like(qt)
    weights = [jnp.where(row < HEAD_DIM, qt, zero), jnp.where(row >= HEAD_DIM, qt, zero)]
    o1, o2 = _attend(ctx_only, k_ref, lambda s, n: vt_ref[0, :, pl.ds(s, n)], weights, (0, 1),
                     scratch)
    lf = lam_ref[...]
    lam = (jnp.exp(jnp.sum(lf[0:1] * lf[1:2], axis=-1, keepdims=True))
           - jnp.exp(jnp.sum(lf[2:3] * lf[3:4], axis=-1, keepdims=True)) + lam_init)
    o = o1 - lam * o2
    ms = jnp.mean(o * o, axis=0, keepdims=True)
    on = (o * lax.rsqrt(ms + EPS)) * g_ref[...] * (1.0 - lam_init)
    o_ref[...] = on.T.astype(o_ref.dtype)


def _da_attention(qt, k, vt, da_lam, subln_g, lam_init, ctx_only):
    def in_blocks(tq, n_keys, q0, k0):
        return [
            pl.BlockSpec((1, LANES, tq), lambda h, i: (h, 0, i + q0)),
            pl.BlockSpec((n_keys, LANES), lambda h, i: (k0, h)),
            pl.BlockSpec((1, LANES, n_keys), lambda h, i: (h, 0, k0)),
            pl.BlockSpec((4, HEAD_DIM), lambda h, i: (0, 0)),
            pl.BlockSpec((LANES, 1), lambda h, i: (0, 0)),
        ]
    return _attn_call(functools.partial(_da_kernel, lam_init), ctx_only, DA_HEADS, LANES,
                      in_blocks, (qt, k, vt, da_lam, subln_g), DA_HEADS * LANES, "da_attention")


def _gqa_kernel(ctx_only, qt_ref, k_ref, vt_ref, o_ref, *scratch):
    g = pl.program_id(0)
    qt = qt_ref[0]
    zero = jnp.zeros((HEAD_DIM, qt.shape[1]), qt.dtype)
    weights = []
    for j in range(2):
        qj = qt[j * HEAD_DIM:(j + 1) * HEAD_DIM]
        weights.append(jnp.where(g == 0, jnp.concatenate([qj, zero], axis=0),
                                 jnp.concatenate([zero, qj], axis=0)))
    o0, o1 = _attend(ctx_only, k_ref, lambda s, n: vt_ref[:, pl.ds(s, n)], weights, (g, g),
                     scratch)
    o_ref[...] = jnp.concatenate([o0, o1], axis=0).T.astype(o_ref.dtype)


def _gqa_attention(qt, k, vt, ctx_only):
    def in_blocks(tq, n_keys, q0, k0):
        return [
            pl.BlockSpec((1, LANES, tq), lambda g, i: (g, 0, i + q0)),
            pl.BlockSpec((n_keys, LANES), lambda g, i: (k0, 0)),
            pl.BlockSpec((HEAD_DIM, n_keys), lambda g, i: (g, k0)),
        ]
    return _attn_call(_gqa_kernel, ctx_only, GQA_KV_HEADS, HEAD_DIM, in_blocks, (qt, k, vt),
                      GQA_KV_HEADS * LANES, "gqa_attention")


def _gelu_tanh(x):
    return 0.5 * x * (1.0 + jnp.tanh(math.sqrt(2.0 / math.pi) * (x + 0.044715 * (x * x * x))))


def _outproj_kernel(h_ref, mod_ref, a_ref, g_ref, hs_ref, gb_ref, w_ref, o_ref):
    mods = mod_ref[0]
    r = ((hs_ref[0] + hs_ref[1]) * _gelu_tanh(gb_ref[...])).astype(BF16)
    n_a = DA_HEADS * LANES
    n_g = n_a + GQA_KV_HEADS * LANES
    y = (jnp.dot(a_ref[...], w_ref[0, 0:n_a, :], preferred_element_type=F32)
         + jnp.dot(g_ref[...], w_ref[0, n_a:n_g, :], preferred_element_type=F32)
         + jnp.dot(r, w_ref[0, n_g:, :], preferred_element_type=F32))
    o_ref[...] = h_ref[...] + mods[5:6] * y


def _outproj(h, mods, a, g, hs, gb, w_out, layer, n_tiles):
    row_tile = lambda width: pl.BlockSpec((TM, width), lambda i: (i, 0))
    return pl.pallas_call(
        _outproj_kernel,
        grid=(n_tiles,),
        in_specs=[
            row_tile(D_MODEL),
            pl.BlockSpec((1, N_MOD, D_MODEL), lambda i: (i // N_LAT_TILES, 0, 0)),
            row_tile(DA_HEADS * LANES),
            row_tile(GQA_KV_HEADS * LANES),
            pl.BlockSpec((2, TM, LRU_WIDTH), lambda i: (0, i, 0)),
            row_tile(LRU_WIDTH),
            _layer_resident((D_MODEL, D_MODEL), layer),
        ],
        out_specs=row_tile(D_MODEL),
        out_shape=jax.ShapeDtypeStruct((n_tiles * TM, D_MODEL), F32),
        compiler_params=_params(1),
        name="outproj",
    )(h, mods, a, g, hs, gb, w_out)


def _final_norm_kernel(h_ref, g_ref, o_ref):
    x = h_ref[...]
    ms = jnp.mean(x * x, axis=-1, keepdims=True)
    o_ref[...] = (x * lax.rsqrt(ms + EPS)) * g_ref[...]


def _final_norm(h, g):
    return pl.pallas_call(
        _final_norm_kernel,
        grid=(N_LAT_TILES,),
        in_specs=[pl.BlockSpec((TM, D_MODEL), lambda i: (i, 0)),
                  pl.BlockSpec((1, D_MODEL), lambda i: (0, 0))],
        out_specs=pl.BlockSpec((TM, D_MODEL), lambda i: (i, 0)),
        out_shape=jax.ShapeDtypeStruct((SEQ, D_MODEL), F32),
        compiler_params=_params(1),
        name="final_norm",
    )(h, g)


def _rope_tables():
    half = HEAD_DIM // 4
    t = jnp.arange(SEQ, dtype=jnp.int32)
    row = (t // GRID_W).astype(F32)
    col = (t % GRID_W).astype(F32)
    inv = ROPE_THETA ** (-jnp.arange(half, dtype=F32) / half)
    ang_r = row[:, None] * inv
    ang_c = col[:, None] * inv
    cos64 = jnp.concatenate([jnp.cos(ang_r)] * 2 + [jnp.cos(ang_c)] * 2, axis=-1)
    sin64 = jnp.concatenate([-jnp.sin(ang_r), jnp.sin(ang_r), -jnp.sin(ang_c), jnp.sin(ang_c)],
                            axis=-1)
    cos_t = jnp.concatenate([jnp.tile(cos64, (1, 2)), jnp.ones((CTX_LEN, LANES), F32)], axis=0)
    sin_t = jnp.concatenate([jnp.tile(sin64, (1, 2)), jnp.zeros((CTX_LEN, LANES), F32)], axis=0)
    return cos_t, sin_t


def _block_diag(w):
    eye = jnp.eye(LRU_BLOCKS, dtype=w.dtype)
    full = jnp.einsum('dkij,kl->dkilj', w, eye)
    return full.reshape(2, LRU_WIDTH, LRU_WIDTH).astype(BF16)


def kernel(x, c, ctx, c_ctx, ada_w, ada_b, norm_g, ffn1_w13, ffn1_w2, ffn2_w13, ffn2_w2,
           w_in, w_out, da_lam, da_subln_g, qk_norm_g, lru_conv_w, lru_conv_b,
           lru_wa, lru_ba, lru_wi, lru_bi, lru_lambda, final_g):
    assert x.shape == (1, SEQ, D_MODEL) and ctx.shape == (1, CTX_LEN, D_MODEL)
    cc = jnp.zeros((8, D_MODEL), F32).at[0].set(c[0]).at[1].set(c_ctx)
    mods_all = _ada_mods(cc, ada_w, ada_b).reshape(DEPTH, 8, N_MOD, D_MODEL)[:, :2]
    cos_t, sin_t = _rope_tables()
    ffn1_w13, ffn1_w2, ffn2_w13, ffn2_w2, w_in, w_out = (
        w.astype(BF16) for w in (ffn1_w13, ffn1_w2, ffn2_w13, ffn2_w2, w_in, w_out))

    h = jnp.concatenate([x[0], ctx[0]], axis=0)
    for l in range(DEPTH):
        last = l == DEPTH - 1
        lam_init = 0.8 - 0.6 * math.exp(-0.3 * l)
        mods = mods_all[l]

        h = _ffn(h, mods, norm_g[l, 0:1], ffn1_w13, ffn1_w2, l, 0, N_TILES)

        qk_gain = jnp.tile(qk_norm_g[l], (1, 2))
        qt_da, k_da, vt_da, qt_g, k_g, vt_g, xb, gb = _inproj(
            h, mods, norm_g[l, 1:2], w_in, l, cos_t, sin_t, qk_gain)

        hs = _lru(xb, lru_conv_w[l], lru_conv_b[l][None, :], _block_diag(lru_wa[l]),
                  _block_diag(lru_wi[l]), lru_ba[l][:, None, :], lru_bi[l][:, None, :],
                  lru_lambda[l][:, None, :])

        n_tiles = N_LAT_TILES if last else N_TILES
        da_args = (qt_da, k_da, vt_da, da_lam[l], da_subln_g[l][:, None], lam_init)
        a = _da_attention(*da_args, False)
        g = _gqa_attention(qt_g, k_g, vt_g, False)
        if not last:
            a = jnp.concatenate([a, _da_attention(*da_args, True)], axis=0)
            g = jnp.concatenate([g, _gqa_attention(qt_g, k_g, vt_g, True)], axis=0)

        h = _outproj(h, mods, a, g, hs, gb, w_out, l, n_tiles)
        h = _ffn(h, mods, norm_g[l, 2:3], ffn2_w13, ffn2_w2, l, 6, n_tiles)
    return _final_norm(h, final_g[None, :])[None]
```

```python
import functools
import math

import jax
import jax.numpy as jnp
from jax import lax
from jax.experimental import pallas as pl
from jax.experimental.pallas import tpu as pltpu

D_MODEL = 1024
SEQ = 16384
CTX_LEN = 256
T_ALL = SEQ + CTX_LEN
DEPTH = 4
GRID_W = 64
HEAD_DIM = 64
SCALE = HEAD_DIM ** -0.5
LOG2E = math.log2(math.e)
ROPE_THETA = 10000.0
DA_HEADS = 4
GQA_KV_HEADS = 2
LRU_WIDTH = 256
LRU_BLOCKS = 4
LRU_BLOCK = LRU_WIDTH // LRU_BLOCKS
LRU_C = 8.0
IN_WIDTH = 2560
D_FF = 2816
N_MOD = 9
EPS = 1e-6

LANES = 128
MXU_DIM = 256

TM = 256
N_TILES = T_ALL // TM
N_LAT_TILES = SEQ // TM
FF_CHUNK = MXU_DIM
N_FF_CHUNKS = D_FF // FF_CHUNK
TQ = 1024
TK = 640
N_K_CHUNKS = T_ALL // TK
TK_FAST = 1280
N_K_FAST = T_ALL // TK_FAST
NEG_BIG = -1e30
SHIFT_GUARD = 32.0
BOUND_SLACK = 1.02
L_MIN = 2.0 ** -60
L_MAX = 2.0 ** 100
VMEM_LIMIT = 48 * 1024 * 1024

F32 = jnp.float32
BF16 = jnp.bfloat16


def _params(n_axes):
    return pltpu.CompilerParams(dimension_semantics=("arbitrary",) * n_axes,
                                vmem_limit_bytes=VMEM_LIMIT)


def _resident(shape):
    zeros = (0,) * len(shape)
    return pl.BlockSpec(shape, lambda *_: zeros, pipeline_mode=pl.Buffered(1))


def _sigmoid(x):
    return 1.0 / (1.0 + jnp.exp(-x))


def _norm_mod(x, g, shift, scale):
    ms = jnp.mean(x * x, axis=-1, keepdims=True)
    y = (x * lax.rsqrt(ms + EPS)) * g
    return y * (1.0 + scale) + shift


ADA_COLS = 1152


def _ada_kernel(c_ref, w_ref, b_ref, o_ref):
    c = c_ref[...]
    s = (c * _sigmoid(c)).astype(BF16)
    o_ref[0] = jnp.dot(s, w_ref[0].astype(BF16), preferred_element_type=F32) + b_ref[0]


def _ada_mods(cc, ada_w, ada_b):
    width = N_MOD * D_MODEL
    return pl.pallas_call(
        _ada_kernel,
        grid=(DEPTH, width // ADA_COLS),
        in_specs=[
            pl.BlockSpec((8, D_MODEL), lambda l, j: (0, 0)),
            pl.BlockSpec((1, D_MODEL, ADA_COLS), lambda l, j: (l, 0, j)),
            pl.BlockSpec((1, 1, ADA_COLS), lambda l, j: (l, 0, j)),
        ],
        out_specs=pl.BlockSpec((1, 8, ADA_COLS), lambda l, j: (l, 0, j)),
        out_shape=jax.ShapeDtypeStruct((DEPTH, 8, width), F32),
        compiler_params=_params(2),
        name="ada_mods",
    )(cc, ada_w, ada_b.reshape(DEPTH, 1, width))


def _gelu_tanh(x):
    return 0.5 * x * (1.0 + jnp.tanh(math.sqrt(2.0 / math.pi) * (x + 0.044715 * (x * x * x))))


def _mixer_out(a_ref, g_ref, hs_ref, gb_ref, w_ref):
    r = ((hs_ref[0] + hs_ref[1]) * _gelu_tanh(gb_ref[...])).astype(BF16)
    n_a = DA_HEADS * LANES
    n_g = n_a + GQA_KV_HEADS * LANES
    return (jnp.dot(a_ref[...], w_ref[0, 0:n_a, :], preferred_element_type=F32)
            + jnp.dot(g_ref[...], w_ref[0, n_a:n_g, :], preferred_element_type=F32)
            + jnp.dot(r, w_ref[0, n_g:, :], preferred_element_type=F32))


def _ffn_kernel(i0, with_mixer, with_final_norm, h_ref, mod_ref, g_ref, w13_ref, w2_ref, *rest):
    rest = list(rest)
    mixer_refs = [rest.pop(0) for _ in range(5)] if with_mixer else None
    final_g_ref = rest.pop(0) if with_final_norm else None
    o_ref, acc_ref, gate_ref = rest
    x = h_ref[...]
    mods = mod_ref[0]
    if with_mixer:
        x = x + mods[5:6] * _mixer_out(*mixer_refs)
    z = _norm_mod(x, g_ref[...], mods[i0:i0 + 1], mods[i0 + 1:i0 + 2]).astype(BF16)
    acc_ref[...] = jnp.zeros_like(acc_ref)

    def gate(c):
        lo = c * FF_CHUNK
        a = jnp.dot(z, w13_ref[0, :, lo:lo + FF_CHUNK], preferred_element_type=F32)
        b = jnp.dot(z, w13_ref[0, :, D_FF + lo:D_FF + lo + FF_CHUNK], preferred_element_type=F32)
        return (a * _sigmoid(a) * b).astype(BF16)

    def down(c):
        acc_ref[...] += jnp.dot(gate_ref[...], w2_ref[0, c * FF_CHUNK:(c + 1) * FF_CHUNK, :],
                                preferred_element_type=F32)

    gate_ref[...] = gate(0)
    for c in range(1, N_FF_CHUNKS):
        down(c - 1)
        gate_ref[...] = gate(c)
    down(N_FF_CHUNKS - 1)
    y = x + (0.5 * mods[i0 + 2:i0 + 3]) * acc_ref[...]
    if with_final_norm:
        y = (y * lax.rsqrt(jnp.mean(y * y, axis=-1, keepdims=True) + EPS)) * final_g_ref[...]
    o_ref[...] = y


def _layer_resident(shape, layer):
    index = (layer,) + (0,) * len(shape)
    return pl.BlockSpec((1,) + shape, lambda *_: index, pipeline_mode=pl.Buffered(1))


def _ffn(h, mods, g, w13, w2, layer, i0, n_tiles, mixer=None, final_g=None):
    row_tile = lambda width: pl.BlockSpec((TM, width), lambda i: (i, 0))
    vector = pl.BlockSpec((1, D_MODEL), lambda i: (0, 0))
    in_specs = [
        row_tile(D_MODEL),
        pl.BlockSpec((1, N_MOD, D_MODEL), lambda i: (i // N_LAT_TILES, 0, 0)),
        vector,
        _layer_resident((D_MODEL, 2 * D_FF), layer),
        _layer_resident((D_FF, D_MODEL), layer),
    ]
    operands = [h, mods, g, w13, w2]
    if mixer is not None:
        in_specs += [row_tile(DA_HEADS * LANES), row_tile(GQA_KV_HEADS * LANES),
                     pl.BlockSpec((2, TM, LRU_WIDTH), lambda i: (0, i, 0)), row_tile(LRU_WIDTH),
                     _layer_resident((D_MODEL, D_MODEL), layer)]
        operands += list(mixer)
    if final_g is not None:
        in_specs.append(vector)
        operands.append(final_g)
    return pl.pallas_call(
        functools.partial(_ffn_kernel, i0, mixer is not None, final_g is not None),
        grid=(n_tiles,),
        in_specs=in_specs,
        out_specs=row_tile(D_MODEL),
        out_shape=jax.ShapeDtypeStruct((n_tiles * TM, D_MODEL), F32),
        scratch_shapes=[pltpu.VMEM((TM, D_MODEL), F32), pltpu.VMEM((TM, FF_CHUNK), BF16)],
        compiler_params=_params(1),
        name="ffn",
    )(*operands)


def _swap16(x):
    lane = lax.broadcasted_iota(jnp.int32, x.shape, 1)
    return jnp.where(lane % 32 < 16, pltpu.roll(x, LANES - 16, 1), pltpu.roll(x, 16, 1))


def _rope(x, cos, sin):
    return x * cos + _swap16(x) * sin


def _head_mean_sq(x):
    sq = x * x
    hi = sq.astype(BF16)
    lo = (sq - hi.astype(F32)).astype(BF16)
    r = lax.broadcasted_iota(jnp.int32, (LANES, LANES), 0) // HEAD_DIM
    c = lax.broadcasted_iota(jnp.int32, (LANES, LANES), 1) // HEAD_DIM
    ones = jnp.where(r == c, 1.0, 0.0).astype(BF16)
    tot = (jnp.dot(hi, ones, preferred_element_type=F32)
           + jnp.dot(lo, ones, preferred_element_type=F32))
    return tot * (1.0 / HEAD_DIM)


def _inproj_kernel(h_ref, mod_ref, g_ref, w_ref, cos_ref, sin_ref, qkg_ref,
                   qt_da_ref, k_da_ref, vt_da_ref, qt_g_ref, k_g_ref, vt_g_ref, xb_ref, gb_ref):
    mods = mod_ref[0]
    z = _norm_mod(h_ref[...], g_ref[...], mods[3:4], mods[4:5]).astype(BF16)
    proj = jnp.dot(z, w_ref[0], preferred_element_type=F32)
    cos = cos_ref[...]
    sin = sin_ref[...]
    q_scale = SCALE * LOG2E

    def slab(j):
        return proj[:, j * LANES:(j + 1) * LANES]

    for h in range(DA_HEADS):
        q = _rope(slab(h), cos, sin) * q_scale
        qt_da_ref[h] = q.T.astype(BF16)
        k_da_ref[:, h * LANES:(h + 1) * LANES] = _rope(slab(4 + h), cos, sin).astype(BF16)
        vt_da_ref[h] = slab(8 + h).T.astype(BF16)

    gq_gain = qkg_ref[0:1, :]
    gk_gain = qkg_ref[1:2, :]
    for g in range(GQA_KV_HEADS):
        q = slab(12 + g)
        q = (q * lax.rsqrt(_head_mean_sq(q) + EPS)) * gq_gain
        qt_g_ref[g] = (_rope(q, cos, sin) * q_scale).T.astype(BF16)
    k = slab(14)
    k = (k * lax.rsqrt(_head_mean_sq(k) + EPS)) * gk_gain
    k_g_ref[...] = _rope(k, cos, sin).astype(BF16)
    vt_g_ref[...] = slab(15).T.astype(BF16)
    xb_ref[...] = proj[:, 2048:2304]
    gb_ref[...] = proj[:, 2304:2560]


def _inproj(h, mods, g, w_in, layer, cos_t, sin_t, qk_gain):
    row_tile = lambda width: pl.BlockSpec((TM, width), lambda i: (i, 0))
    return pl.pallas_call(
        _inproj_kernel,
        grid=(N_TILES,),
        in_specs=[
            row_tile(D_MODEL),
            pl.BlockSpec((1, N_MOD, D_MODEL), lambda i: (i // N_LAT_TILES, 0, 0)),
            pl.BlockSpec((1, D_MODEL), lambda i: (0, 0)),
            _layer_resident((D_MODEL, IN_WIDTH), layer),
            row_tile(LANES),
            row_tile(LANES),
            pl.BlockSpec((2, LANES), lambda i: (0, 0)),
        ],
        out_specs=[
            pl.BlockSpec((DA_HEADS, LANES, TM), lambda i: (0, 0, i)),
            row_tile(DA_HEADS * LANES),
            pl.BlockSpec((DA_HEADS, LANES, TM), lambda i: (0, 0, i)),
            pl.BlockSpec((GQA_KV_HEADS, LANES, TM), lambda i: (0, 0, i)),
            row_tile(LANES),
            pl.BlockSpec((LANES, TM), lambda i: (0, i)),
            row_tile(LRU_WIDTH),
            row_tile(LRU_WIDTH),
        ],
        out_shape=[
            jax.ShapeDtypeStruct((DA_HEADS, LANES, T_ALL), BF16),
            jax.ShapeDtypeStruct((T_ALL, DA_HEADS * LANES), BF16),
            jax.ShapeDtypeStruct((DA_HEADS, LANES, T_ALL), BF16),
            jax.ShapeDtypeStruct((GQA_KV_HEADS, LANES, T_ALL), BF16),
            jax.ShapeDtypeStruct((T_ALL, LANES), BF16),
            jax.ShapeDtypeStruct((LANES, T_ALL), BF16),
            jax.ShapeDtypeStruct((T_ALL, LRU_WIDTH), F32),
            jax.ShapeDtypeStruct((T_ALL, LRU_WIDTH), F32),
        ],
        compiler_params=_params(1),
        name="inproj",
    )(h, mods, g, w_in, cos_t, sin_t, qk_gain)


def _lru_tile_index(d, j):
    lat = jnp.where(d == 0, j - 1, N_LAT_TILES - j)
    return jnp.where(j == 0, N_LAT_TILES, lat)


def _lru_kernel(x_ref, cw_ref, cb_ref, wa_ref, wi_ref, ba_ref, bi_ref, lam_ref, o_ref, carry_ref):
    d = pl.program_id(0)
    j = pl.program_id(1)
    ti = _lru_tile_index(d, j)
    t0 = pl.multiple_of(ti * TM, TM)

    @pl.when(j == 0)
    def _():
        carry_ref[...] = jnp.zeros_like(carry_ref)

    x = x_ref[pl.ds(t0, TM), :]
    prev8 = x_ref[pl.ds(pl.multiple_of(jnp.maximum(t0 - 8, 0), 8), 8), :]
    next8 = x_ref[pl.ds(pl.multiple_of(jnp.minimum(t0 + TM, T_ALL - 8), 8), 8), :]
    has_prev = jnp.logical_and(ti != 0, ti != N_LAT_TILES)
    has_next = jnp.logical_and(ti != N_LAT_TILES - 1, ti != N_LAT_TILES)
    prev_row = jnp.where(has_prev, prev8[7:8], 0.0)
    next_row0 = jnp.where(has_next, next8[0:1], 0.0)
    next_row1 = jnp.where(has_next, next8[1:2], 0.0)
    row = lax.broadcasted_iota(jnp.int32, (TM, LRU_WIDTH), 0)
    xm1 = jnp.where(row == 0, prev_row, pltpu.roll(x, 1, 0))
    xp1 = jnp.where(row == TM - 1, next_row0, pltpu.roll(x, TM - 1, 0))
    xp2 = jnp.where(row == TM - 2, next_row0,
                    jnp.where(row == TM - 1, next_row1, pltpu.roll(x, TM - 2, 0)))
    cw = cw_ref[...]
    xc = xm1 * cw[0:1] + x * cw[1:2] + xp1 * cw[2:3] + xp2 * cw[3:4] + cb_ref[...]

    xcb = xc.astype(BF16)
    r = _sigmoid(jnp.dot(xcb, wa_ref[0], preferred_element_type=F32) + ba_ref[0])
    gi = _sigmoid(jnp.dot(xcb, wi_ref[0], preferred_element_type=F32) + bi_ref[0])
    neg_lam = -lam_ref[0]
    softplus = jnp.maximum(neg_lam, 0.0) + jnp.log1p(jnp.exp(-jnp.abs(neg_lam)))
    log_a = (-LRU_C) * r * softplus
    a = jnp.exp(log_a)
    u = jnp.sqrt(-jnp.tanh(log_a) * (a * a + 1.0)) * (gi * xc)

    def scan(A, B, forward):
        k = 1
        while k < TM:
            if forward:
                keep = row >= k
                As = jnp.where(keep, pltpu.roll(A, k, 0), 1.0)
                Bs = jnp.where(keep, pltpu.roll(B, k, 0), 0.0)
            else:
                keep = row < TM - k
                As = jnp.where(keep, pltpu.roll(A, TM - k, 0), 1.0)
                Bs = jnp.where(keep, pltpu.roll(B, TM - k, 0), 0.0)
            B = A * Bs + B
            A = A * As
            k *= 2
        return A, B

    carry = carry_ref[0:1, :]

    @pl.when(d == 0)
    def _():
        A, B = scan(a, u, True)
        hs = B + A * carry
        o_ref[0] = hs
        carry_ref[0:1, :] = hs[TM - 1:TM]

    @pl.when(d == 1)
    def _():
        A, B = scan(a, u, False)
        hs = B + A * carry
        o_ref[0] = hs
        carry_ref[0:1, :] = hs[0:1]


def _lru(xb, conv_w, conv_b, wa, wi, ba, bi, lam):
    per_dir = lambda *tail: pl.BlockSpec((1,) + tail, lambda d, j: (d,) + (0,) * len(tail))
    return pl.pallas_call(
        _lru_kernel,
        grid=(2, N_TILES),
        in_specs=[
            _resident((T_ALL, LRU_WIDTH)),
            pl.BlockSpec((4, LRU_WIDTH), lambda d, j: (0, 0)),
            pl.BlockSpec((1, LRU_WIDTH), lambda d, j: (0, 0)),
            per_dir(LRU_WIDTH, LRU_WIDTH),
            per_dir(LRU_WIDTH, LRU_WIDTH),
            per_dir(1, LRU_WIDTH),
            per_dir(1, LRU_WIDTH),
            per_dir(1, LRU_WIDTH),
        ],
        out_specs=pl.BlockSpec((1, TM, LRU_WIDTH), lambda d, j: (d, _lru_tile_index(d, j), 0)),
        out_shape=jax.ShapeDtypeStruct((2, T_ALL, LRU_WIDTH), F32),
        scratch_shapes=[pltpu.VMEM((8, LRU_WIDTH), F32)],
        compiler_params=_params(2),
        name="rglru",
    )(xb, conv_w, conv_b, wa, wi, ba, bi, lam)


def _softmax_chunk(s, m_old, l_old):
    m_new = jnp.maximum(m_old, jnp.max(s, axis=0, keepdims=True))
    alpha = jnp.exp2(m_old - m_new)
    p = jnp.exp2(s - m_new)
    l_new = alpha * l_old + jnp.sum(p, axis=0, keepdims=True)
    return p.astype(BF16), m_new, l_new, alpha


def _attend_ctx(kc, vc, weights):
    outs = []
    for w in weights:
        s = jnp.dot(kc, w, preferred_element_type=F32)
        p, _, l, _ = _softmax_chunk(s, jnp.full((1, CTX_LEN), NEG_BIG, F32),
                                    jnp.zeros((1, CTX_LEN), F32))
        outs.append(jnp.dot(vc, p, preferred_element_type=F32) * (1.0 / l))
    return outs


def _key_norm_bound(k_ref, kmax_ref):
    r = lax.broadcasted_iota(jnp.int32, (LANES, LANES), 0) // HEAD_DIM
    c = lax.broadcasted_iota(jnp.int32, (LANES, LANES), 1) // HEAD_DIM
    ones = jnp.where(r == c, 1.0, 0.0).astype(BF16)

    def chunk(i, best):
        kf = k_ref[pl.ds(pl.multiple_of(i * TK, TK), TK), :].astype(F32)
        n2 = jnp.dot((kf * kf).astype(BF16), ones, preferred_element_type=F32)
        return jnp.maximum(best, jnp.max(n2, axis=0, keepdims=True))

    kmax_ref[0:1, :] = lax.fori_loop(0, N_K_CHUNKS, chunk, jnp.zeros((1, LANES), F32))


def _attend_lat_fast(k_ref, v_chunk, weights, shift, acc_ref, p_refs):
    maps = range(len(weights))

    def scores_exp(c, p_ref, l):
        kc = k_ref[pl.ds(pl.multiple_of(c * TK_FAST, TK_FAST), TK_FAST), :]
        l_new = []
        for j in maps:
            p = jnp.exp2(jnp.dot(kc, weights[j], preferred_element_type=F32) - shift[j])
            l_new.append(l[j] + jnp.sum(p, axis=0, keepdims=True))
            p_ref[j] = p.astype(BF16)
        return tuple(l_new)

    def values(c, p_ref):
        vc = v_chunk(pl.multiple_of(c * TK_FAST, TK_FAST), TK_FAST)
        for j in maps:
            acc_ref[j] += jnp.dot(vc, p_ref[j], preferred_element_type=F32)

    acc_ref[...] = jnp.zeros_like(acc_ref)
    l = scores_exp(0, p_refs[0], tuple(jnp.zeros((1, TQ), F32) for _ in maps))

    def chunk_pair(i, l):
        c0 = 2 * i
        l = scores_exp(c0 + 1, p_refs[1], l)
        values(c0, p_refs[0])
        l = scores_exp(c0 + 2, p_refs[0], l)
        values(c0 + 1, p_refs[1])
        return l

    n_pairs = (N_K_FAST - 1) // 2
    l = lax.fori_loop(0, n_pairs, chunk_pair, l)
    if N_K_FAST % 2 == 0:
        l = scores_exp(N_K_FAST - 1, p_refs[1], l)
        values(N_K_FAST - 2, p_refs[0])
        values(N_K_FAST - 1, p_refs[1])
    else:
        values(N_K_FAST - 1, p_refs[0])
    return l


def _attend_lat_exact(k_ref, v_chunk, weights, acc_ref, s_refs, p_refs):
    maps = range(len(weights))

    def scores(c, s_ref):
        kc = k_ref[pl.ds(pl.multiple_of(c * TK, TK), TK), :]
        for j in maps:
            s_ref[j] = jnp.dot(kc, weights[j], preferred_element_type=F32)

    def softmax(s_ref, p_ref, m, l):
        m_new, l_new, alpha = [], [], []
        for j in maps:
            m_j = jnp.maximum(m[j], jnp.max(s_ref[j], axis=0, keepdims=True))
            a_j = jnp.exp2(m[j] - m_j)
            p = jnp.exp2(s_ref[j] - m_j)
            l_new.append(a_j * l[j] + jnp.sum(p, axis=0, keepdims=True))
            p_ref[j, 0:TK, :] = p.astype(BF16)
            m_new.append(m_j)
            alpha.append(a_j)
        return m_new, l_new, alpha

    def values(c, p_ref, alpha):
        vc = v_chunk(pl.multiple_of(c * TK, TK), TK)
        for j in maps:
            acc_ref[j] = alpha[j] * acc_ref[j] + jnp.dot(vc, p_ref[j, 0:TK, :],
                                                         preferred_element_type=F32)

    acc_ref[...] = jnp.zeros_like(acc_ref)
    p_refs[1][...] = jnp.zeros_like(p_refs[1])
    scores(0, s_refs[0])
    stat = lambda v: [jnp.full((1, TQ), v, F32) for _ in maps]

    def chunk_pair(i, carry):
        m, l, alpha_b = carry
        c0 = 2 * i
        scores(c0 + 1, s_refs[1])
        m, l, alpha_a = softmax(s_refs[0], p_refs[0], m, l)
        values(jnp.maximum(c0 - 1, 0), p_refs[1], alpha_b)
        scores(jnp.minimum(c0 + 2, N_K_CHUNKS - 1), s_refs[0])
        m, l, alpha_b = softmax(s_refs[1], p_refs[1], m, l)
        values(c0, p_refs[0], alpha_a)
        return m, l, alpha_b

    _, l, alpha_b = lax.fori_loop(0, N_K_CHUNKS // 2, chunk_pair,
                                  (stat(NEG_BIG), stat(0.0), stat(1.0)))
    values(N_K_CHUNKS - 1, p_refs[1], alpha_b)
    return l


def _attend(ctx_only, k_ref, vt_at, weights, halves, scratch):
    if ctx_only:
        return _attend_ctx(k_ref[...], vt_at(0, CTX_LEN), weights)
    acc_ref, l_ref, kmax_ref, s0_ref, s1_ref, p0_ref, p1_ref = scratch
    maps = range(len(weights))

    @pl.when(pl.program_id(1) == 0)
    def _():
        _key_norm_bound(k_ref, kmax_ref)

    lane = lax.broadcasted_iota(jnp.int32, (1, LANES), 1)
    shift = []
    for j in maps:
        k2 = jnp.max(jnp.where(lane // HEAD_DIM == halves[j], kmax_ref[0:1, :], 0.0),
                     axis=-1, keepdims=True)
        wf = weights[j].astype(F32)
        q2 = jnp.sum(wf * wf, axis=0, keepdims=True)
        shift.append(jnp.sqrt(q2 * k2) * BOUND_SLACK - SHIFT_GUARD)

    l = _attend_lat_fast(k_ref, vt_at, weights, shift, acc_ref, (p0_ref, p1_ref))
    n_bad = jnp.zeros((), F32)
    for j in maps:
        l_ref[j:j + 1, :] = l[j]
        ok = jnp.logical_and(l[j] >= L_MIN, l[j] <= L_MAX)
        n_bad = n_bad + jnp.sum(jnp.where(ok, 0.0, 1.0))

    @pl.when(n_bad > 0.0)
    def _():
        l = _attend_lat_exact(k_ref, vt_at, weights, acc_ref, (s0_ref, s1_ref), (p0_ref, p1_ref))
        for j in maps:
            l_ref[j:j + 1, :] = l[j]

    return [acc_ref[j] * (1.0 / l_ref[j:j + 1, :]) for j in maps]


def _attn_call(kernel, ctx_only, n_heads, dv, in_blocks, operands, out_width, name):
    if ctx_only:
        tq, n_q, n_keys, q0, k0 = CTX_LEN, 1, CTX_LEN, SEQ // CTX_LEN, SEQ // CTX_LEN
        scratch = []
    else:
        tq, n_q, n_keys, q0, k0 = TQ, SEQ // TQ, T_ALL, 0, 0
        scratch = [pltpu.VMEM((2, dv, TQ), F32), pltpu.VMEM((8, TQ), F32),
                   pltpu.VMEM((8, LANES), F32),
                   pltpu.VMEM((2, TK, TQ), F32), pltpu.VMEM((2, TK, TQ), F32),
                   pltpu.VMEM((2, TK_FAST, TQ), BF16), pltpu.VMEM((2, TK_FAST, TQ), BF16)]
    return pl.pallas_call(
        functools.partial(kernel, ctx_only),
        grid=(n_heads, n_q),
        in_specs=in_blocks(tq, n_keys, q0, k0),
        out_specs=pl.BlockSpec((tq, LANES), lambda h, i: (i, h)),
        out_shape=jax.ShapeDtypeStruct((n_q * tq, out_width), BF16),
        scratch_shapes=scratch,
        compiler_params=_params(2),
        name=name + ("_ctx" if ctx_only else ""),
    )(*operands)


def _da_kernel(lam_init, ctx_only, qt_ref, k_ref, vt_ref, lam_ref, g_ref, o_ref, *scratch):
    qt = qt_ref[0]
    row = lax.broadcasted_iota(jnp.int32, qt.shape, 0)
    zero = jnp.zeros_like(qt)
    weights = [jnp.where(row < HEAD_DIM, qt, zero), jnp.where(row >= HEAD_DIM, qt, zero)]
    o1, o2 = _attend(ctx_only, k_ref, lambda s, n: vt_ref[0, :, pl.ds(s, n)], weights, (0, 1),
                     scratch)
    lf = lam_ref[...]
    lam = (jnp.exp(jnp.sum(lf[0:1] * lf[1:2], axis=-1, keepdims=True))
           - jnp.exp(jnp.sum(lf[2:3] * lf[3:4], axis=-1, keepdims=True)) + lam_init)
    o = o1 - lam * o2
    ms = jnp.mean(o * o, axis=0, keepdims=True)
    on = (o * lax.rsqrt(ms + EPS)) * g_ref[...] * (1.0 - lam_init)
    o_ref[...] = on.T.astype(o_ref.dtype)


def _da_attention(qt, k, vt, da_lam, subln_g, lam_init, ctx_only):
    def in_blocks(tq, n_keys, q0, k0):
        return [
            pl.BlockSpec((1, LANES, tq), lambda h, i: (h, 0, i + q0)),
            pl.BlockSpec((n_keys, LANES), lambda h, i: (k0, h)),
            pl.BlockSpec((1, LANES, n_keys), lambda h, i: (h, 0, k0)),
            pl.BlockSpec((4, HEAD_DIM), lambda h, i: (0, 0)),
            pl.BlockSpec((LANES, 1), lambda h, i: (0, 0)),
        ]
    return _attn_call(functools.partial(_da_kernel, lam_init), ctx_only, DA_HEADS, LANES,
                      in_blocks, (qt, k, vt, da_lam, subln_g), DA_HEADS * LANES, "da_attention")


def _gqa_kernel(ctx_only, qt_ref, k_ref, vt_ref, o_ref, *scratch):
    g = pl.program_id(0)
    qt = qt_ref[0]
    zero = jnp.zeros((HEAD_DIM, qt.shape[1]), qt.dtype)
    weights = []
    for j in range(2):
        qj = qt[j * HEAD_DIM:(j + 1) * HEAD_DIM]
        weights.append(jnp.where(g == 0, jnp.concatenate([qj, zero], axis=0),
                                 jnp.concatenate([zero, qj], axis=0)))
    o0, o1 = _attend(ctx_only, k_ref, lambda s, n: vt_ref[:, pl.ds(s, n)], weights, (g, g),
                     scratch)
    o_ref[...] = jnp.concatenate([o0, o1], axis=0).T.astype(o_ref.dtype)


def _gqa_attention(qt, k, vt, ctx_only):
    def in_blocks(tq, n_keys, q0, k0):
        return [
            pl.BlockSpec((1, LANES, tq), lambda g, i: (g, 0, i + q0)),
            pl.BlockSpec((n_keys, LANES), lambda g, i: (k0, 0)),
            pl.BlockSpec((HEAD_DIM, n_keys), lambda g, i: (g, k0)),
        ]
    return _attn_call(_gqa_kernel, ctx_only, GQA_KV_HEADS, HEAD_DIM, in_blocks, (qt, k, vt),
                      GQA_KV_HEADS * LANES, "gqa_attention")


def _rope_tables():
    half = HEAD_DIM // 4
    t = jnp.arange(SEQ, dtype=jnp.int32)
    row = (t // GRID_W).astype(F32)
    col = (t % GRID_W).astype(F32)
    inv = ROPE_THETA ** (-jnp.arange(half, dtype=F32) / half)
    ang_r = row[:, None] * inv
    ang_c = col[:, None] * inv
    cos64 = jnp.concatenate([jnp.cos(ang_r)] * 2 + [jnp.cos(ang_c)] * 2, axis=-1)
    sin64 = jnp.concatenate([-jnp.sin(ang_r), jnp.sin(ang_r), -jnp.sin(ang_c), jnp.sin(ang_c)],
                            axis=-1)
    cos_t = jnp.concatenate([jnp.tile(cos64, (1, 2)), jnp.ones((CTX_LEN, LANES), F32)], axis=0)
    sin_t = jnp.concatenate([jnp.tile(sin64, (1, 2)), jnp.zeros((CTX_LEN, LANES), F32)], axis=0)
    return cos_t, sin_t


def _block_diag(w):
    eye = jnp.eye(LRU_BLOCKS, dtype=w.dtype)
    full = jnp.einsum('dkij,kl->dkilj', w, eye)
    return full.reshape(2, LRU_WIDTH, LRU_WIDTH).astype(BF16)


def kernel(x, c, ctx, c_ctx, ada_w, ada_b, norm_g, ffn1_w13, ffn1_w2, ffn2_w13, ffn2_w2,
           w_in, w_out, da_lam, da_subln_g, qk_norm_g, lru_conv_w, lru_conv_b,
           lru_wa, lru_ba, lru_wi, lru_bi, lru_lambda, final_g):
    assert x.shape == (1, SEQ, D_MODEL) and ctx.shape == (1, CTX_LEN, D_MODEL)
    cc = jnp.zeros((8, D_MODEL), F32).at[0].set(c[0]).at[1].set(c_ctx)
    mods_all = _ada_mods(cc, ada_w, ada_b).reshape(DEPTH, 8, N_MOD, D_MODEL)[:, :2]
    cos_t, sin_t = _rope_tables()
    ffn1_w13, ffn1_w2, ffn2_w13, ffn2_w2, w_in, w_out = (
        w.astype(BF16) for w in (ffn1_w13, ffn1_w2, ffn2_w13, ffn2_w2, w_in, w_out))

    h = jnp.concatenate([x[0], ctx[0]], axis=0)
    for l in range(DEPTH):
        last = l == DEPTH - 1
        lam_init = 0.8 - 0.6 * math.exp(-0.3 * l)
        mods = mods_all[l]

        h = _ffn(h, mods, norm_g[l, 0:1], ffn1_w13, ffn1_w2, l, 0, N_TILES)

        qk_gain = jnp.tile(qk_norm_g[l], (1, 2))
        qt_da, k_da, vt_da, qt_g, k_g, vt_g, xb, gb = _inproj(
            h, mods, norm_g[l, 1:2], w_in, l, cos_t, sin_t, qk_gain)

        hs = _lru(xb, lru_conv_w[l], lru_conv_b[l][None, :], _block_diag(lru_wa[l]),
                  _block_diag(lru_wi[l]), lru_ba[l][:, None, :], lru_bi[l][:, None, :],
                  lru_lambda[l][:, None, :])

        n_tiles = N_LAT_TILES if last else N_TILES
        da_args = (qt_da, k_da, vt_da, da_lam[l], da_subln_g[l][:, None], lam_init)
        a = _da_attention(*da_args, False)
        g = _gqa_attention(qt_g, k_g, vt_g, False)
        if not last:
            a = jnp.concatenate([a, _da_attention(*da_args, True)], axis=0)
            g = jnp.concatenate([g, _gqa_attention(qt_g, k_g, vt_g, True)], axis=0)

        h = _ffn(h, mods, norm_g[l, 2:3], ffn2_w13, ffn2_w2, l, 6, n_tiles,
                 mixer=(a, g, hs, gb, w_out), final_g=final_g[None, :] if last else None)
    return h[None]
```

```python
import functools
import math

import jax
import jax.numpy as jnp
from jax import lax
from jax.experimental import pallas as pl
from jax.experimental.pallas import tpu as pltpu

D_MODEL = 1024
SEQ = 16384
CTX_LEN = 256
T_ALL = SEQ + CTX_LEN
DEPTH = 4
GRID_W = 64
HEAD_DIM = 64
SCALE = HEAD_DIM ** -0.5
LOG2E = math.log2(math.e)
ROPE_THETA = 10000.0
DA_HEADS = 4
GQA_KV_HEADS = 2
LRU_WIDTH = 256
LRU_BLOCKS = 4
LRU_BLOCK = LRU_WIDTH // LRU_BLOCKS
LRU_C = 8.0
IN_WIDTH = 2560
D_FF = 2816
N_MOD = 9
EPS = 1e-6

LANES = 128
MXU_DIM = 256

TM = 512
T_PAD = SEQ + TM
N_TILES = T_PAD // TM
N_LAT_TILES = SEQ // TM
TL = 256
N_LRU_LAT = SEQ // TL
FF_CHUNK = MXU_DIM
N_FF_CHUNKS = D_FF // FF_CHUNK
TQ = 1024
TK = 640
N_K_CHUNKS = T_ALL // TK
TK_FAST = 1280
N_K_FAST = T_ALL // TK_FAST
NEG_BIG = -1e30
SHIFT_GUARD = 32.0
BOUND_SLACK = 1.02
L_MIN = 2.0 ** -60
L_MAX = 2.0 ** 100
VMEM_LIMIT = 48 * 1024 * 1024

F32 = jnp.float32
BF16 = jnp.bfloat16


def _params(n_axes):
    return pltpu.CompilerParams(dimension_semantics=("arbitrary",) * n_axes,
                                vmem_limit_bytes=VMEM_LIMIT)


def _resident(shape):
    zeros = (0,) * len(shape)
    return pl.BlockSpec(shape, lambda *_: zeros, pipeline_mode=pl.Buffered(1))


def _sigmoid(x):
    return 1.0 / (1.0 + jnp.exp(-x))


def _norm_mod(x, g, shift, scale):
    ms = jnp.mean(x * x, axis=-1, keepdims=True)
    y = (x * lax.rsqrt(ms + EPS)) * g
    return y * (1.0 + scale) + shift


ADA_COLS = 1152


def _ada_kernel(c_ref, w_ref, b_ref, o_ref):
    c = c_ref[...]
    s = (c * _sigmoid(c)).astype(BF16)
    o_ref[0] = jnp.dot(s, w_ref[0].astype(BF16), preferred_element_type=F32) + b_ref[0]


def _ada_mods(cc, ada_w, ada_b):
    width = N_MOD * D_MODEL
    return pl.pallas_call(
        _ada_kernel,
        grid=(DEPTH, width // ADA_COLS),
        in_specs=[
            pl.BlockSpec((8, D_MODEL), lambda l, j: (0, 0)),
            pl.BlockSpec((1, D_MODEL, ADA_COLS), lambda l, j: (l, 0, j)),
            pl.BlockSpec((1, 1, ADA_COLS), lambda l, j: (l, 0, j)),
        ],
        out_specs=pl.BlockSpec((1, 8, ADA_COLS), lambda l, j: (l, 0, j)),
        out_shape=jax.ShapeDtypeStruct((DEPTH, 8, width), F32),
        compiler_params=_params(2),
        name="ada_mods",
    )(cc, ada_w, ada_b.reshape(DEPTH, 1, width))


def _gelu_tanh(x):
    return 0.5 * x * (1.0 + jnp.tanh(math.sqrt(2.0 / math.pi) * (x + 0.044715 * (x * x * x))))


def _mixer_out(a_ref, g_ref, hs_ref, gb_ref, w_ref):
    r = ((hs_ref[0] + hs_ref[1]) * _gelu_tanh(gb_ref[...])).astype(BF16)
    n_a = DA_HEADS * LANES
    n_g = n_a + GQA_KV_HEADS * LANES
    return (jnp.dot(a_ref[...], w_ref[0, 0:n_a, :], preferred_element_type=F32)
            + jnp.dot(g_ref[...], w_ref[0, n_a:n_g, :], preferred_element_type=F32)
            + jnp.dot(r, w_ref[0, n_g:, :], preferred_element_type=F32))


def _ffn_kernel(i0, with_mixer, with_final_norm, h_ref, mod_ref, g_ref, w13_ref, w2_ref, *rest):
    rest = list(rest)
    mixer_refs = [rest.pop(0) for _ in range(5)] if with_mixer else None
    final_g_ref = rest.pop(0) if with_final_norm else None
    o_ref, acc_ref, gate_ref = rest
    x = h_ref[...]
    mods = mod_ref[0]
    if with_mixer:
        x = x + mods[5:6] * _mixer_out(*mixer_refs)
    z = _norm_mod(x, g_ref[...], mods[i0:i0 + 1], mods[i0 + 1:i0 + 2]).astype(BF16)
    acc_ref[...] = jnp.zeros_like(acc_ref)

    def gate(c):
        lo = c * FF_CHUNK
        a = jnp.dot(z, w13_ref[0, :, lo:lo + FF_CHUNK], preferred_element_type=F32)
        b = jnp.dot(z, w13_ref[0, :, D_FF + lo:D_FF + lo + FF_CHUNK], preferred_element_type=F32)
        return (a * _sigmoid(a) * b).astype(BF16)

    def down(c):
        acc_ref[...] += jnp.dot(gate_ref[...], w2_ref[0, c * FF_CHUNK:(c + 1) * FF_CHUNK, :],
                                preferred_element_type=F32)

    gate_ref[...] = gate(0)
    for c in range(1, N_FF_CHUNKS):
        down(c - 1)
        gate_ref[...] = gate(c)
    down(N_FF_CHUNKS - 1)
    y = x + (0.5 * mods[i0 + 2:i0 + 3]) * acc_ref[...]
    if with_final_norm:
        y = (y * lax.rsqrt(jnp.mean(y * y, axis=-1, keepdims=True) + EPS)) * final_g_ref[...]
    o_ref[...] = y


def _layer_resident(shape, layer):
    index = (layer,) + (0,) * len(shape)
    return pl.BlockSpec((1,) + shape, lambda *_: index, pipeline_mode=pl.Buffered(1))


def _ffn(h, mods, g, w13, w2, layer, i0, n_tiles, mixer=None, final_g=None):
    row_tile = lambda width: pl.BlockSpec((TM, width), lambda i: (i, 0))
    vector = pl.BlockSpec((1, D_MODEL), lambda i: (0, 0))
    in_specs = [
        row_tile(D_MODEL),
        pl.BlockSpec((1, N_MOD, D_MODEL), lambda i: (i // N_LAT_TILES, 0, 0)),
        vector,
        _layer_resident((D_MODEL, 2 * D_FF), layer),
        _layer_resident((D_FF, D_MODEL), layer),
    ]
    operands = [h, mods, g, w13, w2]
    if mixer is not None:
        in_specs += [row_tile(DA_HEADS * LANES), row_tile(GQA_KV_HEADS * LANES),
                     pl.BlockSpec((2, TM, LRU_WIDTH), lambda i: (0, i, 0)), row_tile(LRU_WIDTH),
                     _layer_resident((D_MODEL, D_MODEL), layer)]
        operands += list(mixer)
    if final_g is not None:
        in_specs.append(vector)
        operands.append(final_g)
    return pl.pallas_call(
        functools.partial(_ffn_kernel, i0, mixer is not None, final_g is not None),
        grid=(n_tiles,),
        in_specs=in_specs,
        out_specs=row_tile(D_MODEL),
        out_shape=jax.ShapeDtypeStruct((n_tiles * TM, D_MODEL), F32),
        scratch_shapes=[pltpu.VMEM((TM, D_MODEL), F32), pltpu.VMEM((TM, FF_CHUNK), BF16)],
        compiler_params=_params(1),
        name="ffn",
    )(*operands)


def _swap16(x):
    lane = lax.broadcasted_iota(jnp.int32, x.shape, 1)
    return jnp.where(lane % 32 < 16, pltpu.roll(x, LANES - 16, 1), pltpu.roll(x, 16, 1))


def _rope(x, cos, sin):
    return x * cos + _swap16(x) * sin


def _head_mean_sq(x):
    sq = x * x
    hi = sq.astype(BF16)
    lo = (sq - hi.astype(F32)).astype(BF16)
    r = lax.broadcasted_iota(jnp.int32, (LANES, LANES), 0) // HEAD_DIM
    c = lax.broadcasted_iota(jnp.int32, (LANES, LANES), 1) // HEAD_DIM
    ones = jnp.where(r == c, 1.0, 0.0).astype(BF16)
    tot = (jnp.dot(hi, ones, preferred_element_type=F32)
           + jnp.dot(lo, ones, preferred_element_type=F32))
    return tot * (1.0 / HEAD_DIM)


def _inproj_kernel(h_ref, mod_ref, g_ref, w_ref, cos_ref, sin_ref, qkg_ref,
                   qt_da_ref, k_da_ref, vt_da_ref, qt_g_ref, k_g_ref, vt_g_ref, xb_ref, gb_ref):
    mods = mod_ref[0]
    z = _norm_mod(h_ref[...], g_ref[...], mods[3:4], mods[4:5]).astype(BF16)
    proj = jnp.dot(z, w_ref[0], preferred_element_type=F32)
    cos = cos_ref[...]
    sin = sin_ref[...]
    q_scale = SCALE * LOG2E

    def slab(j):
        return proj[:, j * LANES:(j + 1) * LANES]

    for h in range(DA_HEADS):
        q = _rope(slab(h), cos, sin) * q_scale
        qt_da_ref[h] = q.T.astype(BF16)
        k_da_ref[:, h * LANES:(h + 1) * LANES] = _rope(slab(4 + h), cos, sin).astype(BF16)
        vt_da_ref[h] = slab(8 + h).T.astype(BF16)

    gq_gain = qkg_ref[0:1, :]
    gk_gain = qkg_ref[1:2, :]
    for g in range(GQA_KV_HEADS):
        q = slab(12 + g)
        q = (q * lax.rsqrt(_head_mean_sq(q) + EPS)) * gq_gain
        qt_g_ref[g] = (_rope(q, cos, sin) * q_scale).T.astype(BF16)
    k = slab(14)
    k = (k * lax.rsqrt(_head_mean_sq(k) + EPS)) * gk_gain
    k_g_ref[...] = _rope(k, cos, sin).astype(BF16)
    vt_g_ref[...] = slab(15).T.astype(BF16)
    xb_ref[...] = proj[:, 2048:2304]
    gb_ref[...] = proj[:, 2304:2560]


def _inproj(h, mods, g, w_in, layer, cos_t, sin_t, qk_gain):
    row_tile = lambda width: pl.BlockSpec((TM, width), lambda i: (i, 0))
    return pl.pallas_call(
        _inproj_kernel,
        grid=(N_TILES,),
        in_specs=[
            row_tile(D_MODEL),
            pl.BlockSpec((1, N_MOD, D_MODEL), lambda i: (i // N_LAT_TILES, 0, 0)),
            pl.BlockSpec((1, D_MODEL), lambda i: (0, 0)),
            _layer_resident((D_MODEL, IN_WIDTH), layer),
            row_tile(LANES),
            row_tile(LANES),
            pl.BlockSpec((2, LANES), lambda i: (0, 0)),
        ],
        out_specs=[
            pl.BlockSpec((DA_HEADS, LANES, TM), lambda i: (0, 0, i)),
            row_tile(DA_HEADS * LANES),
            pl.BlockSpec((DA_HEADS, LANES, TM), lambda i: (0, 0, i)),
            pl.BlockSpec((GQA_KV_HEADS, LANES, TM), lambda i: (0, 0, i)),
            row_tile(LANES),
            pl.BlockSpec((LANES, TM), lambda i: (0, i)),
            row_tile(LRU_WIDTH),
            row_tile(LRU_WIDTH),
        ],
        out_shape=[
            jax.ShapeDtypeStruct((DA_HEADS, LANES, T_PAD), BF16),
            jax.ShapeDtypeStruct((T_PAD, DA_HEADS * LANES), BF16),
            jax.ShapeDtypeStruct((DA_HEADS, LANES, T_PAD), BF16),
            jax.ShapeDtypeStruct((GQA_KV_HEADS, LANES, T_PAD), BF16),
            jax.ShapeDtypeStruct((T_PAD, LANES), BF16),
            jax.ShapeDtypeStruct((LANES, T_PAD), BF16),
            jax.ShapeDtypeStruct((T_PAD, LRU_WIDTH), F32),
            jax.ShapeDtypeStruct((T_PAD, LRU_WIDTH), F32),
        ],
        compiler_params=_params(1),
        name="inproj",
    )(h, mods, g, w_in, cos_t, sin_t, qk_gain)


N_LRU_STEPS = N_LRU_LAT + 2


def _lru_tile_index(d, j):
    lat = jnp.where(d == 0, j - 1, N_LRU_LAT - j)
    return jnp.where(j == 0, N_LRU_LAT, jnp.where(j == N_LRU_STEPS - 1, N_LRU_LAT + 1, lat))


def _lru_kernel(x_ref, cw_ref, cb_ref, wa_ref, wi_ref, ba_ref, bi_ref, lam_ref, o_ref, carry_ref):
    d = pl.program_id(0)
    j = pl.program_id(1)
    ti = _lru_tile_index(d, j)
    t0 = pl.multiple_of(ti * TL, TL)

    @pl.when(j == 0)
    def _():
        carry_ref[...] = jnp.zeros_like(carry_ref)

    x = x_ref[pl.ds(t0, TL), :]
    prev8 = x_ref[pl.ds(pl.multiple_of(jnp.maximum(t0 - 8, 0), 8), 8), :]
    next8 = x_ref[pl.ds(pl.multiple_of(jnp.minimum(t0 + TL, T_PAD - 8), 8), 8), :]
    has_prev = jnp.logical_and(ti != 0, ti < N_LRU_LAT)
    has_next = ti < N_LRU_LAT - 1
    prev_row = jnp.where(has_prev, prev8[7:8], 0.0)
    next_row0 = jnp.where(has_next, next8[0:1], 0.0)
    next_row1 = jnp.where(has_next, next8[1:2], 0.0)
    row = lax.broadcasted_iota(jnp.int32, (TL, LRU_WIDTH), 0)
    xm1 = jnp.where(row == 0, prev_row, pltpu.roll(x, 1, 0))
    xp1 = jnp.where(row == TL - 1, next_row0, pltpu.roll(x, TL - 1, 0))
    xp2 = jnp.where(row == TL - 2, next_row0,
                    jnp.where(row == TL - 1, next_row1, pltpu.roll(x, TL - 2, 0)))
    cw = cw_ref[...]
    xc = xm1 * cw[0:1] + x * cw[1:2] + xp1 * cw[2:3] + xp2 * cw[3:4] + cb_ref[...]

    xcb = xc.astype(BF16)
    r = _sigmoid(jnp.dot(xcb, wa_ref[0], preferred_element_type=F32) + ba_ref[0])
    gi = _sigmoid(jnp.dot(xcb, wi_ref[0], preferred_element_type=F32) + bi_ref[0])
    neg_lam = -lam_ref[0]
    softplus = jnp.maximum(neg_lam, 0.0) + jnp.log1p(jnp.exp(-jnp.abs(neg_lam)))
    log_a = (-LRU_C) * r * softplus
    a = jnp.exp(log_a)
    u = jnp.sqrt(-jnp.tanh(log_a) * (a * a + 1.0)) * (gi * xc)

    def scan(A, B, forward):
        k = 1
        while k < TL:
            if forward:
                keep = row >= k
                As = jnp.where(keep, pltpu.roll(A, k, 0), 1.0)
                Bs = jnp.where(keep, pltpu.roll(B, k, 0), 0.0)
            else:
                keep = row < TL - k
                As = jnp.where(keep, pltpu.roll(A, TL - k, 0), 1.0)
                Bs = jnp.where(keep, pltpu.roll(B, TL - k, 0), 0.0)
            B = A * Bs + B
            A = A * As
            k *= 2
        return A, B

    carry = carry_ref[0:1, :]

    @pl.when(d == 0)
    def _():
        A, B = scan(a, u, True)
        hs = B + A * carry
        o_ref[0] = hs
        carry_ref[0:1, :] = hs[TL - 1:TL]

    @pl.when(d == 1)
    def _():
        A, B = scan(a, u, False)
        hs = B + A * carry
        o_ref[0] = hs
        carry_ref[0:1, :] = hs[0:1]


def _lru(xb, conv_w, conv_b, wa, wi, ba, bi, lam):
    per_dir = lambda *tail: pl.BlockSpec((1,) + tail, lambda d, j: (d,) + (0,) * len(tail))
    return pl.pallas_call(
        _lru_kernel,
        grid=(2, N_LRU_STEPS),
        in_specs=[
            _resident((T_PAD, LRU_WIDTH)),
            pl.BlockSpec((4, LRU_WIDTH), lambda d, j: (0, 0)),
            pl.BlockSpec((1, LRU_WIDTH), lambda d, j: (0, 0)),
            per_dir(LRU_WIDTH, LRU_WIDTH),
            per_dir(LRU_WIDTH, LRU_WIDTH),
            per_dir(1, LRU_WIDTH),
            per_dir(1, LRU_WIDTH),
            per_dir(1, LRU_WIDTH),
        ],
        out_specs=pl.BlockSpec((1, TL, LRU_WIDTH), lambda d, j: (d, _lru_tile_index(d, j), 0)),
        out_shape=jax.ShapeDtypeStruct((2, T_PAD, LRU_WIDTH), F32),
        scratch_shapes=[pltpu.VMEM((8, LRU_WIDTH), F32)],
        compiler_params=_params(2),
        name="rglru",
    )(xb, conv_w, conv_b, wa, wi, ba, bi, lam)


def _softmax_chunk(s, m_old, l_old):
    m_new = jnp.maximum(m_old, jnp.max(s, axis=0, keepdims=True))
    alpha = jnp.exp2(m_old - m_new)
    p = jnp.exp2(s - m_new)
    l_new = alpha * l_old + jnp.sum(p, axis=0, keepdims=True)
    return p.astype(BF16), m_new, l_new, alpha


def _attend_ctx(kc, vc, weights):
    outs = []
    for w in weights:
        s = jnp.dot(kc, w, preferred_element_type=F32)
        p, _, l, _ = _softmax_chunk(s, jnp.full((1, CTX_LEN), NEG_BIG, F32),
                                    jnp.zeros((1, CTX_LEN), F32))
        outs.append(jnp.dot(vc, p, preferred_element_type=F32) * (1.0 / l))
    return outs


def _key_norm_bound(k_ref, kmax_ref):
    r = lax.broadcasted_iota(jnp.int32, (LANES, LANES), 0) // HEAD_DIM
    c = lax.broadcasted_iota(jnp.int32, (LANES, LANES), 1) // HEAD_DIM
    ones = jnp.where(r == c, 1.0, 0.0).astype(BF16)

    def chunk(i, best):
        kf = k_ref[pl.ds(pl.multiple_of(i * TK, TK), TK), :].astype(F32)
        n2 = jnp.dot((kf * kf).astype(BF16), ones, preferred_element_type=F32)
        return jnp.maximum(best, jnp.max(n2, axis=0, keepdims=True))

    kmax_ref[0:1, :] = lax.fori_loop(0, N_K_CHUNKS, chunk, jnp.zeros((1, LANES), F32))


def _attend_lat_fast(k_ref, v_chunk, weights, shift, acc_ref, p_refs):
    maps = range(len(weights))

    def scores_exp(c, p_ref, l):
        kc = k_ref[pl.ds(pl.multiple_of(c * TK_FAST, TK_FAST), TK_FAST), :]
        l_new = []
        for j in maps:
            p = jnp.exp2(jnp.dot(kc, weights[j], preferred_element_type=F32) - shift[j])
            l_new.append(l[j] + jnp.sum(p, axis=0, keepdims=True))
            p_ref[j] = p.astype(BF16)
        return tuple(l_new)

    def values(c, p_ref):
        vc = v_chunk(pl.multiple_of(c * TK_FAST, TK_FAST), TK_FAST)
        for j in maps:
            acc_ref[j] += jnp.dot(vc, p_ref[j], preferred_element_type=F32)

    acc_ref[...] = jnp.zeros_like(acc_ref)
    l = scores_exp(0, p_refs[0], tuple(jnp.zeros((1, TQ), F32) for _ in maps))

    def chunk_pair(i, l):
        c0 = 2 * i
        l = scores_exp(c0 + 1, p_refs[1], l)
        values(c0, p_refs[0])
        l = scores_exp(c0 + 2, p_refs[0], l)
        values(c0 + 1, p_refs[1])
        return l

    n_pairs = (N_K_FAST - 1) // 2
    l = lax.fori_loop(0, n_pairs, chunk_pair, l)
    if N_K_FAST % 2 == 0:
        l = scores_exp(N_K_FAST - 1, p_refs[1], l)
        values(N_K_FAST - 2, p_refs[0])
        values(N_K_FAST - 1, p_refs[1])
    else:
        values(N_K_FAST - 1, p_refs[0])
    return l


def _attend_lat_exact(k_ref, v_chunk, weights, acc_ref, s_refs, p_refs):
    maps = range(len(weights))

    def scores(c, s_ref):
        kc = k_ref[pl.ds(pl.multiple_of(c * TK, TK), TK), :]
        for j in maps:
            s_ref[j] = jnp.dot(kc, weights[j], preferred_element_type=F32)

    def softmax(s_ref, p_ref, m, l):
        m_new, l_new, alpha = [], [], []
        for j in maps:
            m_j = jnp.maximum(m[j], jnp.max(s_ref[j], axis=0, keepdims=True))
            a_j = jnp.exp2(m[j] - m_j)
            p = jnp.exp2(s_ref[j] - m_j)
            l_new.append(a_j * l[j] + jnp.sum(p, axis=0, keepdims=True))
            p_ref[j, 0:TK, :] = p.astype(BF16)
            m_new.append(m_j)
            alpha.append(a_j)
        return m_new, l_new, alpha

    def values(c, p_ref, alpha):
        vc = v_chunk(pl.multiple_of(c * TK, TK), TK)
        for j in maps:
            acc_ref[j] = alpha[j] * acc_ref[j] + jnp.dot(vc, p_ref[j, 0:TK, :],
                                                         preferred_element_type=F32)

    acc_ref[...] = jnp.zeros_like(acc_ref)
    p_refs[1][...] = jnp.zeros_like(p_refs[1])
    scores(0, s_refs[0])
    stat = lambda v: [jnp.full((1, TQ), v, F32) for _ in maps]

    def chunk_pair(i, carry):
        m, l, alpha_b = carry
        c0 = 2 * i
        scores(c0 + 1, s_refs[1])
        m, l, alpha_a = softmax(s_refs[0], p_refs[0], m, l)
        values(jnp.maximum(c0 - 1, 0), p_refs[1], alpha_b)
        scores(jnp.minimum(c0 + 2, N_K_CHUNKS - 1), s_refs[0])
        m, l, alpha_b = softmax(s_refs[1], p_refs[1], m, l)
        values(c0, p_refs[0], alpha_a)
        return m, l, alpha_b

    _, l, alpha_b = lax.fori_loop(0, N_K_CHUNKS // 2, chunk_pair,
                                  (stat(NEG_BIG), stat(0.0), stat(1.0)))
    values(N_K_CHUNKS - 1, p_refs[1], alpha_b)
    return l


def _attend(ctx_only, k_ref, vt_at, weights, halves, scratch):
    if ctx_only:
        return _attend_ctx(k_ref[...], vt_at(0, CTX_LEN), weights)
    acc_ref, l_ref, kmax_ref, s0_ref, s1_ref, p0_ref, p1_ref = scratch
    maps = range(len(weights))

    @pl.when(pl.program_id(1) == 0)
    def _():
        _key_norm_bound(k_ref, kmax_ref)

    lane = lax.broadcasted_iota(jnp.int32, (1, LANES), 1)
    shift = []
    for j in maps:
        k2 = jnp.max(jnp.where(lane // HEAD_DIM == halves[j], kmax_ref[0:1, :], 0.0),
                     axis=-1, keepdims=True)
        wf = weights[j].astype(F32)
        q2 = jnp.sum(wf * wf, axis=0, keepdims=True)
        shift.append(jnp.sqrt(q2 * k2) * BOUND_SLACK - SHIFT_GUARD)

    l = _attend_lat_fast(k_ref, vt_at, weights, shift, acc_ref, (p0_ref, p1_ref))
    n_bad = jnp.zeros((), F32)
    for j in maps:
        l_ref[j:j + 1, :] = l[j]
        ok = jnp.logical_and(l[j] >= L_MIN, l[j] <= L_MAX)
        n_bad = n_bad + jnp.sum(jnp.where(ok, 0.0, 1.0))

    @pl.when(n_bad > 0.0)
    def _():
        l = _attend_lat_exact(k_ref, vt_at, weights, acc_ref, (s0_ref, s1_ref), (p0_ref, p1_ref))
        for j in maps:
            l_ref[j:j + 1, :] = l[j]

    return [acc_ref[j] * (1.0 / l_ref[j:j + 1, :]) for j in maps]


def _attn_call(kernel, ctx_only, n_heads, dv, in_blocks, operands, out_width, name):
    if ctx_only:
        tq, n_q, n_keys, q0, k0 = CTX_LEN, 1, CTX_LEN, SEQ // CTX_LEN, SEQ // CTX_LEN
        scratch = []
    else:
        tq, n_q, n_keys, q0, k0 = TQ, SEQ // TQ, T_ALL, 0, 0
        scratch = [pltpu.VMEM((2, dv, TQ), F32), pltpu.VMEM((8, TQ), F32),
                   pltpu.VMEM((8, LANES), F32),
                   pltpu.VMEM((2, TK, TQ), F32), pltpu.VMEM((2, TK, TQ), F32),
                   pltpu.VMEM((2, TK_FAST, TQ), BF16), pltpu.VMEM((2, TK_FAST, TQ), BF16)]
    return pl.pallas_call(
        functools.partial(kernel, ctx_only),
        grid=(n_heads, n_q),
        in_specs=in_blocks(tq, n_keys, q0, k0),
        out_specs=pl.BlockSpec((tq, LANES), lambda h, i: (i, h)),
        out_shape=jax.ShapeDtypeStruct((n_q * tq, out_width), BF16),
        scratch_shapes=scratch,
        compiler_params=_params(2),
        name=name + ("_ctx" if ctx_only else ""),
    )(*operands)


def _da_kernel(lam_init, ctx_only, qt_ref, k_ref, vt_ref, lam_ref, g_ref, o_ref, *scratch):
    qt = qt_ref[0]
    row = lax.broadcasted_iota(jnp.int32, qt.shape, 0)
    zero = jnp.zeros_like(qt)
    weights = [jnp.where(row < HEAD_DIM, qt, zero), jnp.where(row >= HEAD_DIM, qt, zero)]
    o1, o2 = _attend(ctx_only, k_ref, lambda s, n: vt_ref[0, :, pl.ds(s, n)], weights, (0, 1),
                     scratch)
    lf = lam_ref[...]
    lam = (jnp.exp(jnp.sum(lf[0:1] * lf[1:2], axis=-1, keepdims=True))
           - jnp.exp(jnp.sum(lf[2:3] * lf[3:4], axis=-1, keepdims=True)) + lam_init)
    o = o1 - lam * o2
    ms = jnp.mean(o * o, axis=0, keepdims=True)
    on = (o * lax.rsqrt(ms + EPS)) * g_ref[...] * (1.0 - lam_init)
    o_ref[...] = on.T.astype(o_ref.dtype)


def _da_attention(qt, k, vt, da_lam, subln_g, lam_init, ctx_only):
    def in_blocks(tq, n_keys, q0, k0):
        return [
            pl.BlockSpec((1, LANES, tq), lambda h, i: (h, 0, i + q0)),
            pl.BlockSpec((n_keys, LANES), lambda h, i: (k0, h)),
            pl.BlockSpec((1, LANES, n_keys), lambda h, i: (h, 0, k0)),
            pl.BlockSpec((4, HEAD_DIM), lambda h, i: (0, 0)),
            pl.BlockSpec((LANES, 1), lambda h, i: (0, 0)),
        ]
    return _attn_call(functools.partial(_da_kernel, lam_init), ctx_only, DA_HEADS, LANES,
                      in_blocks, (qt, k, vt, da_lam, subln_g), DA_HEADS * LANES, "da_attention")


def _gqa_kernel(ctx_only, qt_ref, k_ref, vt_ref, o_ref, *scratch):
    g = pl.program_id(0)
    qt = qt_ref[0]
    zero = jnp.zeros((HEAD_DIM, qt.shape[1]), qt.dtype)
    weights = []
    for j in range(2):
        qj = qt[j * HEAD_DIM:(j + 1) * HEAD_DIM]
        weights.append(jnp.where(g == 0, jnp.concatenate([qj, zero], axis=0),
                                 jnp.concatenate([zero, qj], axis=0)))
    o0, o1 = _attend(ctx_only, k_ref, lambda s, n: vt_ref[:, pl.ds(s, n)], weights, (g, g),
                     scratch)
    o_ref[...] = jnp.concatenate([o0, o1], axis=0).T.astype(o_ref.dtype)


def _gqa_attention(qt, k, vt, ctx_only):
    def in_blocks(tq, n_keys, q0, k0):
        return [
            pl.BlockSpec((1, LANES, tq), lambda g, i: (g, 0, i + q0)),
            pl.BlockSpec((n_keys, LANES), lambda g, i: (k0, 0)),
            pl.BlockSpec((HEAD_DIM, n_keys), lambda g, i: (g, k0)),
        ]
    return _attn_call(_gqa_kernel, ctx_only, GQA_KV_HEADS, HEAD_DIM, in_blocks, (qt, k, vt),
                      GQA_KV_HEADS * LANES, "gqa_attention")


def _rope_tables():
    half = HEAD_DIM // 4
    t = jnp.arange(SEQ, dtype=jnp.int32)
    row = (t // GRID_W).astype(F32)
    col = (t % GRID_W).astype(F32)
    inv = ROPE_THETA ** (-jnp.arange(half, dtype=F32) / half)
    ang_r = row[:, None] * inv
    ang_c = col[:, None] * inv
    cos64 = jnp.concatenate([jnp.cos(ang_r)] * 2 + [jnp.cos(ang_c)] * 2, axis=-1)
    sin64 = jnp.concatenate([-jnp.sin(ang_r), jnp.sin(ang_r), -jnp.sin(ang_c), jnp.sin(ang_c)],
                            axis=-1)
    n_rest = T_PAD - SEQ
    cos_t = jnp.concatenate([jnp.tile(cos64, (1, 2)), jnp.ones((n_rest, LANES), F32)], axis=0)
    sin_t = jnp.concatenate([jnp.tile(sin64, (1, 2)), jnp.zeros((n_rest, LANES), F32)], axis=0)
    return cos_t, sin_t


def _block_diag(w):
    eye = jnp.eye(LRU_BLOCKS, dtype=w.dtype)
    full = jnp.einsum('dkij,kl->dkilj', w, eye)
    return full.reshape(2, LRU_WIDTH, LRU_WIDTH).astype(BF16)


def kernel(x, c, ctx, c_ctx, ada_w, ada_b, norm_g, ffn1_w13, ffn1_w2, ffn2_w13, ffn2_w2,
           w_in, w_out, da_lam, da_subln_g, qk_norm_g, lru_conv_w, lru_conv_b,
           lru_wa, lru_ba, lru_wi, lru_bi, lru_lambda, final_g):
    assert x.shape == (1, SEQ, D_MODEL) and ctx.shape == (1, CTX_LEN, D_MODEL)
    cc = jnp.zeros((8, D_MODEL), F32).at[0].set(c[0]).at[1].set(c_ctx)
    mods_all = _ada_mods(cc, ada_w, ada_b).reshape(DEPTH, 8, N_MOD, D_MODEL)[:, :2]
    cos_t, sin_t = _rope_tables()
    ffn1_w13, ffn1_w2, ffn2_w13, ffn2_w2, w_in, w_out = (
        w.astype(BF16) for w in (ffn1_w13, ffn1_w2, ffn2_w13, ffn2_w2, w_in, w_out))

    def stream(lat, ctx_rows):
        rest = jnp.zeros((T_PAD - T_ALL, lat.shape[1]), lat.dtype)
        return jnp.concatenate([lat, ctx_rows, rest], axis=0)

    h = stream(x[0], ctx[0])
    for l in range(DEPTH):
        last = l == DEPTH - 1
        lam_init = 0.8 - 0.6 * math.exp(-0.3 * l)
        mods = mods_all[l]

        h = _ffn(h, mods, norm_g[l, 0:1], ffn1_w13, ffn1_w2, l, 0, N_TILES)

        qk_gain = jnp.tile(qk_norm_g[l], (1, 2))
        qt_da, k_da, vt_da, qt_g, k_g, vt_g, xb, gb = _inproj(
            h, mods, norm_g[l, 1:2], w_in, l, cos_t, sin_t, qk_gain)

        hs = _lru(xb, lru_conv_w[l], lru_conv_b[l][None, :], _block_diag(lru_wa[l]),
                  _block_diag(lru_wi[l]), lru_ba[l][:, None, :], lru_bi[l][:, None, :],
                  lru_lambda[l][:, None, :])

        n_tiles = N_LAT_TILES if last else N_TILES
        da_args = (qt_da, k_da, vt_da, da_lam[l], da_subln_g[l][:, None], lam_init)
        a = _da_attention(*da_args, False)
        g = _gqa_attention(qt_g, k_g, vt_g, False)
        if not last:
            a = stream(a, _da_attention(*da_args, True))
            g = stream(g, _gqa_attention(qt_g, k_g, vt_g, True))

        h = _ffn(h, mods, norm_g[l, 2:3], ffn2_w13, ffn2_w2, l, 6, n_tiles,
                 mixer=(a, g, hs, gb, w_out), final_g=final_g[None, :] if last else None)
    return h[None]
```

```python
import functools
import math

import jax
import jax.numpy as jnp
from jax import lax
from jax.experimental import pallas as pl
from jax.experimental.pallas import tpu as pltpu

D_MODEL = 1024
SEQ = 16384
CTX_LEN = 256
T_ALL = SEQ + CTX_LEN
DEPTH = 4
GRID_W = 64
HEAD_DIM = 64
SCALE = HEAD_DIM ** -0.5
LOG2E = math.log2(math.e)
ROPE_THETA = 10000.0
DA_HEADS = 4
GQA_KV_HEADS = 2
LRU_WIDTH = 256
LRU_BLOCKS = 4
LRU_BLOCK = LRU_WIDTH // LRU_BLOCKS
LRU_C = 8.0
IN_WIDTH = 2560
D_FF = 2816
N_MOD = 9
EPS = 1e-6

LANES = 128
MXU_DIM = 256

TM = 512
T_PAD = SEQ + TM
N_TILES = T_PAD // TM
N_LAT_TILES = SEQ // TM
TL = 256
N_LRU_LAT = SEQ // TL
FF_CHUNK = MXU_DIM
N_FF_CHUNKS = D_FF // FF_CHUNK
TQ = 1024
TK = 640
N_K_CHUNKS = T_ALL // TK
TK_FAST = 1280
N_K_FAST = T_ALL // TK_FAST
NEG_BIG = -1e30
SHIFT_GUARD = 32.0
BOUND_SLACK = 1.02
L_MIN = 2.0 ** -60
L_MAX = 2.0 ** 100
VMEM_LIMIT = 48 * 1024 * 1024

F32 = jnp.float32
BF16 = jnp.bfloat16


def _params(n_axes):
    return pltpu.CompilerParams(dimension_semantics=("arbitrary",) * n_axes,
                                vmem_limit_bytes=VMEM_LIMIT)


def _resident(shape):
    zeros = (0,) * len(shape)
    return pl.BlockSpec(shape, lambda *_: zeros, pipeline_mode=pl.Buffered(1))


def _sigmoid(x):
    return 1.0 / (1.0 + jnp.exp(-x))


def _norm_mod(x, g, shift, scale):
    ms = jnp.mean(x * x, axis=-1, keepdims=True)
    y = (x * lax.rsqrt(ms + EPS)) * g
    return y * (1.0 + scale) + shift


ADA_COLS = 1152


def _ada_kernel(c_ref, w_ref, b_ref, o_ref):
    c = c_ref[...]
    s = (c * _sigmoid(c)).astype(BF16)
    o_ref[0] = jnp.dot(s, w_ref[0].astype(BF16), preferred_element_type=F32) + b_ref[0]


def _ada_mods(cc, ada_w, ada_b):
    width = N_MOD * D_MODEL
    return pl.pallas_call(
        _ada_kernel,
        grid=(DEPTH, width // ADA_COLS),
        in_specs=[
            pl.BlockSpec((8, D_MODEL), lambda l, j: (0, 0)),
            pl.BlockSpec((1, D_MODEL, ADA_COLS), lambda l, j: (l, 0, j)),
            pl.BlockSpec((1, 1, ADA_COLS), lambda l, j: (l, 0, j)),
        ],
        out_specs=pl.BlockSpec((1, 8, ADA_COLS), lambda l, j: (l, 0, j)),
        out_shape=jax.ShapeDtypeStruct((DEPTH, 8, width), F32),
        compiler_params=_params(2),
        name="ada_mods",
    )(cc, ada_w, ada_b.reshape(DEPTH, 1, width))


def _gelu_tanh(x):
    return 0.5 * x * (1.0 + jnp.tanh(math.sqrt(2.0 / math.pi) * (x + 0.044715 * (x * x * x))))


def _mixer_out(a_ref, g_ref, hs_ref, gb_ref, w_ref):
    r = ((hs_ref[0] + hs_ref[1]) * _gelu_tanh(gb_ref[...])).astype(BF16)
    n_a = DA_HEADS * LANES
    n_g = n_a + GQA_KV_HEADS * LANES
    return (jnp.dot(a_ref[...], w_ref[0, 0:n_a, :], preferred_element_type=F32)
            + jnp.dot(g_ref[...], w_ref[0, n_a:n_g, :], preferred_element_type=F32)
            + jnp.dot(r, w_ref[0, n_g:, :], preferred_element_type=F32))


def _ffn_kernel(i0, with_mixer, with_final_norm, h_ref, mod_ref, g_ref, w13_ref, w2_ref, *rest):
    rest = list(rest)
    mixer_refs = [rest.pop(0) for _ in range(5)] if with_mixer else None
    final_g_ref = rest.pop(0) if with_final_norm else None
    o_ref, acc_ref, gate_ref = rest
    x = h_ref[...]
    mods = mod_ref[0]
    if with_mixer:
        x = x + mods[5:6] * _mixer_out(*mixer_refs)
    z = _norm_mod(x, g_ref[...], mods[i0:i0 + 1], mods[i0 + 1:i0 + 2]).astype(BF16)
    acc_ref[...] = jnp.zeros_like(acc_ref)

    def gate(c):
        lo = c * FF_CHUNK
        a = jnp.dot(z, w13_ref[0, :, lo:lo + FF_CHUNK], preferred_element_type=F32)
        b = jnp.dot(z, w13_ref[0, :, D_FF + lo:D_FF + lo + FF_CHUNK], preferred_element_type=F32)
        return (a * _sigmoid(a) * b).astype(BF16)

    def down(c):
        acc_ref[...] += jnp.dot(gate_ref[...], w2_ref[0, c * FF_CHUNK:(c + 1) * FF_CHUNK, :],
                                preferred_element_type=F32)

    gate_ref[...] = gate(0)
    for c in range(1, N_FF_CHUNKS):
        down(c - 1)
        gate_ref[...] = gate(c)
    down(N_FF_CHUNKS - 1)
    y = x + (0.5 * mods[i0 + 2:i0 + 3]) * acc_ref[...]
    if with_final_norm:
        y = (y * lax.rsqrt(jnp.mean(y * y, axis=-1, keepdims=True) + EPS)) * final_g_ref[...]
    o_ref[...] = y


def _layer_resident(shape, layer):
    index = (layer,) + (0,) * len(shape)
    return pl.BlockSpec((1,) + shape, lambda *_: index, pipeline_mode=pl.Buffered(1))


def _ffn(h, mods, g, w13, w2, layer, i0, n_tiles, mixer=None, final_g=None):
    row_tile = lambda width: pl.BlockSpec((TM, width), lambda i: (i, 0))
    vector = pl.BlockSpec((1, D_MODEL), lambda i: (0, 0))
    in_specs = [
        row_tile(D_MODEL),
        pl.BlockSpec((1, N_MOD, D_MODEL), lambda i: (i // N_LAT_TILES, 0, 0)),
        vector,
        _layer_resident((D_MODEL, 2 * D_FF), layer),
        _layer_resident((D_FF, D_MODEL), layer),
    ]
    operands = [h, mods, g, w13, w2]
    if mixer is not None:
        in_specs += [row_tile(DA_HEADS * LANES), row_tile(GQA_KV_HEADS * LANES),
                     pl.BlockSpec((2, TM, LRU_WIDTH), lambda i: (0, i, 0)), row_tile(LRU_WIDTH),
                     _layer_resident((D_MODEL, D_MODEL), layer)]
        operands += list(mixer)
    if final_g is not None:
        in_specs.append(vector)
        operands.append(final_g)
    return pl.pallas_call(
        functools.partial(_ffn_kernel, i0, mixer is not None, final_g is not None),
        grid=(n_tiles,),
        in_specs=in_specs,
        out_specs=row_tile(D_MODEL),
        out_shape=jax.ShapeDtypeStruct((n_tiles * TM, D_MODEL), F32),
        scratch_shapes=[pltpu.VMEM((TM, D_MODEL), F32), pltpu.VMEM((TM, FF_CHUNK), BF16)],
        compiler_params=_params(1),
        name="ffn",
    )(*operands)


def _swap16(x):
    lane = lax.broadcasted_iota(jnp.int32, x.shape, 1)
    return jnp.where(lane % 32 < 16, pltpu.roll(x, LANES - 16, 1), pltpu.roll(x, 16, 1))


def _rope(x, cos, sin):
    return x * cos + _swap16(x) * sin


def _head_mean_sq(x):
    sq = x * x
    hi = sq.astype(BF16)
    lo = (sq - hi.astype(F32)).astype(BF16)
    r = lax.broadcasted_iota(jnp.int32, (LANES, LANES), 0) // HEAD_DIM
    c = lax.broadcasted_iota(jnp.int32, (LANES, LANES), 1) // HEAD_DIM
    ones = jnp.where(r == c, 1.0, 0.0).astype(BF16)
    tot = (jnp.dot(hi, ones, preferred_element_type=F32)
           + jnp.dot(lo, ones, preferred_element_type=F32))
    return tot * (1.0 / HEAD_DIM)


def _inproj_kernel(h_ref, mod_ref, g_ref, w_ref, cos_ref, sin_ref, qkg_ref,
                   qt_da_ref, k_da_ref, vt_da_ref, qt_g_ref, k_g_ref, vt_g_ref, xb_ref, gb_ref):
    mods = mod_ref[0]
    z = _norm_mod(h_ref[...], g_ref[...], mods[3:4], mods[4:5]).astype(BF16)
    proj = jnp.dot(z, w_ref[0], preferred_element_type=F32)
    cos = cos_ref[...]
    sin = sin_ref[...]
    q_scale = SCALE * LOG2E

    def slab(j):
        return proj[:, j * LANES:(j + 1) * LANES]

    for h in range(DA_HEADS):
        q = _rope(slab(h), cos, sin) * q_scale
        qt_da_ref[h] = q.T.astype(BF16)
        k_da_ref[:, h * LANES:(h + 1) * LANES] = _rope(slab(4 + h), cos, sin).astype(BF16)
        vt_da_ref[h] = slab(8 + h).T.astype(BF16)

    gq_gain = qkg_ref[0:1, :]
    gk_gain = qkg_ref[1:2, :]
    for g in range(GQA_KV_HEADS):
        q = slab(12 + g)
        q = (q * lax.rsqrt(_head_mean_sq(q) + EPS)) * gq_gain
        qt_g_ref[g] = (_rope(q, cos, sin) * q_scale).T.astype(BF16)
    k = slab(14)
    k = (k * lax.rsqrt(_head_mean_sq(k) + EPS)) * gk_gain
    k_g_ref[...] = _rope(k, cos, sin).astype(BF16)
    vt_g_ref[...] = slab(15).T.astype(BF16)
    xb_ref[...] = proj[:, 2048:2304]
    gb_ref[...] = proj[:, 2304:2560]


def _inproj(h, mods, g, w_in, layer, cos_t, sin_t, qk_gain):
    row_tile = lambda width: pl.BlockSpec((TM, width), lambda i: (i, 0))
    return pl.pallas_call(
        _inproj_kernel,
        grid=(N_TILES,),
        in_specs=[
            row_tile(D_MODEL),
            pl.BlockSpec((1, N_MOD, D_MODEL), lambda i: (i // N_LAT_TILES, 0, 0)),
            pl.BlockSpec((1, D_MODEL), lambda i: (0, 0)),
            _layer_resident((D_MODEL, IN_WIDTH), layer),
            row_tile(LANES),
            row_tile(LANES),
            pl.BlockSpec((2, LANES), lambda i: (0, 0)),
        ],
        out_specs=[
            pl.BlockSpec((DA_HEADS, LANES, TM), lambda i: (0, 0, i)),
            row_tile(DA_HEADS * LANES),
            pl.BlockSpec((DA_HEADS, LANES, TM), lambda i: (0, 0, i)),
            pl.BlockSpec((GQA_KV_HEADS, LANES, TM), lambda i: (0, 0, i)),
            row_tile(LANES),
            pl.BlockSpec((LANES, TM), lambda i: (0, i)),
            row_tile(LRU_WIDTH),
            row_tile(LRU_WIDTH),
        ],
        out_shape=[
            jax.ShapeDtypeStruct((DA_HEADS, LANES, T_PAD), BF16),
            jax.ShapeDtypeStruct((T_PAD, DA_HEADS * LANES), BF16),
            jax.ShapeDtypeStruct((DA_HEADS, LANES, T_PAD), BF16),
            jax.ShapeDtypeStruct((GQA_KV_HEADS, LANES, T_PAD), BF16),
            jax.ShapeDtypeStruct((T_PAD, LANES), BF16),
            jax.ShapeDtypeStruct((LANES, T_PAD), BF16),
            jax.ShapeDtypeStruct((T_PAD, LRU_WIDTH), F32),
            jax.ShapeDtypeStruct((T_PAD, LRU_WIDTH), F32),
        ],
        compiler_params=_params(1),
        name="inproj",
    )(h, mods, g, w_in, cos_t, sin_t, qk_gain)


N_LRU_STEPS = N_LRU_LAT + 2


def _lru_tile_index(d, j):
    lat = jnp.where(d == 0, j - 1, N_LRU_LAT - j)
    return jnp.where(j == 0, N_LRU_LAT, jnp.where(j == N_LRU_STEPS - 1, N_LRU_LAT + 1, lat))


def _lru_kernel(x_ref, cw_ref, cb_ref, wa_ref, wi_ref, ba_ref, bi_ref, lam_ref, o_ref, carry_ref):
    d = pl.program_id(0)
    j = pl.program_id(1)
    ti = _lru_tile_index(d, j)
    t0 = pl.multiple_of(ti * TL, TL)

    @pl.when(j == 0)
    def _():
        carry_ref[...] = jnp.zeros_like(carry_ref)

    x = x_ref[pl.ds(t0, TL), :]
    prev8 = x_ref[pl.ds(pl.multiple_of(jnp.maximum(t0 - 8, 0), 8), 8), :]
    next8 = x_ref[pl.ds(pl.multiple_of(jnp.minimum(t0 + TL, T_PAD - 8), 8), 8), :]
    has_prev = jnp.logical_and(ti != 0, ti < N_LRU_LAT)
    has_next = ti < N_LRU_LAT - 1
    prev_row = jnp.where(has_prev, prev8[7:8], 0.0)
    next_row0 = jnp.where(has_next, next8[0:1], 0.0)
    next_row1 = jnp.where(has_next, next8[1:2], 0.0)
    row = lax.broadcasted_iota(jnp.int32, (TL, LRU_WIDTH), 0)
    xm1 = jnp.where(row == 0, prev_row, pltpu.roll(x, 1, 0))
    xp1 = jnp.where(row == TL - 1, next_row0, pltpu.roll(x, TL - 1, 0))
    xp2 = jnp.where(row == TL - 2, next_row0,
                    jnp.where(row == TL - 1, next_row1, pltpu.roll(x, TL - 2, 0)))
    cw = cw_ref[...]
    xc = xm1 * cw[0:1] + x * cw[1:2] + xp1 * cw[2:3] + xp2 * cw[3:4] + cb_ref[...]

    xcb = xc.astype(BF16)
    r = _sigmoid(jnp.dot(xcb, wa_ref[0], preferred_element_type=F32) + ba_ref[0])
    gi = _sigmoid(jnp.dot(xcb, wi_ref[0], preferred_element_type=F32) + bi_ref[0])
    neg_lam = -lam_ref[0]
    softplus = jnp.maximum(neg_lam, 0.0) + jnp.log1p(jnp.exp(-jnp.abs(neg_lam)))
    log_a = (-LRU_C) * r * softplus
    a = jnp.exp(log_a)
    u = jnp.sqrt(-jnp.tanh(log_a) * (a * a + 1.0)) * (gi * xc)

    def scan(A, B, forward):
        k = 1
        while k < TL:
            if forward:
                keep = row >= k
                As = jnp.where(keep, pltpu.roll(A, k, 0), 1.0)
                Bs = jnp.where(keep, pltpu.roll(B, k, 0), 0.0)
            else:
                keep = row < TL - k
                As = jnp.where(keep, pltpu.roll(A, TL - k, 0), 1.0)
                Bs = jnp.where(keep, pltpu.roll(B, TL - k, 0), 0.0)
            B = A * Bs + B
            A = A * As
            k *= 2
        return A, B

    carry = carry_ref[0:1, :]

    @pl.when(d == 0)
    def _():
        A, B = scan(a, u, True)
        hs = B + A * carry
        o_ref[0] = hs
        carry_ref[0:1, :] = hs[TL - 1:TL]

    @pl.when(d == 1)
    def _():
        A, B = scan(a, u, False)
        hs = B + A * carry
        o_ref[0] = hs
        carry_ref[0:1, :] = hs[0:1]


def _lru(xb, conv_w, conv_b, wa, wi, ba, bi, lam):
    per_dir = lambda *tail: pl.BlockSpec((1,) + tail, lambda d, j: (d,) + (0,) * len(tail))
    return pl.pallas_call(
        _lru_kernel,
        grid=(2, N_LRU_STEPS),
        in_specs=[
            _resident((T_PAD, LRU_WIDTH)),
            pl.BlockSpec((4, LRU_WIDTH), lambda d, j: (0, 0)),
            pl.BlockSpec((1, LRU_WIDTH), lambda d, j: (0, 0)),
            per_dir(LRU_WIDTH, LRU_WIDTH),
            per_dir(LRU_WIDTH, LRU_WIDTH),
            per_dir(1, LRU_WIDTH),
            per_dir(1, LRU_WIDTH),
            per_dir(1, LRU_WIDTH),
        ],
        out_specs=pl.BlockSpec((1, TL, LRU_WIDTH), lambda d, j: (d, _lru_tile_index(d, j), 0)),
        out_shape=jax.ShapeDtypeStruct((2, T_PAD, LRU_WIDTH), F32),
        scratch_shapes=[pltpu.VMEM((8, LRU_WIDTH), F32)],
        compiler_params=_params(2),
        name="rglru",
    )(xb, conv_w, conv_b, wa, wi, ba, bi, lam)


def _softmax_chunk(s, m_old, l_old):
    m_new = jnp.maximum(m_old, jnp.max(s, axis=0, keepdims=True))
    alpha = jnp.exp2(m_old - m_new)
    p = jnp.exp2(s - m_new)
    l_new = alpha * l_old + jnp.sum(p, axis=0, keepdims=True)
    return p.astype(BF16), m_new, l_new, alpha


def _attend_ctx(kc, vc, weights):
    outs = []
    for w in weights:
        s = jnp.dot(kc, w, preferred_element_type=F32)
        p, _, l, _ = _softmax_chunk(s, jnp.full((1, CTX_LEN), NEG_BIG, F32),
                                    jnp.zeros((1, CTX_LEN), F32))
        outs.append(jnp.dot(vc, p, preferred_element_type=F32) * (1.0 / l))
    return outs


def _key_norm_bound(k_ref, kmax_ref):
    r = lax.broadcasted_iota(jnp.int32, (LANES, LANES), 0) // HEAD_DIM
    c = lax.broadcasted_iota(jnp.int32, (LANES, LANES), 1) // HEAD_DIM
    ones = jnp.where(r == c, 1.0, 0.0).astype(BF16)

    def chunk(i, best):
        kf = k_ref[pl.ds(pl.multiple_of(i * TK, TK), TK), :].astype(F32)
        n2 = jnp.dot((kf * kf).astype(BF16), ones, preferred_element_type=F32)
        return jnp.maximum(best, jnp.max(n2, axis=0, keepdims=True))

    kmax_ref[0:1, :] = lax.fori_loop(0, N_K_CHUNKS, chunk, jnp.zeros((1, LANES), F32))


def _attend_lat_fast(k_ref, v_chunk, weights, shift, acc_ref, p_refs):
    maps = range(len(weights))

    def scores_exp(c, p_ref, l):
        kc = k_ref[pl.ds(pl.multiple_of(c * TK_FAST, TK_FAST), TK_FAST), :]
        l_new = []
        for j in maps:
            p = jnp.exp2(jnp.dot(kc, weights[j], preferred_element_type=F32) - shift[j])
            l_new.append(l[j] + jnp.sum(p, axis=0, keepdims=True))
            p_ref[j] = p.astype(BF16)
        return tuple(l_new)

    def values(c, p_ref):
        vc = v_chunk(pl.multiple_of(c * TK_FAST, TK_FAST), TK_FAST)
        for j in maps:
            acc_ref[j] += jnp.dot(vc, p_ref[j], preferred_element_type=F32)

    acc_ref[...] = jnp.zeros_like(acc_ref)
    l = scores_exp(0, p_refs[0], tuple(jnp.zeros((1, TQ), F32) for _ in maps))

    def chunk_pair(i, l):
        c0 = 2 * i
        l = scores_exp(c0 + 1, p_refs[1], l)
        values(c0, p_refs[0])
        l = scores_exp(c0 + 2, p_refs[0], l)
        values(c0 + 1, p_refs[1])
        return l

    n_pairs = (N_K_FAST - 1) // 2
    l = lax.fori_loop(0, n_pairs, chunk_pair, l, unroll=3)
    if N_K_FAST % 2 == 0:
        l = scores_exp(N_K_FAST - 1, p_refs[1], l)
        values(N_K_FAST - 2, p_refs[0])
        values(N_K_FAST - 1, p_refs[1])
    else:
        values(N_K_FAST - 1, p_refs[0])
    return l


def _attend_lat_exact(k_ref, v_chunk, weights, acc_ref, s_refs, p_refs):
    maps = range(len(weights))

    def scores(c, s_ref):
        kc = k_ref[pl.ds(pl.multiple_of(c * TK, TK), TK), :]
        for j in maps:
            s_ref[j] = jnp.dot(kc, weights[j], preferred_element_type=F32)

    def softmax(s_ref, p_ref, m, l):
        m_new, l_new, alpha = [], [], []
        for j in maps:
            m_j = jnp.maximum(m[j], jnp.max(s_ref[j], axis=0, keepdims=True))
            a_j = jnp.exp2(m[j] - m_j)
            p = jnp.exp2(s_ref[j] - m_j)
            l_new.append(a_j * l[j] + jnp.sum(p, axis=0, keepdims=True))
            p_ref[j, 0:TK, :] = p.astype(BF16)
            m_new.append(m_j)
            alpha.append(a_j)
        return m_new, l_new, alpha

    def values(c, p_ref, alpha):
        vc = v_chunk(pl.multiple_of(c * TK, TK), TK)
        for j in maps:
            acc_ref[j] = alpha[j] * acc_ref[j] + jnp.dot(vc, p_ref[j, 0:TK, :],
                                                         preferred_element_type=F32)

    acc_ref[...] = jnp.zeros_like(acc_ref)
    p_refs[1][...] = jnp.zeros_like(p_refs[1])
    scores(0, s_refs[0])
    stat = lambda v: [jnp.full((1, TQ), v, F32) for _ in maps]

    def chunk_pair(i, carry):
        m, l, alpha_b = carry
        c0 = 2 * i
        scores(c0 + 1, s_refs[1])
        m, l, alpha_a = softmax(s_refs[0], p_refs[0], m, l)
        values(jnp.maximum(c0 - 1, 0), p_refs[1], alpha_b)
        scores(jnp.minimum(c0 + 2, N_K_CHUNKS - 1), s_refs[0])
        m, l, alpha_b = softmax(s_refs[1], p_refs[1], m, l)
        values(c0, p_refs[0], alpha_a)
        return m, l, alpha_b

    _, l, alpha_b = lax.fori_loop(0, N_K_CHUNKS // 2, chunk_pair,
                                  (stat(NEG_BIG), stat(0.0), stat(1.0)))
    values(N_K_CHUNKS - 1, p_refs[1], alpha_b)
    return l


def _attend(ctx_only, k_ref, vt_at, weights, halves, scratch):
    if ctx_only:
        return _attend_ctx(k_ref[...], vt_at(0, CTX_LEN), weights)
    acc_ref, l_ref, kmax_ref, s0_ref, s1_ref, p0_ref, p1_ref = scratch
    maps = range(len(weights))

    @pl.when(pl.program_id(1) == 0)
    def _():
        _key_norm_bound(k_ref, kmax_ref)

    lane = lax.broadcasted_iota(jnp.int32, (1, LANES), 1)
    shift = []
    for j in maps:
        k2 = jnp.max(jnp.where(lane // HEAD_DIM == halves[j], kmax_ref[0:1, :], 0.0),
                     axis=-1, keepdims=True)
        wf = weights[j].astype(F32)
        q2 = jnp.sum(wf * wf, axis=0, keepdims=True)
        shift.append(jnp.sqrt(q2 * k2) * BOUND_SLACK - SHIFT_GUARD)

    l = _attend_lat_fast(k_ref, vt_at, weights, shift, acc_ref, (p0_ref, p1_ref))
    n_bad = jnp.zeros((), F32)
    for j in maps:
        l_ref[j:j + 1, :] = l[j]
        ok = jnp.logical_and(l[j] >= L_MIN, l[j] <= L_MAX)
        n_bad = n_bad + jnp.sum(jnp.where(ok, 0.0, 1.0))

    @pl.when(n_bad > 0.0)
    def _():
        l = _attend_lat_exact(k_ref, vt_at, weights, acc_ref, (s0_ref, s1_ref), (p0_ref, p1_ref))
        for j in maps:
            l_ref[j:j + 1, :] = l[j]

    return [acc_ref[j] * (1.0 / l_ref[j:j + 1, :]) for j in maps]


def _attn_call(kernel, ctx_only, n_heads, dv, in_blocks, operands, out_width, name):
    if ctx_only:
        tq, n_q, n_keys, q0, k0 = CTX_LEN, 1, CTX_LEN, SEQ // CTX_LEN, SEQ // CTX_LEN
        scratch = []
    else:
        tq, n_q, n_keys, q0, k0 = TQ, SEQ // TQ, T_ALL, 0, 0
        scratch = [pltpu.VMEM((2, dv, TQ), F32), pltpu.VMEM((8, TQ), F32),
                   pltpu.VMEM((8, LANES), F32),
                   pltpu.VMEM((2, TK, TQ), F32), pltpu.VMEM((2, TK, TQ), F32),
                   pltpu.VMEM((2, TK_FAST, TQ), BF16), pltpu.VMEM((2, TK_FAST, TQ), BF16)]
    return pl.pallas_call(
        functools.partial(kernel, ctx_only),
        grid=(n_heads, n_q),
        in_specs=in_blocks(tq, n_keys, q0, k0),
        out_specs=pl.BlockSpec((tq, LANES), lambda h, i: (i, h)),
        out_shape=jax.ShapeDtypeStruct((n_q * tq, out_width), BF16),
        scratch_shapes=scratch,
        compiler_params=_params(2),
        name=name + ("_ctx" if ctx_only else ""),
    )(*operands)


def _da_kernel(lam_init, ctx_only, qt_ref, k_ref, vt_ref, lam_ref, g_ref, o_ref, *scratch):
    qt = qt_ref[0]
    row = lax.broadcasted_iota(jnp.int32, qt.shape, 0)
    zero = jnp.zeros_like(qt)
    weights = [jnp.where(row < HEAD_DIM, qt, zero), jnp.where(row >= HEAD_DIM, qt, zero)]
    o1, o2 = _attend(ctx_only, k_ref, lambda s, n: vt_ref[0, :, pl.ds(s, n)], weights, (0, 1),
                     scratch)
    lf = lam_ref[...]
    lam = (jnp.exp(jnp.sum(lf[0:1] * lf[1:2], axis=-1, keepdims=True))
           - jnp.exp(jnp.sum(lf[2:3] * lf[3:4], axis=-1, keepdims=True)) + lam_init)
    o = o1 - lam * o2
    ms = jnp.mean(o * o, axis=0, keepdims=True)
    on = (o * lax.rsqrt(ms + EPS)) * g_ref[...] * (1.0 - lam_init)
    o_ref[...] = on.T.astype(o_ref.dtype)


def _da_attention(qt, k, vt, da_lam, subln_g, lam_init, ctx_only):
    def in_blocks(tq, n_keys, q0, k0):
        return [
            pl.BlockSpec((1, LANES, tq), lambda h, i: (h, 0, i + q0)),
            pl.BlockSpec((n_keys, LANES), lambda h, i: (k0, h)),
            pl.BlockSpec((1, LANES, n_keys), lambda h, i: (h, 0, k0)),
            pl.BlockSpec((4, HEAD_DIM), lambda h, i: (0, 0)),
            pl.BlockSpec((LANES, 1), lambda h, i: (0, 0)),
        ]
    return _attn_call(functools.partial(_da_kernel, lam_init), ctx_only, DA_HEADS, LANES,
                      in_blocks, (qt, k, vt, da_lam, subln_g), DA_HEADS * LANES, "da_attention")


def _gqa_kernel(ctx_only, qt_ref, k_ref, vt_ref, o_ref, *scratch):
    g = pl.program_id(0)
    qt = qt_ref[0]
    zero = jnp.zeros((HEAD_DIM, qt.shape[1]), qt.dtype)
    weights = []
    for j in range(2):
        qj = qt[j * HEAD_DIM:(j + 1) * HEAD_DIM]
        weights.append(jnp.where(g == 0, jnp.concatenate([qj, zero], axis=0),
                                 jnp.concatenate([zero, qj], axis=0)))
    o0, o1 = _attend(ctx_only, k_ref, lambda s, n: vt_ref[:, pl.ds(s, n)], weights, (g, g),
                     scratch)
    o_ref[...] = jnp.concatenate([o0, o1], axis=0).T.astype(o_ref.dtype)


def _gqa_attention(qt, k, vt, ctx_only):
    def in_blocks(tq, n_keys, q0, k0):
        return [
            pl.BlockSpec((1, LANES, tq), lambda g, i: (g, 0, i + q0)),
            pl.BlockSpec((n_keys, LANES), lambda g, i: (k0, 0)),
            pl.BlockSpec((HEAD_DIM, n_keys), lambda g, i: (g, k0)),
        ]
    return _attn_call(_gqa_kernel, ctx_only, GQA_KV_HEADS, HEAD_DIM, in_blocks, (qt, k, vt),
                      GQA_KV_HEADS * LANES, "gqa_attention")


def _rope_tables():
    half = HEAD_DIM // 4
    t = jnp.arange(SEQ, dtype=jnp.int32)
    row = (t // GRID_W).astype(F32)
    col = (t % GRID_W).astype(F32)
    inv = ROPE_THETA ** (-jnp.arange(half, dtype=F32) / half)
    ang_r = row[:, None] * inv
    ang_c = col[:, None] * inv
    cos64 = jnp.concatenate([jnp.cos(ang_r)] * 2 + [jnp.cos(ang_c)] * 2, axis=-1)
    sin64 = jnp.concatenate([-jnp.sin(ang_r), jnp.sin(ang_r), -jnp.sin(ang_c), jnp.sin(ang_c)],
                            axis=-1)
    n_rest = T_PAD - SEQ
    cos_t = jnp.concatenate([jnp.tile(cos64, (1, 2)), jnp.ones((n_rest, LANES), F32)], axis=0)
    sin_t = jnp.concatenate([jnp.tile(sin64, (1, 2)), jnp.zeros((n_rest, LANES), F32)], axis=0)
    return cos_t, sin_t


def _block_diag(w):
    eye = jnp.eye(LRU_BLOCKS, dtype=w.dtype)
    full = jnp.einsum('dkij,kl->dkilj', w, eye)
    return full.reshape(2, LRU_WIDTH, LRU_WIDTH).astype(BF16)


def kernel(x, c, ctx, c_ctx, ada_w, ada_b, norm_g, ffn1_w13, ffn1_w2, ffn2_w13, ffn2_w2,
           w_in, w_out, da_lam, da_subln_g, qk_norm_g, lru_conv_w, lru_conv_b,
           lru_wa, lru_ba, lru_wi, lru_bi, lru_lambda, final_g):
    assert x.shape == (1, SEQ, D_MODEL) and ctx.shape == (1, CTX_LEN, D_MODEL)
    cc = jnp.zeros((8, D_MODEL), F32).at[0].set(c[0]).at[1].set(c_ctx)
    mods_all = _ada_mods(cc, ada_w, ada_b).reshape(DEPTH, 8, N_MOD, D_MODEL)[:, :2]
    cos_t, sin_t = _rope_tables()
    ffn1_w13, ffn1_w2, ffn2_w13, ffn2_w2, w_in, w_out = (
        w.astype(BF16) for w in (ffn1_w13, ffn1_w2, ffn2_w13, ffn2_w2, w_in, w_out))

    def stream(lat, ctx_rows):
        rest = jnp.zeros((T_PAD - T_ALL, lat.shape[1]), lat.dtype)
        return jnp.concatenate([lat, ctx_rows, rest], axis=0)

    h = stream(x[0], ctx[0])
    for l in range(DEPTH):
        last = l == DEPTH - 1
        lam_init = 0.8 - 0.6 * math.exp(-0.3 * l)
        mods = mods_all[l]

        h = _ffn(h, mods, norm_g[l, 0:1], ffn1_w13, ffn1_w2, l, 0, N_TILES)

        qk_gain = jnp.tile(qk_norm_g[l], (1, 2))
        qt_da, k_da, vt_da, qt_g, k_g, vt_g, xb, gb = _inproj(
            h, mods, norm_g[l, 1:2], w_in, l, cos_t, sin_t, qk_gain)

        hs = _lru(xb, lru_conv_w[l], lru_conv_b[l][None, :], _block_diag(lru_wa[l]),
                  _block_diag(lru_wi[l]), lru_ba[l][:, None, :], lru_bi[l][:, None, :],
                  lru_lambda[l][:, None, :])

        n_tiles = N_LAT_TILES if last else N_TILES
        da_args = (qt_da, k_da, vt_da, da_lam[l], da_subln_g[l][:, None], lam_init)
        a = _da_attention(*da_args, False)
        g = _gqa_attention(qt_g, k_g, vt_g, False)
        if not last:
            a = stream(a, _da_attention(*da_args, True))
            g = stream(g, _gqa_attention(qt_g, k_g, vt_g, True))

        h = _ffn(h, mods, norm_g[l, 2:3], ffn2_w13, ffn2_w2, l, 6, n_tiles,
                 mixer=(a, g, hs, gb, w_out), final_g=final_g[None, :] if last else None)
    return h[None]
```

```python
import functools
import math

import jax
import jax.numpy as jnp
from jax import lax
from jax.experimental import pallas as pl
from jax.experimental.pallas import tpu as pltpu

D_MODEL = 1024
SEQ = 16384
CTX_LEN = 256
T_ALL = SEQ + CTX_LEN
DEPTH = 4
GRID_W = 64
HEAD_DIM = 64
SCALE = HEAD_DIM ** -0.5
LOG2E = math.log2(math.e)
ROPE_THETA = 10000.0
DA_HEADS = 4
GQA_KV_HEADS = 2
LRU_WIDTH = 256
LRU_BLOCKS = 4
LRU_BLOCK = LRU_WIDTH // LRU_BLOCKS
LRU_C = 8.0
IN_WIDTH = 2560
D_FF = 2816
N_MOD = 9
EPS = 1e-6

LANES = 128
MXU_DIM = 256

TM = 512
T_PAD = SEQ + TM
N_TILES = T_PAD // TM
N_LAT_TILES = SEQ // TM
TL = 256
N_LRU_LAT = SEQ // TL
FF_CHUNK = MXU_DIM
N_FF_CHUNKS = D_FF // FF_CHUNK
TQ = 1024
TK = 640
N_K_CHUNKS = T_ALL // TK
TK_FAST = 1280
N_K_FAST = T_ALL // TK_FAST
NEG_BIG = -1e30
SHIFT_GUARD = 32.0
BOUND_SLACK = 1.02
L_MIN = 2.0 ** -60
L_MAX = 2.0 ** 100
VMEM_LIMIT = 48 * 1024 * 1024

F32 = jnp.float32
BF16 = jnp.bfloat16


def _params(n_axes):
    return pltpu.CompilerParams(dimension_semantics=("arbitrary",) * n_axes,
                                vmem_limit_bytes=VMEM_LIMIT)


def _resident(shape):
    zeros = (0,) * len(shape)
    return pl.BlockSpec(shape, lambda *_: zeros, pipeline_mode=pl.Buffered(1))


def _sigmoid(x):
    return 1.0 / (1.0 + jnp.exp(-x))


def _norm_mod(x, g, shift, scale):
    ms = jnp.mean(x * x, axis=-1, keepdims=True)
    y = (x * lax.rsqrt(ms + EPS)) * g
    return y * (1.0 + scale) + shift


ADA_COLS = 1152


def _ada_kernel(c_ref, w_ref, b_ref, o_ref):
    c = c_ref[...]
    s = (c * _sigmoid(c)).astype(BF16)
    o_ref[0] = jnp.dot(s, w_ref[0].astype(BF16), preferred_element_type=F32) + b_ref[0]


def _ada_mods(cc, ada_w, ada_b):
    width = N_MOD * D_MODEL
    return pl.pallas_call(
        _ada_kernel,
        grid=(DEPTH, width // ADA_COLS),
        in_specs=[
            pl.BlockSpec((8, D_MODEL), lambda l, j: (0, 0)),
            pl.BlockSpec((1, D_MODEL, ADA_COLS), lambda l, j: (l, 0, j)),
            pl.BlockSpec((1, 1, ADA_COLS), lambda l, j: (l, 0, j)),
        ],
        out_specs=pl.BlockSpec((1, 8, ADA_COLS), lambda l, j: (l, 0, j)),
        out_shape=jax.ShapeDtypeStruct((DEPTH, 8, width), F32),
        compiler_params=_params(2),
        name="ada_mods",
    )(cc, ada_w, ada_b.reshape(DEPTH, 1, width))


def _gelu_tanh(x):
    return 0.5 * x * (1.0 + jnp.tanh(math.sqrt(2.0 / math.pi) * (x + 0.044715 * (x * x * x))))


def _mixer_out(a_ref, g_ref, hs_ref, gb_ref, w_ref):
    r = ((hs_ref[0] + hs_ref[1]) * _gelu_tanh(gb_ref[...])).astype(BF16)
    n_a = DA_HEADS * LANES
    n_g = n_a + GQA_KV_HEADS * LANES
    return (jnp.dot(a_ref[...], w_ref[0, 0:n_a, :], preferred_element_type=F32)
            + jnp.dot(g_ref[...], w_ref[0, n_a:n_g, :], preferred_element_type=F32)
            + jnp.dot(r, w_ref[0, n_g:, :], preferred_element_type=F32))


def _ffn_kernel(i0, with_mixer, with_final_norm, h_ref, mod_ref, g_ref, w13_ref, w2_ref, *rest):
    rest = list(rest)
    mixer_refs = [rest.pop(0) for _ in range(5)] if with_mixer else None
    final_g_ref = rest.pop(0) if with_final_norm else None
    o_ref, acc_ref, gate_ref = rest
    x = h_ref[...]
    mods = mod_ref[0]
    if with_mixer:
        x = x + mods[5:6] * _mixer_out(*mixer_refs)
    z = _norm_mod(x, g_ref[...], mods[i0:i0 + 1], mods[i0 + 1:i0 + 2]).astype(BF16)
    acc_ref[...] = jnp.zeros_like(acc_ref)

    def gate(c):
        lo = c * FF_CHUNK
        a = jnp.dot(z, w13_ref[0, :, lo:lo + FF_CHUNK], preferred_element_type=F32)
        b = jnp.dot(z, w13_ref[0, :, D_FF + lo:D_FF + lo + FF_CHUNK], preferred_element_type=F32)
        return (a * _sigmoid(a) * b).astype(BF16)

    def down(c):
        acc_ref[...] += jnp.dot(gate_ref[...], w2_ref[0, c * FF_CHUNK:(c + 1) * FF_CHUNK, :],
                                preferred_element_type=F32)

    gate_ref[...] = gate(0)
    for c in range(1, N_FF_CHUNKS):
        down(c - 1)
        gate_ref[...] = gate(c)
    down(N_FF_CHUNKS - 1)
    y = x + (0.5 * mods[i0 + 2:i0 + 3]) * acc_ref[...]
    if with_final_norm:
        y = (y * lax.rsqrt(jnp.mean(y * y, axis=-1, keepdims=True) + EPS)) * final_g_ref[...]
    o_ref[...] = y


def _layer_resident(shape, layer):
    index = (layer,) + (0,) * len(shape)
    return pl.BlockSpec((1,) + shape, lambda *_: index, pipeline_mode=pl.Buffered(1))


def _ffn(h, mods, g, w13, w2, layer, i0, n_tiles, mixer=None, final_g=None):
    row_tile = lambda width: pl.BlockSpec((TM, width), lambda i: (i, 0))
    vector = pl.BlockSpec((1, D_MODEL), lambda i: (0, 0))
    in_specs = [
        row_tile(D_MODEL),
        pl.BlockSpec((1, N_MOD, D_MODEL), lambda i: (i // N_LAT_TILES, 0, 0)),
        vector,
        _layer_resident((D_MODEL, 2 * D_FF), layer),
        _layer_resident((D_FF, D_MODEL), layer),
    ]
    operands = [h, mods, g, w13, w2]
    if mixer is not None:
        in_specs += [row_tile(DA_HEADS * LANES), row_tile(GQA_KV_HEADS * LANES),
                     pl.BlockSpec((2, TM, LRU_WIDTH), lambda i: (0, i, 0)), row_tile(LRU_WIDTH),
                     _layer_resident((D_MODEL, D_MODEL), layer)]
        operands += list(mixer)
    if final_g is not None:
        in_specs.append(vector)
        operands.append(final_g)
    return pl.pallas_call(
        functools.partial(_ffn_kernel, i0, mixer is not None, final_g is not None),
        grid=(n_tiles,),
        in_specs=in_specs,
        out_specs=row_tile(D_MODEL),
        out_shape=jax.ShapeDtypeStruct((n_tiles * TM, D_MODEL), F32),
        scratch_shapes=[pltpu.VMEM((TM, D_MODEL), F32), pltpu.VMEM((TM, FF_CHUNK), BF16)],
        compiler_params=_params(1),
        name="ffn",
    )(*operands)


def _swap16(x):
    lane = lax.broadcasted_iota(jnp.int32, x.shape, 1)
    return jnp.where(lane % 32 < 16, pltpu.roll(x, LANES - 16, 1), pltpu.roll(x, 16, 1))


def _rope(x, cos, sin):
    return x * cos + _swap16(x) * sin


def _head_mean_sq(x):
    sq = x * x
    hi = sq.astype(BF16)
    lo = (sq - hi.astype(F32)).astype(BF16)
    r = lax.broadcasted_iota(jnp.int32, (LANES, LANES), 0) // HEAD_DIM
    c = lax.broadcasted_iota(jnp.int32, (LANES, LANES), 1) // HEAD_DIM
    ones = jnp.where(r == c, 1.0, 0.0).astype(BF16)
    tot = (jnp.dot(hi, ones, preferred_element_type=F32)
           + jnp.dot(lo, ones, preferred_element_type=F32))
    return tot * (1.0 / HEAD_DIM)


def _inproj_kernel(h_ref, mod_ref, g_ref, w_ref, cos_ref, sin_ref, qkg_ref,
                   qt_da_ref, k_da_ref, vt_da_ref, qt_g_ref, k_g_ref, vt_g_ref, xb_ref, gb_ref):
    mods = mod_ref[0]
    z = _norm_mod(h_ref[...], g_ref[...], mods[3:4], mods[4:5]).astype(BF16)
    proj = jnp.dot(z, w_ref[0], preferred_element_type=F32)
    cos = cos_ref[...]
    sin = sin_ref[...]
    q_scale = SCALE * LOG2E

    def slab(j):
        return proj[:, j * LANES:(j + 1) * LANES]

    for h in range(DA_HEADS):
        q = _rope(slab(h), cos, sin) * q_scale
        qt_da_ref[h] = q.T.astype(BF16)
        k_da_ref[:, h * LANES:(h + 1) * LANES] = _rope(slab(4 + h), cos, sin).astype(BF16)
        vt_da_ref[h] = slab(8 + h).T.astype(BF16)

    gq_gain = qkg_ref[0:1, :]
    gk_gain = qkg_ref[1:2, :]
    for g in range(GQA_KV_HEADS):
        q = slab(12 + g)
        q = (q * lax.rsqrt(_head_mean_sq(q) + EPS)) * gq_gain
        qt_g_ref[g] = (_rope(q, cos, sin) * q_scale).T.astype(BF16)
    k = slab(14)
    k = (k * lax.rsqrt(_head_mean_sq(k) + EPS)) * gk_gain
    k_g_ref[...] = _rope(k, cos, sin).astype(BF16)
    vt_g_ref[...] = slab(15).T.astype(BF16)
    xb_ref[...] = proj[:, 2048:2304]
    gb_ref[...] = proj[:, 2304:2560]


def _inproj(h, mods, g, w_in, layer, cos_t, sin_t, qk_gain):
    row_tile = lambda width: pl.BlockSpec((TM, width), lambda i: (i, 0))
    return pl.pallas_call(
        _inproj_kernel,
        grid=(N_TILES,),
        in_specs=[
            row_tile(D_MODEL),
            pl.BlockSpec((1, N_MOD, D_MODEL), lambda i: (i // N_LAT_TILES, 0, 0)),
            pl.BlockSpec((1, D_MODEL), lambda i: (0, 0)),
            _layer_resident((D_MODEL, IN_WIDTH), layer),
            row_tile(LANES),
            row_tile(LANES),
            pl.BlockSpec((2, LANES), lambda i: (0, 0)),
        ],
        out_specs=[
            pl.BlockSpec((DA_HEADS, LANES, TM), lambda i: (0, 0, i)),
            row_tile(DA_HEADS * LANES),
            pl.BlockSpec((DA_HEADS, LANES, TM), lambda i: (0, 0, i)),
            pl.BlockSpec((GQA_KV_HEADS, LANES, TM), lambda i: (0, 0, i)),
            row_tile(LANES),
            pl.BlockSpec((LANES, TM), lambda i: (0, i)),
            row_tile(LRU_WIDTH),
            row_tile(LRU_WIDTH),
        ],
        out_shape=[
            jax.ShapeDtypeStruct((DA_HEADS, LANES, T_PAD), BF16),
            jax.ShapeDtypeStruct((T_PAD, DA_HEADS * LANES), BF16),
            jax.ShapeDtypeStruct((DA_HEADS, LANES, T_PAD), BF16),
            jax.ShapeDtypeStruct((GQA_KV_HEADS, LANES, T_PAD), BF16),
            jax.ShapeDtypeStruct((T_PAD, LANES), BF16),
            jax.ShapeDtypeStruct((LANES, T_PAD), BF16),
            jax.ShapeDtypeStruct((T_PAD, LRU_WIDTH), F32),
            jax.ShapeDtypeStruct((T_PAD, LRU_WIDTH), F32),
        ],
        compiler_params=_params(1),
        name="inproj",
    )(h, mods, g, w_in, cos_t, sin_t, qk_gain)


N_LRU_STEPS = N_LRU_LAT + 2


def _lru_tile_index(d, j):
    lat = jnp.where(d == 0, j - 1, N_LRU_LAT - j)
    return jnp.where(j == 0, N_LRU_LAT, jnp.where(j == N_LRU_STEPS - 1, N_LRU_LAT + 1, lat))


def _lru_kernel(x_ref, cw_ref, cb_ref, wa_ref, wi_ref, ba_ref, bi_ref, lam_ref, o_ref, carry_ref):
    d = pl.program_id(0)
    j = pl.program_id(1)
    ti = _lru_tile_index(d, j)
    t0 = pl.multiple_of(ti * TL, TL)

    @pl.when(j == 0)
    def _():
        carry_ref[...] = jnp.zeros_like(carry_ref)

    x = x_ref[pl.ds(t0, TL), :]
    prev8 = x_ref[pl.ds(pl.multiple_of(jnp.maximum(t0 - 8, 0), 8), 8), :]
    next8 = x_ref[pl.ds(pl.multiple_of(jnp.minimum(t0 + TL, T_PAD - 8), 8), 8), :]
    has_prev = jnp.logical_and(ti != 0, ti < N_LRU_LAT)
    has_next = ti < N_LRU_LAT - 1
    prev_row = jnp.where(has_prev, prev8[7:8], 0.0)
    next_row0 = jnp.where(has_next, next8[0:1], 0.0)
    next_row1 = jnp.where(has_next, next8[1:2], 0.0)
    row = lax.broadcasted_iota(jnp.int32, (TL, LRU_WIDTH), 0)
    xm1 = jnp.where(row == 0, prev_row, pltpu.roll(x, 1, 0))
    xp1 = jnp.where(row == TL - 1, next_row0, pltpu.roll(x, TL - 1, 0))
    xp2 = jnp.where(row == TL - 2, next_row0,
                    jnp.where(row == TL - 1, next_row1, pltpu.roll(x, TL - 2, 0)))
    cw = cw_ref[...]
    xc = xm1 * cw[0:1] + x * cw[1:2] + xp1 * cw[2:3] + xp2 * cw[3:4] + cb_ref[...]

    xcb = xc.astype(BF16)
    r = _sigmoid(jnp.dot(xcb, wa_ref[0], preferred_element_type=F32) + ba_ref[0])
    gi = _sigmoid(jnp.dot(xcb, wi_ref[0], preferred_element_type=F32) + bi_ref[0])
    neg_lam = -lam_ref[0]
    softplus = jnp.maximum(neg_lam, 0.0) + jnp.log1p(jnp.exp(-jnp.abs(neg_lam)))
    log_a = (-LRU_C) * r * softplus
    a = jnp.exp(log_a)
    u = jnp.sqrt(-jnp.tanh(log_a) * (a * a + 1.0)) * (gi * xc)

    def scan(A, B, forward):
        k = 1
        while k < TL:
            if forward:
                keep = row >= k
                As = jnp.where(keep, pltpu.roll(A, k, 0), 1.0)
                Bs = jnp.where(keep, pltpu.roll(B, k, 0), 0.0)
            else:
                keep = row < TL - k
                As = jnp.where(keep, pltpu.roll(A, TL - k, 0), 1.0)
                Bs = jnp.where(keep, pltpu.roll(B, TL - k, 0), 0.0)
            B = A * Bs + B
            A = A * As
            k *= 2
        return A, B

    carry = carry_ref[0:1, :]

    @pl.when(d == 0)
    def _():
        A, B = scan(a, u, True)
        hs = B + A * carry
        o_ref[0] = hs
        carry_ref[0:1, :] = hs[TL - 1:TL]

    @pl.when(d == 1)
    def _():
        A, B = scan(a, u, False)
        hs = B + A * carry
        o_ref[0] = hs
        carry_ref[0:1, :] = hs[0:1]


def _lru(xb, conv_w, conv_b, wa, wi, ba, bi, lam):
    per_dir = lambda *tail: pl.BlockSpec((1,) + tail, lambda d, j: (d,) + (0,) * len(tail))
    return pl.pallas_call(
        _lru_kernel,
        grid=(2, N_LRU_STEPS),
        in_specs=[
            _resident((T_PAD, LRU_WIDTH)),
            pl.BlockSpec((4, LRU_WIDTH), lambda d, j: (0, 0)),
            pl.BlockSpec((1, LRU_WIDTH), lambda d, j: (0, 0)),
            per_dir(LRU_WIDTH, LRU_WIDTH),
            per_dir(LRU_WIDTH, LRU_WIDTH),
            per_dir(1, LRU_WIDTH),
            per_dir(1, LRU_WIDTH),
            per_dir(1, LRU_WIDTH),
        ],
        out_specs=pl.BlockSpec((1, TL, LRU_WIDTH), lambda d, j: (d, _lru_tile_index(d, j), 0)),
        out_shape=jax.ShapeDtypeStruct((2, T_PAD, LRU_WIDTH), F32),
        scratch_shapes=[pltpu.VMEM((8, LRU_WIDTH), F32)],
        compiler_params=_params(2),
        name="rglru",
    )(xb, conv_w, conv_b, wa, wi, ba, bi, lam)


def _softmax_chunk(s, m_old, l_old):
    m_new = jnp.maximum(m_old, jnp.max(s, axis=0, keepdims=True))
    alpha = jnp.exp2(m_old - m_new)
    p = jnp.exp2(s - m_new)
    l_new = alpha * l_old + jnp.sum(p, axis=0, keepdims=True)
    return p.astype(BF16), m_new, l_new, alpha


def _attend_ctx(kc, vc, weights):
    outs = []
    for w in weights:
        s = jnp.dot(kc, w, preferred_element_type=F32)
        p, _, l, _ = _softmax_chunk(s, jnp.full((1, CTX_LEN), NEG_BIG, F32),
                                    jnp.zeros((1, CTX_LEN), F32))
        outs.append(jnp.dot(vc, p, preferred_element_type=F32) * (1.0 / l))
    return outs


def _key_norm_bound(k_ref, kmax_ref):
    r = lax.broadcasted_iota(jnp.int32, (LANES, LANES), 0) // HEAD_DIM
    c = lax.broadcasted_iota(jnp.int32, (LANES, LANES), 1) // HEAD_DIM
    ones = jnp.where(r == c, 1.0, 0.0).astype(BF16)

    def chunk(i, best):
        kf = k_ref[pl.ds(pl.multiple_of(i * TK, TK), TK), :].astype(F32)
        n2 = jnp.dot((kf * kf).astype(BF16), ones, preferred_element_type=F32)
        return jnp.maximum(best, jnp.max(n2, axis=0, keepdims=True))

    kmax_ref[0:1, :] = lax.fori_loop(0, N_K_CHUNKS, chunk, jnp.zeros((1, LANES), F32))


def _attend_lat_fast(k_ref, v_chunk, weights, shift, acc_ref, p_refs):
    maps = range(len(weights))

    def scores_exp(c, p_ref, l):
        kc = k_ref[pl.ds(pl.multiple_of(c * TK_FAST, TK_FAST), TK_FAST), :]
        l_new = []
        for j in maps:
            p = jnp.exp2(jnp.dot(kc, weights[j], preferred_element_type=F32) - shift[j])
            l_new.append(l[j] + jnp.sum(p, axis=0, keepdims=True))
            p_ref[j] = p.astype(BF16)
        return tuple(l_new)

    def values(c, p_ref):
        vc = v_chunk(pl.multiple_of(c * TK_FAST, TK_FAST), TK_FAST)
        for j in maps:
            acc_ref[j] += jnp.dot(vc, p_ref[j], preferred_element_type=F32)

    acc_ref[...] = jnp.zeros_like(acc_ref)
    l = scores_exp(0, p_refs[0], tuple(jnp.zeros((1, TQ), F32) for _ in maps))

    def chunk_pair(i, l):
        c0 = 2 * i
        l = scores_exp(c0 + 1, p_refs[1], l)
        values(c0, p_refs[0])
        l = scores_exp(c0 + 2, p_refs[0], l)
        values(c0 + 1, p_refs[1])
        return l

    n_pairs = (N_K_FAST - 1) // 2
    l = lax.fori_loop(0, n_pairs, chunk_pair, l, unroll=3)
    if N_K_FAST % 2 == 0:
        l = scores_exp(N_K_FAST - 1, p_refs[1], l)
        values(N_K_FAST - 2, p_refs[0])
        values(N_K_FAST - 1, p_refs[1])
    else:
        values(N_K_FAST - 1, p_refs[0])
    return l


def _attend_lat_exact(k_ref, v_chunk, weights, acc_ref, s_refs, p_refs):
    maps = range(len(weights))

    def scores(c, s_ref):
        kc = k_ref[pl.ds(pl.multiple_of(c * TK, TK), TK), :]
        for j in maps:
            s_ref[j] = jnp.dot(kc, weights[j], preferred_element_type=F32)

    def softmax(s_ref, p_ref, m, l):
        m_new, l_new, alpha = [], [], []
        for j in maps:
            m_j = jnp.maximum(m[j], jnp.max(s_ref[j], axis=0, keepdims=True))
            a_j = jnp.exp2(m[j] - m_j)
            p = jnp.exp2(s_ref[j] - m_j)
            l_new.append(a_j * l[j] + jnp.sum(p, axis=0, keepdims=True))
            p_ref[j, 0:TK, :] = p.astype(BF16)
            m_new.append(m_j)
            alpha.append(a_j)
        return m_new, l_new, alpha

    def values(c, p_ref, alpha):
        vc = v_chunk(pl.multiple_of(c * TK, TK), TK)
        for j in maps:
            acc_ref[j] = alpha[j] * acc_ref[j] + jnp.dot(vc, p_ref[j, 0:TK, :],
                                                         preferred_element_type=F32)

    acc_ref[...] = jnp.zeros_like(acc_ref)
    p_refs[1][...] = jnp.zeros_like(p_refs[1])
    scores(0, s_refs[0])
    stat = lambda v: [jnp.full((1, TQ), v, F32) for _ in maps]

    def chunk_pair(i, carry):
        m, l, alpha_b = carry
        c0 = 2 * i
        scores(c0 + 1, s_refs[1])
        m, l, alpha_a = softmax(s_refs[0], p_refs[0], m, l)
        values(jnp.maximum(c0 - 1, 0), p_refs[1], alpha_b)
        scores(jnp.minimum(c0 + 2, N_K_CHUNKS - 1), s_refs[0])
        m, l, alpha_b = softmax(s_refs[1], p_refs[1], m, l)
        values(c0, p_refs[0], alpha_a)
        return m, l, alpha_b

    _, l, alpha_b = lax.fori_loop(0, N_K_CHUNKS // 2, chunk_pair,
                                  (stat(NEG_BIG), stat(0.0), stat(1.0)))
    values(N_K_CHUNKS - 1, p_refs[1], alpha_b)
    return l


def _attend(ctx_only, k_ref, vt_at, weights, halves, scratch):
    if ctx_only:
        return _attend_ctx(k_ref[...], vt_at(0, CTX_LEN), weights)
    acc_ref, l_ref, kmax_ref, s0_ref, s1_ref, p0_ref, p1_ref = scratch
    maps = range(len(weights))

    @pl.when(pl.program_id(1) == 0)
    def _():
        _key_norm_bound(k_ref, kmax_ref)

    lane = lax.broadcasted_iota(jnp.int32, (1, LANES), 1)
    shift = []
    for j in maps:
        k2 = jnp.max(jnp.where(lane // HEAD_DIM == halves[j], kmax_ref[0:1, :], 0.0),
                     axis=-1, keepdims=True)
        wf = weights[j].astype(F32)
        q2 = jnp.sum(wf * wf, axis=0, keepdims=True)
        shift.append(jnp.sqrt(q2 * k2) * BOUND_SLACK - SHIFT_GUARD)

    l = _attend_lat_fast(k_ref, vt_at, weights, shift, acc_ref, (p0_ref, p1_ref))
    n_bad = jnp.zeros((), F32)
    for j in maps:
        l_ref[j:j + 1, :] = l[j]
        ok = jnp.logical_and(l[j] >= L_MIN, l[j] <= L_MAX)
        n_bad = n_bad + jnp.sum(jnp.where(ok, 0.0, 1.0))

    @pl.when(n_bad > 0.0)
    def _():
        l = _attend_lat_exact(k_ref, vt_at, weights, acc_ref, (s0_ref, s1_ref), (p0_ref, p1_ref))
        for j in maps:
            l_ref[j:j + 1, :] = l[j]

    return [acc_ref[j] * (1.0 / l_ref[j:j + 1, :]) for j in maps]


def _attn_call(kernel, ctx_only, n_heads, dv, in_blocks, operands, out_width, name):
    if ctx_only:
        tq, n_q, n_keys, q0, k0 = CTX_LEN, 1, CTX_LEN, SEQ // CTX_LEN, SEQ // CTX_LEN
        scratch = []
    else:
        tq, n_q, n_keys, q0, k0 = TQ, SEQ // TQ, T_ALL, 0, 0
        scratch = [pltpu.VMEM((2, dv, TQ), F32), pltpu.VMEM((8, TQ), F32),
                   pltpu.VMEM((8, LANES), F32),
                   pltpu.VMEM((2, TK, TQ), F32), pltpu.VMEM((2, TK, TQ), F32),
                   pltpu.VMEM((2, TK_FAST, TQ), BF16), pltpu.VMEM((2, TK_FAST, TQ), BF16)]
    return pl.pallas_call(
        functools.partial(kernel, ctx_only),
        grid=(n_heads, n_q),
        in_specs=in_blocks(tq, n_keys, q0, k0),
        out_specs=pl.BlockSpec((tq, LANES), lambda h, i: (i, h)),
        out_shape=jax.ShapeDtypeStruct((n_q * tq, out_width), BF16),
        scratch_shapes=scratch,
        compiler_params=_params(2),
        name=name + ("_ctx" if ctx_only else ""),
    )(*operands)


def _da_kernel(lam_init, ctx_only, qt_ref, k_ref, vt_ref, lam_ref, g_ref, o_ref, *scratch):
    qt = qt_ref[0]
    row = lax.broadcasted_iota(jnp.int32, qt.shape, 0)
    zero = jnp.zeros_like(qt)
    weights = [jnp.where(row < HEAD_DIM, qt, zero), jnp.where(row >= HEAD_DIM, qt, zero)]
    o1, o2 = _attend(ctx_only, k_ref, lambda s, n: vt_ref[0, :, pl.ds(s, n)], weights, (0, 1),
                     scratch)
    lf = lam_ref[...]
    lam = (jnp.exp(jnp.sum(lf[0:1] * lf[1:2], axis=-1, keepdims=True))
           - jnp.exp(jnp.sum(lf[2:3] * lf[3:4], axis=-1, keepdims=True)) + lam_init)
    o = o1 - lam * o2
    ms = jnp.mean(o * o, axis=0, keepdims=True)
    on = (o * lax.rsqrt(ms + EPS)) * g_ref[...] * (1.0 - lam_init)
    o_ref[...] = on.T.astype(o_ref.dtype)


def _da_attention(qt, k, vt, da_lam, subln_g, lam_init, ctx_only):
    def in_blocks(tq, n_keys, q0, k0):
        return [
            pl.BlockSpec((1, LANES, tq), lambda h, i: (h, 0, i + q0)),
            pl.BlockSpec((n_keys, LANES), lambda h, i: (k0, h)),
            pl.BlockSpec((1, LANES, n_keys), lambda h, i: (h, 0, k0)),
            pl.BlockSpec((4, HEAD_DIM), lambda h, i: (0, 0)),
            pl.BlockSpec((LANES, 1), lambda h, i: (0, 0)),
        ]
    return _attn_call(functools.partial(_da_kernel, lam_init), ctx_only, DA_HEADS, LANES,
                      in_blocks, (qt, k, vt, da_lam, subln_g), DA_HEADS * LANES, "da_attention")


def _gqa_kernel(ctx_only, qt_ref, k_ref, vt_ref, o_ref, *scratch):
    g = pl.program_id(0)
    qt = qt_ref[0]
    zero = jnp.zeros((HEAD_DIM, qt.shape[1]), qt.dtype)
    weights = []
    for j in range(2):
        qj = qt[j * HEAD_DIM:(j + 1) * HEAD_DIM]
        weights.append(jnp.where(g == 0, jnp.concatenate([qj, zero], axis=0),
                                 jnp.concatenate([zero, qj], axis=0)))
    def vt_at(s, n):
        v = vt_ref[:, pl.ds(s, n)]
        return jnp.concatenate([v, jnp.zeros_like(v)], axis=0)

    o0, o1 = _attend(ctx_only, k_ref, vt_at, weights, (g, g), scratch)
    o_ref[...] = jnp.concatenate([o0[0:HEAD_DIM], o1[0:HEAD_DIM]], axis=0).T.astype(o_ref.dtype)


def _gqa_attention(qt, k, vt, ctx_only):
    def in_blocks(tq, n_keys, q0, k0):
        return [
            pl.BlockSpec((1, LANES, tq), lambda g, i: (g, 0, i + q0)),
            pl.BlockSpec((n_keys, LANES), lambda g, i: (k0, 0)),
            pl.BlockSpec((HEAD_DIM, n_keys), lambda g, i: (g, k0)),
        ]
    return _attn_call(_gqa_kernel, ctx_only, GQA_KV_HEADS, LANES, in_blocks, (qt, k, vt),
                      GQA_KV_HEADS * LANES, "gqa_attention")


def _rope_tables():
    half = HEAD_DIM // 4
    t = jnp.arange(SEQ, dtype=jnp.int32)
    row = (t // GRID_W).astype(F32)
    col = (t % GRID_W).astype(F32)
    inv = ROPE_THETA ** (-jnp.arange(half, dtype=F32) / half)
    ang_r = row[:, None] * inv
    ang_c = col[:, None] * inv
    cos64 = jnp.concatenate([jnp.cos(ang_r)] * 2 + [jnp.cos(ang_c)] * 2, axis=-1)
    sin64 = jnp.concatenate([-jnp.sin(ang_r), jnp.sin(ang_r), -jnp.sin(ang_c), jnp.sin(ang_c)],
                            axis=-1)
    n_rest = T_PAD - SEQ
    cos_t = jnp.concatenate([jnp.tile(cos64, (1, 2)), jnp.ones((n_rest, LANES), F32)], axis=0)
    sin_t = jnp.concatenate([jnp.tile(sin64, (1, 2)), jnp.zeros((n_rest, LANES), F32)], axis=0)
    return cos_t, sin_t


def _block_diag(w):
    eye = jnp.eye(LRU_BLOCKS, dtype=w.dtype)
    full = jnp.einsum('dkij,kl->dkilj', w, eye)
    return full.reshape(2, LRU_WIDTH, LRU_WIDTH).astype(BF16)


def kernel(x, c, ctx, c_ctx, ada_w, ada_b, norm_g, ffn1_w13, ffn1_w2, ffn2_w13, ffn2_w2,
           w_in, w_out, da_lam, da_subln_g, qk_norm_g, lru_conv_w, lru_conv_b,
           lru_wa, lru_ba, lru_wi, lru_bi, lru_lambda, final_g):
    assert x.shape == (1, SEQ, D_MODEL) and ctx.shape == (1, CTX_LEN, D_MODEL)
    cc = jnp.zeros((8, D_MODEL), F32).at[0].set(c[0]).at[1].set(c_ctx)
    mods_all = _ada_mods(cc, ada_w, ada_b).reshape(DEPTH, 8, N_MOD, D_MODEL)[:, :2]
    cos_t, sin_t = _rope_tables()
    ffn1_w13, ffn1_w2, ffn2_w13, ffn2_w2, w_in, w_out = (
        w.astype(BF16) for w in (ffn1_w13, ffn1_w2, ffn2_w13, ffn2_w2, w_in, w_out))

    def stream(lat, ctx_rows):
        rest = jnp.zeros((T_PAD - T_ALL, lat.shape[1]), lat.dtype)
        return jnp.concatenate([lat, ctx_rows, rest], axis=0)

    h = stream(x[0], ctx[0])
    for l in range(DEPTH):
        last = l == DEPTH - 1
        lam_init = 0.8 - 0.6 * math.exp(-0.3 * l)
        mods = mods_all[l]

        h = _ffn(h, mods, norm_g[l, 0:1], ffn1_w13, ffn1_w2, l, 0, N_TILES)

        qk_gain = jnp.tile(qk_norm_g[l], (1, 2))
        qt_da, k_da, vt_da, qt_g, k_g, vt_g, xb, gb = _inproj(
            h, mods, norm_g[l, 1:2], w_in, l, cos_t, sin_t, qk_gain)

        hs = _lru(xb, lru_conv_w[l], lru_conv_b[l][None, :], _block_diag(lru_wa[l]),
                  _block_diag(lru_wi[l]), lru_ba[l][:, None, :], lru_bi[l][:, None, :],
                  lru_lambda[l][:, None, :])

        n_tiles = N_LAT_TILES if last else N_TILES
        da_args = (qt_da, k_da, vt_da, da_lam[l], da_subln_g[l][:, None], lam_init)
        a = _da_attention(*da_args, False)
        g = _gqa_attention(qt_g, k_g, vt_g, False)
        if not last:
            a = stream(a, _da_attention(*da_args, True))
            g = stream(g, _gqa_attention(qt_g, k_g, vt_g, True))

        h = _ffn(h, mods, norm_g[l, 2:3], ffn2_w13, ffn2_w2, l, 6, n_tiles,
                 mixer=(a, g, hs, gb, w_out), final_g=final_g[None, :] if last else None)
    return h[None]
```

```python
import functools
import math

import jax
import jax.numpy as jnp
from jax import lax
from jax.experimental import pallas as pl
from jax.experimental.pallas import tpu as pltpu

D_MODEL = 1024
SEQ = 16384
CTX_LEN = 256
T_ALL = SEQ + CTX_LEN
DEPTH = 4
GRID_W = 64
HEAD_DIM = 64
SCALE = HEAD_DIM ** -0.5
LOG2E = math.log2(math.e)
ROPE_THETA = 10000.0
DA_HEADS = 4
GQA_KV_HEADS = 2
LRU_WIDTH = 256
LRU_BLOCKS = 4
LRU_BLOCK = LRU_WIDTH // LRU_BLOCKS
LRU_C = 8.0
IN_WIDTH = 2560
D_FF = 2816
N_MOD = 9
EPS = 1e-6

LANES = 128
MXU_DIM = 256

TM = 512
T_PAD = SEQ + TM
N_TILES = T_PAD // TM
N_LAT_TILES = SEQ // TM
TL = 256
N_LRU_LAT = SEQ // TL
FF_CHUNK = MXU_DIM
N_FF_CHUNKS = D_FF // FF_CHUNK
TQ = 1024
TK = 640
N_K_CHUNKS = T_ALL // TK
TK_FAST = 1280
N_K_FAST = T_ALL // TK_FAST
NEG_BIG = -1e30
SHIFT_GUARD = 32.0
BOUND_SLACK = 1.02
L_MIN = 2.0 ** -60
L_MAX = 2.0 ** 100
PV_ROWS = 144
VMEM_LIMIT = 48 * 1024 * 1024

F32 = jnp.float32
BF16 = jnp.bfloat16


def _params(n_axes):
    return pltpu.CompilerParams(dimension_semantics=("arbitrary",) * n_axes,
                                vmem_limit_bytes=VMEM_LIMIT)


def _resident(shape):
    zeros = (0,) * len(shape)
    return pl.BlockSpec(shape, lambda *_: zeros, pipeline_mode=pl.Buffered(1))


def _sigmoid(x):
    return 1.0 / (1.0 + jnp.exp(-x))


def _norm_mod(x, g, shift, scale):
    ms = jnp.mean(x * x, axis=-1, keepdims=True)
    y = (x * lax.rsqrt(ms + EPS)) * g
    return y * (1.0 + scale) + shift


ADA_COLS = 1152


def _ada_kernel(c_ref, w_ref, b_ref, o_ref):
    c = c_ref[...]
    s = (c * _sigmoid(c)).astype(BF16)
    o_ref[0] = jnp.dot(s, w_ref[0].astype(BF16), preferred_element_type=F32) + b_ref[0]


def _ada_mods(cc, ada_w, ada_b):
    width = N_MOD * D_MODEL
    return pl.pallas_call(
        _ada_kernel,
        grid=(DEPTH, width // ADA_COLS),
        in_specs=[
            pl.BlockSpec((8, D_MODEL), lambda l, j: (0, 0)),
            pl.BlockSpec((1, D_MODEL, ADA_COLS), lambda l, j: (l, 0, j)),
            pl.BlockSpec((1, 1, ADA_COLS), lambda l, j: (l, 0, j)),
        ],
        out_specs=pl.BlockSpec((1, 8, ADA_COLS), lambda l, j: (l, 0, j)),
        out_shape=jax.ShapeDtypeStruct((DEPTH, 8, width), F32),
        compiler_params=_params(2),
        name="ada_mods",
    )(cc, ada_w, ada_b.reshape(DEPTH, 1, width))


def _gelu_tanh(x):
    return 0.5 * x * (1.0 + jnp.tanh(math.sqrt(2.0 / math.pi) * (x + 0.044715 * (x * x * x))))


def _mixer_out(a_ref, g_ref, hs_ref, gb_ref, w_ref):
    r = ((hs_ref[0] + hs_ref[1]) * _gelu_tanh(gb_ref[...])).astype(BF16)
    n_a = DA_HEADS * LANES
    n_g = n_a + GQA_KV_HEADS * LANES
    return (jnp.dot(a_ref[...], w_ref[0, 0:n_a, :], preferred_element_type=F32)
            + jnp.dot(g_ref[...], w_ref[0, n_a:n_g, :], preferred_element_type=F32)
            + jnp.dot(r, w_ref[0, n_g:, :], preferred_element_type=F32))


def _ffn_kernel(i0, with_mixer, with_final_norm, h_ref, mod_ref, g_ref, w13_ref, w2_ref, *rest):
    rest = list(rest)
    mixer_refs = [rest.pop(0) for _ in range(5)] if with_mixer else None
    final_g_ref = rest.pop(0) if with_final_norm else None
    o_ref, acc_ref, gate_ref = rest
    x = h_ref[...]
    mods = mod_ref[0]
    if with_mixer:
        x = x + mods[5:6] * _mixer_out(*mixer_refs)
    z = _norm_mod(x, g_ref[...], mods[i0:i0 + 1], mods[i0 + 1:i0 + 2]).astype(BF16)
    acc_ref[...] = jnp.zeros_like(acc_ref)

    def gate(c):
        lo = c * FF_CHUNK
        a = jnp.dot(z, w13_ref[0, :, lo:lo + FF_CHUNK], preferred_element_type=F32)
        b = jnp.dot(z, w13_ref[0, :, D_FF + lo:D_FF + lo + FF_CHUNK], preferred_element_type=F32)
        return (a * _sigmoid(a) * b).astype(BF16)

    def down(c):
        acc_ref[...] += jnp.dot(gate_ref[...], w2_ref[0, c * FF_CHUNK:(c + 1) * FF_CHUNK, :],
                                preferred_element_type=F32)

    gate_ref[...] = gate(0)
    for c in range(1, N_FF_CHUNKS):
        down(c - 1)
        gate_ref[...] = gate(c)
    down(N_FF_CHUNKS - 1)
    y = x + (0.5 * mods[i0 + 2:i0 + 3]) * acc_ref[...]
    if with_final_norm:
        y = (y * lax.rsqrt(jnp.mean(y * y, axis=-1, keepdims=True) + EPS)) * final_g_ref[...]
    o_ref[...] = y


def _layer_resident(shape, layer):
    index = (layer,) + (0,) * len(shape)
    return pl.BlockSpec((1,) + shape, lambda *_: index, pipeline_mode=pl.Buffered(1))


def _ffn(h, mods, g, w13, w2, layer, i0, n_tiles, mixer=None, final_g=None):
    row_tile = lambda width: pl.BlockSpec((TM, width), lambda i: (i, 0))
    vector = pl.BlockSpec((1, D_MODEL), lambda i: (0, 0))
    in_specs = [
        row_tile(D_MODEL),
        pl.BlockSpec((1, N_MOD, D_MODEL), lambda i: (i // N_LAT_TILES, 0, 0)),
        vector,
        _layer_resident((D_MODEL, 2 * D_FF), layer),
        _layer_resident((D_FF, D_MODEL), layer),
    ]
    operands = [h, mods, g, w13, w2]
    if mixer is not None:
        in_specs += [row_tile(DA_HEADS * LANES), row_tile(GQA_KV_HEADS * LANES),
                     pl.BlockSpec((2, TM, LRU_WIDTH), lambda i: (0, i, 0)), row_tile(LRU_WIDTH),
                     _layer_resident((D_MODEL, D_MODEL), layer)]
        operands += list(mixer)
    if final_g is not None:
        in_specs.append(vector)
        operands.append(final_g)
    return pl.pallas_call(
        functools.partial(_ffn_kernel, i0, mixer is not None, final_g is not None),
        grid=(n_tiles,),
        in_specs=in_specs,
        out_specs=row_tile(D_MODEL),
        out_shape=jax.ShapeDtypeStruct((n_tiles * TM, D_MODEL), F32),
        scratch_shapes=[pltpu.VMEM((TM, D_MODEL), F32), pltpu.VMEM((TM, FF_CHUNK), BF16)],
        compiler_params=_params(1),
        name="ffn",
    )(*operands)


def _swap16(x):
    lane = lax.broadcasted_iota(jnp.int32, x.shape, 1)
    return jnp.where(lane % 32 < 16, pltpu.roll(x, LANES - 16, 1), pltpu.roll(x, 16, 1))


def _rope(x, cos, sin):
    return x * cos + _swap16(x) * sin


def _head_mean_sq(x):
    sq = x * x
    hi = sq.astype(BF16)
    lo = (sq - hi.astype(F32)).astype(BF16)
    r = lax.broadcasted_iota(jnp.int32, (LANES, LANES), 0) // HEAD_DIM
    c = lax.broadcasted_iota(jnp.int32, (LANES, LANES), 1) // HEAD_DIM
    ones = jnp.where(r == c, 1.0, 0.0).astype(BF16)
    tot = (jnp.dot(hi, ones, preferred_element_type=F32)
           + jnp.dot(lo, ones, preferred_element_type=F32))
    return tot * (1.0 / HEAD_DIM)


def _inproj_kernel(h_ref, mod_ref, g_ref, w_ref, cos_ref, sin_ref, qkg_ref,
                   qt_da_ref, k_da_ref, vt_da_ref, qt_g_ref, k_g_ref, vt_g_ref, xb_ref, gb_ref):
    mods = mod_ref[0]
    z = _norm_mod(h_ref[...], g_ref[...], mods[3:4], mods[4:5]).astype(BF16)
    proj = jnp.dot(z, w_ref[0], preferred_element_type=F32)
    cos = cos_ref[...]
    sin = sin_ref[...]
    q_scale = SCALE * LOG2E

    def slab(j):
        return proj[:, j * LANES:(j + 1) * LANES]

    for h in range(DA_HEADS):
        q = _rope(slab(h), cos, sin) * q_scale
        qt_da_ref[h] = q.T.astype(BF16)
        k_da_ref[:, h * LANES:(h + 1) * LANES] = _rope(slab(4 + h), cos, sin).astype(BF16)
        vt_da_ref[h] = slab(8 + h).T.astype(BF16)

    gq_gain = qkg_ref[0:1, :]
    gk_gain = qkg_ref[1:2, :]
    for g in range(GQA_KV_HEADS):
        q = slab(12 + g)
        q = (q * lax.rsqrt(_head_mean_sq(q) + EPS)) * gq_gain
        qt_g_ref[g] = (_rope(q, cos, sin) * q_scale).T.astype(BF16)
    k = slab(14)
    k = (k * lax.rsqrt(_head_mean_sq(k) + EPS)) * gk_gain
    k_g_ref[...] = _rope(k, cos, sin).astype(BF16)
    vt_g_ref[...] = slab(15).T.astype(BF16)
    xb_ref[...] = proj[:, 2048:2304]
    gb_ref[...] = proj[:, 2304:2560]


def _inproj(h, mods, g, w_in, layer, cos_t, sin_t, qk_gain):
    row_tile = lambda width: pl.BlockSpec((TM, width), lambda i: (i, 0))
    return pl.pallas_call(
        _inproj_kernel,
        grid=(N_TILES,),
        in_specs=[
            row_tile(D_MODEL),
            pl.BlockSpec((1, N_MOD, D_MODEL), lambda i: (i // N_LAT_TILES, 0, 0)),
            pl.BlockSpec((1, D_MODEL), lambda i: (0, 0)),
            _layer_resident((D_MODEL, IN_WIDTH), layer),
            row_tile(LANES),
            row_tile(LANES),
            pl.BlockSpec((2, LANES), lambda i: (0, 0)),
        ],
        out_specs=[
            pl.BlockSpec((DA_HEADS, LANES, TM), lambda i: (0, 0, i)),
            row_tile(DA_HEADS * LANES),
            pl.BlockSpec((DA_HEADS, LANES, TM), lambda i: (0, 0, i)),
            pl.BlockSpec((GQA_KV_HEADS, LANES, TM), lambda i: (0, 0, i)),
            row_tile(LANES),
            pl.BlockSpec((LANES, TM), lambda i: (0, i)),
            row_tile(LRU_WIDTH),
            row_tile(LRU_WIDTH),
        ],
        out_shape=[
            jax.ShapeDtypeStruct((DA_HEADS, LANES, T_PAD), BF16),
            jax.ShapeDtypeStruct((T_PAD, DA_HEADS * LANES), BF16),
            jax.ShapeDtypeStruct((DA_HEADS, LANES, T_PAD), BF16),
            jax.ShapeDtypeStruct((GQA_KV_HEADS, LANES, T_PAD), BF16),
            jax.ShapeDtypeStruct((T_PAD, LANES), BF16),
            jax.ShapeDtypeStruct((LANES, T_PAD), BF16),
            jax.ShapeDtypeStruct((T_PAD, LRU_WIDTH), F32),
            jax.ShapeDtypeStruct((T_PAD, LRU_WIDTH), F32),
        ],
        compiler_params=_params(1),
        name="inproj",
    )(h, mods, g, w_in, cos_t, sin_t, qk_gain)


N_LRU_STEPS = N_LRU_LAT + 2


def _lru_tile_index(d, j):
    lat = jnp.where(d == 0, j - 1, N_LRU_LAT - j)
    return jnp.where(j == 0, N_LRU_LAT, jnp.where(j == N_LRU_STEPS - 1, N_LRU_LAT + 1, lat))


def _lru_kernel(x_ref, cw_ref, cb_ref, wa_ref, wi_ref, ba_ref, bi_ref, lam_ref, o_ref, carry_ref):
    d = pl.program_id(0)
    j = pl.program_id(1)
    ti = _lru_tile_index(d, j)
    t0 = pl.multiple_of(ti * TL, TL)

    @pl.when(j == 0)
    def _():
        carry_ref[...] = jnp.zeros_like(carry_ref)

    x = x_ref[pl.ds(t0, TL), :]
    prev8 = x_ref[pl.ds(pl.multiple_of(jnp.maximum(t0 - 8, 0), 8), 8), :]
    next8 = x_ref[pl.ds(pl.multiple_of(jnp.minimum(t0 + TL, T_PAD - 8), 8), 8), :]
    has_prev = jnp.logical_and(ti != 0, ti < N_LRU_LAT)
    has_next = ti < N_LRU_LAT - 1
    prev_row = jnp.where(has_prev, prev8[7:8], 0.0)
    next_row0 = jnp.where(has_next, next8[0:1], 0.0)
    next_row1 = jnp.where(has_next, next8[1:2], 0.0)
    row = lax.broadcasted_iota(jnp.int32, (TL, LRU_WIDTH), 0)
    xm1 = jnp.where(row == 0, prev_row, pltpu.roll(x, 1, 0))
    xp1 = jnp.where(row == TL - 1, next_row0, pltpu.roll(x, TL - 1, 0))
    xp2 = jnp.where(row == TL - 2, next_row0,
                    jnp.where(row == TL - 1, next_row1, pltpu.roll(x, TL - 2, 0)))
    cw = cw_ref[...]
    xc = xm1 * cw[0:1] + x * cw[1:2] + xp1 * cw[2:3] + xp2 * cw[3:4] + cb_ref[...]

    xcb = xc.astype(BF16)
    r = _sigmoid(jnp.dot(xcb, wa_ref[0], preferred_element_type=F32) + ba_ref[0])
    gi = _sigmoid(jnp.dot(xcb, wi_ref[0], preferred_element_type=F32) + bi_ref[0])
    neg_lam = -lam_ref[0]
    softplus = jnp.maximum(neg_lam, 0.0) + jnp.log1p(jnp.exp(-jnp.abs(neg_lam)))
    log_a = (-LRU_C) * r * softplus
    a = jnp.exp(log_a)
    u = jnp.sqrt(-jnp.tanh(log_a) * (a * a + 1.0)) * (gi * xc)

    def scan(A, B, forward):
        k = 1
        while k < TL:
            if forward:
                keep = row >= k
                As = jnp.where(keep, pltpu.roll(A, k, 0), 1.0)
                Bs = jnp.where(keep, pltpu.roll(B, k, 0), 0.0)
            else:
                keep = row < TL - k
                As = jnp.where(keep, pltpu.roll(A, TL - k, 0), 1.0)
                Bs = jnp.where(keep, pltpu.roll(B, TL - k, 0), 0.0)
            B = A * Bs + B
            A = A * As
            k *= 2
        return A, B

    carry = carry_ref[0:1, :]

    @pl.when(d == 0)
    def _():
        A, B = scan(a, u, True)
        hs = B + A * carry
        o_ref[0] = hs
        carry_ref[0:1, :] = hs[TL - 1:TL]

    @pl.when(d == 1)
    def _():
        A, B = scan(a, u, False)
        hs = B + A * carry
        o_ref[0] = hs
        carry_ref[0:1, :] = hs[0:1]


def _lru(xb, conv_w, conv_b, wa, wi, ba, bi, lam):
    per_dir = lambda *tail: pl.BlockSpec((1,) + tail, lambda d, j: (d,) + (0,) * len(tail))
    return pl.pallas_call(
        _lru_kernel,
        grid=(2, N_LRU_STEPS),
        in_specs=[
            _resident((T_PAD, LRU_WIDTH)),
            pl.BlockSpec((4, LRU_WIDTH), lambda d, j: (0, 0)),
            pl.BlockSpec((1, LRU_WIDTH), lambda d, j: (0, 0)),
            per_dir(LRU_WIDTH, LRU_WIDTH),
            per_dir(LRU_WIDTH, LRU_WIDTH),
            per_dir(1, LRU_WIDTH),
            per_dir(1, LRU_WIDTH),
            per_dir(1, LRU_WIDTH),
        ],
        out_specs=pl.BlockSpec((1, TL, LRU_WIDTH), lambda d, j: (d, _lru_tile_index(d, j), 0)),
        out_shape=jax.ShapeDtypeStruct((2, T_PAD, LRU_WIDTH), F32),
        scratch_shapes=[pltpu.VMEM((8, LRU_WIDTH), F32)],
        compiler_params=_params(2),
        name="rglru",
    )(xb, conv_w, conv_b, wa, wi, ba, bi, lam)


def _softmax_chunk(s, m_old, l_old):
    m_new = jnp.maximum(m_old, jnp.max(s, axis=0, keepdims=True))
    alpha = jnp.exp2(m_old - m_new)
    p = jnp.exp2(s - m_new)
    l_new = alpha * l_old + jnp.sum(p, axis=0, keepdims=True)
    return p.astype(BF16), m_new, l_new, alpha


def _attend_ctx(kc, vc, weights):
    outs = []
    for w in weights:
        s = jnp.dot(kc, w, preferred_element_type=F32)
        p, _, l, _ = _softmax_chunk(s, jnp.full((1, CTX_LEN), NEG_BIG, F32),
                                    jnp.zeros((1, CTX_LEN), F32))
        outs.append(jnp.dot(vc, p, preferred_element_type=F32) * (1.0 / l))
    return outs


def _key_norm_bound(k_ref, kmax_ref):
    r = lax.broadcasted_iota(jnp.int32, (LANES, LANES), 0) // HEAD_DIM
    c = lax.broadcasted_iota(jnp.int32, (LANES, LANES), 1) // HEAD_DIM
    ones = jnp.where(r == c, 1.0, 0.0).astype(BF16)

    def chunk(i, best):
        kf = k_ref[pl.ds(pl.multiple_of(i * TK, TK), TK), :].astype(F32)
        n2 = jnp.dot((kf * kf).astype(BF16), ones, preferred_element_type=F32)
        return jnp.maximum(best, jnp.max(n2, axis=0, keepdims=True))

    kmax_ref[0:1, :] = lax.fori_loop(0, N_K_CHUNKS, chunk, jnp.zeros((1, LANES), F32))


def _attend_lat_fast(k_ref, v_chunk, weights, shift, acc_ref, p_refs):
    maps = range(len(weights))

    def scores_exp(c, p_ref, l):
        kc = k_ref[pl.ds(pl.multiple_of(c * TK_FAST, TK_FAST), TK_FAST), :]
        l_new = []
        for j in maps:
            p = jnp.exp2(jnp.dot(kc, weights[j], preferred_element_type=F32) - shift[j])
            l_new.append(l[j] + jnp.sum(p, axis=0, keepdims=True))
            p_ref[j] = p.astype(BF16)
        return tuple(l_new)

    def values(c, p_ref):
        vc = v_chunk(pl.multiple_of(c * TK_FAST, TK_FAST), TK_FAST)
        for j in maps:
            acc_ref[j] += jnp.dot(vc, p_ref[j], preferred_element_type=F32)

    acc_ref[...] = jnp.zeros_like(acc_ref)
    l = scores_exp(0, p_refs[0], tuple(jnp.zeros((1, TQ), F32) for _ in maps))

    def chunk_pair(i, l):
        c0 = 2 * i
        l = scores_exp(c0 + 1, p_refs[1], l)
        values(c0, p_refs[0])
        l = scores_exp(c0 + 2, p_refs[0], l)
        values(c0 + 1, p_refs[1])
        return l

    n_pairs = (N_K_FAST - 1) // 2
    l = lax.fori_loop(0, n_pairs, chunk_pair, l, unroll=3)
    if N_K_FAST % 2 == 0:
        l = scores_exp(N_K_FAST - 1, p_refs[1], l)
        values(N_K_FAST - 2, p_refs[0])
        values(N_K_FAST - 1, p_refs[1])
    else:
        values(N_K_FAST - 1, p_refs[0])
    return l


def _attend_lat_exact(k_ref, v_chunk, weights, acc_ref, s_refs, p_refs):
    maps = range(len(weights))

    def scores(c, s_ref):
        kc = k_ref[pl.ds(pl.multiple_of(c * TK, TK), TK), :]
        for j in maps:
            s_ref[j] = jnp.dot(kc, weights[j], preferred_element_type=F32)

    def softmax(s_ref, p_ref, m, l):
        m_new, l_new, alpha = [], [], []
        for j in maps:
            m_j = jnp.maximum(m[j], jnp.max(s_ref[j], axis=0, keepdims=True))
            a_j = jnp.exp2(m[j] - m_j)
            p = jnp.exp2(s_ref[j] - m_j)
            l_new.append(a_j * l[j] + jnp.sum(p, axis=0, keepdims=True))
            p_ref[j, 0:TK, :] = p.astype(BF16)
            m_new.append(m_j)
            alpha.append(a_j)
        return m_new, l_new, alpha

    def values(c, p_ref, alpha):
        vc = v_chunk(pl.multiple_of(c * TK, TK), TK)
        for j in maps:
            acc_ref[j] = alpha[j] * acc_ref[j] + jnp.dot(vc, p_ref[j, 0:TK, :],
                                                         preferred_element_type=F32)

    acc_ref[...] = jnp.zeros_like(acc_ref)
    p_refs[1][...] = jnp.zeros_like(p_refs[1])
    scores(0, s_refs[0])
    stat = lambda v: [jnp.full((1, TQ), v, F32) for _ in maps]

    def chunk_pair(i, carry):
        m, l, alpha_b = carry
        c0 = 2 * i
        scores(c0 + 1, s_refs[1])
        m, l, alpha_a = softmax(s_refs[0], p_refs[0], m, l)
        values(jnp.maximum(c0 - 1, 0), p_refs[1], alpha_b)
        scores(jnp.minimum(c0 + 2, N_K_CHUNKS - 1), s_refs[0])
        m, l, alpha_b = softmax(s_refs[1], p_refs[1], m, l)
        values(c0, p_refs[0], alpha_a)
        return m, l, alpha_b

    _, l, alpha_b = lax.fori_loop(0, N_K_CHUNKS // 2, chunk_pair,
                                  (stat(NEG_BIG), stat(0.0), stat(1.0)))
    values(N_K_CHUNKS - 1, p_refs[1], alpha_b)
    return l


def _attend(ctx_only, k_ref, vt_at, weights, halves, scratch):
    if ctx_only:
        return _attend_ctx(k_ref[...], vt_at(0, CTX_LEN), weights)
    acc_ref, l_ref, kmax_ref, s0_ref, s1_ref, p0_ref, p1_ref = scratch
    maps = range(len(weights))

    @pl.when(pl.program_id(1) == 0)
    def _():
        _key_norm_bound(k_ref, kmax_ref)

    lane = lax.broadcasted_iota(jnp.int32, (1, LANES), 1)
    shift = []
    for j in maps:
        k2 = jnp.max(jnp.where(lane // HEAD_DIM == halves[j], kmax_ref[0:1, :], 0.0),
                     axis=-1, keepdims=True)
        wf = weights[j].astype(F32)
        q2 = jnp.sum(wf * wf, axis=0, keepdims=True)
        shift.append(jnp.sqrt(q2 * k2) * BOUND_SLACK - SHIFT_GUARD)

    l = _attend_lat_fast(k_ref, vt_at, weights, shift, acc_ref, (p0_ref, p1_ref))
    n_bad = jnp.zeros((), F32)
    for j in maps:
        l_ref[j:j + 1, :] = l[j]
        ok = jnp.logical_and(l[j] >= L_MIN, l[j] <= L_MAX)
        n_bad = n_bad + jnp.sum(jnp.where(ok, 0.0, 1.0))

    @pl.when(n_bad > 0.0)
    def _():
        l = _attend_lat_exact(k_ref, vt_at, weights, acc_ref, (s0_ref, s1_ref), (p0_ref, p1_ref))
        for j in maps:
            l_ref[j:j + 1, :] = l[j]

    return [acc_ref[j] * (1.0 / l_ref[j:j + 1, :]) for j in maps]


def _attn_call(kernel, ctx_only, n_heads, dv, in_blocks, operands, out_width, name):
    if ctx_only:
        tq, n_q, n_keys, q0, k0 = CTX_LEN, 1, CTX_LEN, SEQ // CTX_LEN, SEQ // CTX_LEN
        scratch = []
    else:
        tq, n_q, n_keys, q0, k0 = TQ, SEQ // TQ, T_ALL, 0, 0
        scratch = [pltpu.VMEM((2, dv, TQ), F32), pltpu.VMEM((8, TQ), F32),
                   pltpu.VMEM((8, LANES), F32),
                   pltpu.VMEM((2, TK, TQ), F32), pltpu.VMEM((2, TK, TQ), F32),
                   pltpu.VMEM((2, TK_FAST, TQ), BF16), pltpu.VMEM((2, TK_FAST, TQ), BF16)]
    return pl.pallas_call(
        functools.partial(kernel, ctx_only),
        grid=(n_heads, n_q),
        in_specs=in_blocks(tq, n_keys, q0, k0),
        out_specs=pl.BlockSpec((tq, LANES), lambda h, i: (i, h)),
        out_shape=jax.ShapeDtypeStruct((n_q * tq, out_width), BF16),
        scratch_shapes=scratch,
        compiler_params=_params(2),
        name=name + ("_ctx" if ctx_only else ""),
    )(*operands)


def _da_kernel(lam_init, ctx_only, qt_ref, k_ref, vt_ref, lam_ref, g_ref, o_ref, *scratch):
    qt = qt_ref[0]
    row = lax.broadcasted_iota(jnp.int32, qt.shape, 0)
    zero = jnp.zeros_like(qt)
    weights = [jnp.where(row < HEAD_DIM, qt, zero), jnp.where(row >= HEAD_DIM, qt, zero)]
    o1, o2 = _attend(ctx_only, k_ref, lambda s, n: vt_ref[0, :, pl.ds(s, n)], weights, (0, 1),
                     scratch)
    lf = lam_ref[...]
    lam = (jnp.exp(jnp.sum(lf[0:1] * lf[1:2], axis=-1, keepdims=True))
           - jnp.exp(jnp.sum(lf[2:3] * lf[3:4], axis=-1, keepdims=True)) + lam_init)
    o = o1 - lam * o2
    ms = jnp.mean(o * o, axis=0, keepdims=True)
    on = (o * lax.rsqrt(ms + EPS)) * g_ref[...] * (1.0 - lam_init)
    o_ref[...] = on.T.astype(o_ref.dtype)


def _da_attention(qt, k, vt, da_lam, subln_g, lam_init, ctx_only):
    def in_blocks(tq, n_keys, q0, k0):
        return [
            pl.BlockSpec((1, LANES, tq), lambda h, i: (h, 0, i + q0)),
            pl.BlockSpec((n_keys, LANES), lambda h, i: (k0, h)),
            pl.BlockSpec((1, LANES, n_keys), lambda h, i: (h, 0, k0)),
            pl.BlockSpec((4, HEAD_DIM), lambda h, i: (0, 0)),
            pl.BlockSpec((LANES, 1), lambda h, i: (0, 0)),
        ]
    return _attn_call(functools.partial(_da_kernel, lam_init), ctx_only, DA_HEADS, LANES,
                      in_blocks, (qt, k, vt, da_lam, subln_g), DA_HEADS * LANES, "da_attention")


def _gqa_kernel(ctx_only, qt_ref, k_ref, vt_ref, o_ref, *scratch):
    g = pl.program_id(0)
    qt = qt_ref[0]
    zero = jnp.zeros((HEAD_DIM, qt.shape[1]), qt.dtype)
    weights = []
    for j in range(2):
        qj = qt[j * HEAD_DIM:(j + 1) * HEAD_DIM]
        weights.append(jnp.where(g == 0, jnp.concatenate([qj, zero], axis=0),
                                 jnp.concatenate([zero, qj], axis=0)))
    def vt_at(s, n):
        v = vt_ref[:, pl.ds(s, n)]
        return jnp.concatenate([v, jnp.zeros((PV_ROWS - HEAD_DIM, n), v.dtype)], axis=0)

    o0, o1 = _attend(ctx_only, k_ref, vt_at, weights, (g, g), scratch)
    o_ref[...] = jnp.concatenate([o0[0:HEAD_DIM], o1[0:HEAD_DIM]], axis=0).T.astype(o_ref.dtype)


def _gqa_attention(qt, k, vt, ctx_only):
    def in_blocks(tq, n_keys, q0, k0):
        return [
            pl.BlockSpec((1, LANES, tq), lambda g, i: (g, 0, i + q0)),
            pl.BlockSpec((n_keys, LANES), lambda g, i: (k0, 0)),
            pl.BlockSpec((HEAD_DIM, n_keys), lambda g, i: (g, k0)),
        ]
    return _attn_call(_gqa_kernel, ctx_only, GQA_KV_HEADS, PV_ROWS, in_blocks, (qt, k, vt),
                      GQA_KV_HEADS * LANES, "gqa_attention")


def _rope_tables():
    half = HEAD_DIM // 4
    t = jnp.arange(SEQ, dtype=jnp.int32)
    row = (t // GRID_W).astype(F32)
    col = (t % GRID_W).astype(F32)
    inv = ROPE_THETA ** (-jnp.arange(half, dtype=F32) / half)
    ang_r = row[:, None] * inv
    ang_c = col[:, None] * inv
    cos64 = jnp.concatenate([jnp.cos(ang_r)] * 2 + [jnp.cos(ang_c)] * 2, axis=-1)
    sin64 = jnp.concatenate([-jnp.sin(ang_r), jnp.sin(ang_r), -jnp.sin(ang_c), jnp.sin(ang_c)],
                            axis=-1)
    n_rest = T_PAD - SEQ
    cos_t = jnp.concatenate([jnp.tile(cos64, (1, 2)), jnp.ones((n_rest, LANES), F32)], axis=0)
    sin_t = jnp.concatenate([jnp.tile(sin64, (1, 2)), jnp.zeros((n_rest, LANES), F32)], axis=0)
    return cos_t, sin_t


def _block_diag(w):
    eye = jnp.eye(LRU_BLOCKS, dtype=w.dtype)
    full = jnp.einsum('dkij,kl->dkilj', w, eye)
    return full.reshape(2, LRU_WIDTH, LRU_WIDTH).astype(BF16)


def kernel(x, c, ctx, c_ctx, ada_w, ada_b, norm_g, ffn1_w13, ffn1_w2, ffn2_w13, ffn2_w2,
           w_in, w_out, da_lam, da_subln_g, qk_norm_g, lru_conv_w, lru_conv_b,
           lru_wa, lru_ba, lru_wi, lru_bi, lru_lambda, final_g):
    assert x.shape == (1, SEQ, D_MODEL) and ctx.shape == (1, CTX_LEN, D_MODEL)
    cc = jnp.zeros((8, D_MODEL), F32).at[0].set(c[0]).at[1].set(c_ctx)
    mods_all = _ada_mods(cc, ada_w, ada_b).reshape(DEPTH, 8, N_MOD, D_MODEL)[:, :2]
    cos_t, sin_t = _rope_tables()
    ffn1_w13, ffn1_w2, ffn2_w13, ffn2_w2, w_in, w_out = (
        w.astype(BF16) for w in (ffn1_w13, ffn1_w2, ffn2_w13, ffn2_w2, w_in, w_out))

    def stream(lat, ctx_rows):
        rest = jnp.zeros((T_PAD - T_ALL, lat.shape[1]), lat.dtype)
        return jnp.concatenate([lat, ctx_rows, rest], axis=0)

    h = stream(x[0], ctx[0])
    for l in range(DEPTH):
        last = l == DEPTH - 1
        lam_init = 0.8 - 0.6 * math.exp(-0.3 * l)
        mods = mods_all[l]

        h = _ffn(h, mods, norm_g[l, 0:1], ffn1_w13, ffn1_w2, l, 0, N_TILES)

        qk_gain = jnp.tile(qk_norm_g[l], (1, 2))
        qt_da, k_da, vt_da, qt_g, k_g, vt_g, xb, gb = _inproj(
            h, mods, norm_g[l, 1:2], w_in, l, cos_t, sin_t, qk_gain)

        hs = _lru(xb, lru_conv_w[l], lru_conv_b[l][None, :], _block_diag(lru_wa[l]),
                  _block_diag(lru_wi[l]), lru_ba[l][:, None, :], lru_bi[l][:, None, :],
                  lru_lambda[l][:, None, :])

        n_tiles = N_LAT_TILES if last else N_TILES
        da_args = (qt_da, k_da, vt_da, da_lam[l], da_subln_g[l][:, None], lam_init)
        a = _da_attention(*da_args, False)
        g = _gqa_attention(qt_g, k_g, vt_g, False)
        if not last:
            a = stream(a, _da_attention(*da_args, True))
            g = stream(g, _gqa_attention(qt_g, k_g, vt_g, True))

        h = _ffn(h, mods, norm_g[l, 2:3], ffn2_w13, ffn2_w2, l, 6, n_tiles,
                 mixer=(a, g, hs, gb, w_out), final_g=final_g[None, :] if last else None)
    return h[None]
```

```python
import functools
import math

import jax
import jax.numpy as jnp
from jax import lax
from jax.experimental import pallas as pl
from jax.experimental.pallas import tpu as pltpu

D_MODEL = 1024
SEQ = 16384
CTX_LEN = 256
T_ALL = SEQ + CTX_LEN
DEPTH = 4
GRID_W = 64
HEAD_DIM = 64
SCALE = HEAD_DIM ** -0.5
LOG2E = math.log2(math.e)
ROPE_THETA = 10000.0
DA_HEADS = 4
GQA_KV_HEADS = 2
LRU_WIDTH = 256
LRU_BLOCKS = 4
LRU_BLOCK = LRU_WIDTH // LRU_BLOCKS
LRU_C = 8.0
IN_WIDTH = 2560
D_FF = 2816
N_MOD = 9
EPS = 1e-6

LANES = 128
MXU_DIM = 256

TM = 512
T_PAD = SEQ + TM
N_TILES = T_PAD // TM
N_LAT_TILES = SEQ // TM
TL = 256
N_LRU_LAT = SEQ // TL
FF_CHUNK = MXU_DIM
N_FF_CHUNKS = D_FF // FF_CHUNK
TQ = 1024
TK = 640
N_K_CHUNKS = T_ALL // TK
TK_FAST = 1280
N_K_FAST = T_ALL // TK_FAST
NEG_BIG = -1e30
SHIFT_GUARD = 32.0
BOUND_SLACK = 1.02
L_MIN = 2.0 ** -60
L_MAX = 2.0 ** 100
VMEM_LIMIT = 48 * 1024 * 1024

F32 = jnp.float32
BF16 = jnp.bfloat16


def _params(n_axes):
    return pltpu.CompilerParams(dimension_semantics=("arbitrary",) * n_axes,
                                vmem_limit_bytes=VMEM_LIMIT)


def _resident(shape):
    zeros = (0,) * len(shape)
    return pl.BlockSpec(shape, lambda *_: zeros, pipeline_mode=pl.Buffered(1))


def _sigmoid(x):
    return 1.0 / (1.0 + jnp.exp(-x))


def _norm_mod(x, g, shift, scale):
    ms = jnp.mean(x * x, axis=-1, keepdims=True)
    y = (x * lax.rsqrt(ms + EPS)) * g
    return y * (1.0 + scale) + shift


ADA_COLS = 1152


def _ada_kernel(c_ref, w_ref, b_ref, o_ref):
    c = c_ref[...]
    s = (c * _sigmoid(c)).astype(BF16)
    o_ref[0] = jnp.dot(s, w_ref[0].astype(BF16), preferred_element_type=F32) + b_ref[0]


def _ada_mods(cc, ada_w, ada_b):
    width = N_MOD * D_MODEL
    return pl.pallas_call(
        _ada_kernel,
        grid=(DEPTH, width // ADA_COLS),
        in_specs=[
            pl.BlockSpec((8, D_MODEL), lambda l, j: (0, 0)),
            pl.BlockSpec((1, D_MODEL, ADA_COLS), lambda l, j: (l, 0, j)),
            pl.BlockSpec((1, 1, ADA_COLS), lambda l, j: (l, 0, j)),
        ],
        out_specs=pl.BlockSpec((1, 8, ADA_COLS), lambda l, j: (l, 0, j)),
        out_shape=jax.ShapeDtypeStruct((DEPTH, 8, width), F32),
        compiler_params=_params(2),
        name="ada_mods",
    )(cc, ada_w, ada_b.reshape(DEPTH, 1, width))


def _gelu_tanh(x):
    return 0.5 * x * (1.0 + jnp.tanh(math.sqrt(2.0 / math.pi) * (x + 0.044715 * (x * x * x))))


def _mixer_out(a_ref, g_ref, hs_ref, gb_ref, w_ref):
    r = ((hs_ref[0] + hs_ref[1]) * _gelu_tanh(gb_ref[...])).astype(BF16)
    n_a = DA_HEADS * LANES
    n_g = n_a + GQA_KV_HEADS * LANES
    return (jnp.dot(a_ref[...], w_ref[0, 0:n_a, :], preferred_element_type=F32)
            + jnp.dot(g_ref[...], w_ref[0, n_a:n_g, :], preferred_element_type=F32)
            + jnp.dot(r, w_ref[0, n_g:, :], preferred_element_type=F32))


def _ffn_kernel(i0, with_mixer, with_final_norm, h_ref, mod_ref, g_ref, w13_ref, w2_ref, *rest):
    rest = list(rest)
    mixer_refs = [rest.pop(0) for _ in range(5)] if with_mixer else None
    final_g_ref = rest.pop(0) if with_final_norm else None
    o_ref, acc_ref, gate_ref = rest
    x = h_ref[...]
    mods = mod_ref[0]
    if with_mixer:
        x = x + mods[5:6] * _mixer_out(*mixer_refs)
    z = _norm_mod(x, g_ref[...], mods[i0:i0 + 1], mods[i0 + 1:i0 + 2]).astype(BF16)
    acc_ref[...] = jnp.zeros_like(acc_ref)

    def gate(c):
        lo = c * FF_CHUNK
        a = jnp.dot(z, w13_ref[0, :, lo:lo + FF_CHUNK], preferred_element_type=F32)
        b = jnp.dot(z, w13_ref[0, :, D_FF + lo:D_FF + lo + FF_CHUNK], preferred_element_type=F32)
        return (a * _sigmoid(a) * b).astype(BF16)

    def down(c):
        acc_ref[...] += jnp.dot(gate_ref[...], w2_ref[0, c * FF_CHUNK:(c + 1) * FF_CHUNK, :],
                                preferred_element_type=F32)

    gate_ref[...] = gate(0)
    for c in range(1, N_FF_CHUNKS):
        down(c - 1)
        gate_ref[...] = gate(c)
    down(N_FF_CHUNKS - 1)
    y = x + (0.5 * mods[i0 + 2:i0 + 3]) * acc_ref[...]
    if with_final_norm:
        y = (y * lax.rsqrt(jnp.mean(y * y, axis=-1, keepdims=True) + EPS)) * final_g_ref[...]
    o_ref[...] = y


def _layer_resident(shape, layer):
    index = (layer,) + (0,) * len(shape)
    return pl.BlockSpec((1,) + shape, lambda *_: index, pipeline_mode=pl.Buffered(1))


def _ffn(h, mods, g, w13, w2, layer, i0, n_tiles, mixer=None, final_g=None):
    row_tile = lambda width: pl.BlockSpec((TM, width), lambda i: (i, 0))
    vector = pl.BlockSpec((1, D_MODEL), lambda i: (0, 0))
    in_specs = [
        row_tile(D_MODEL),
        pl.BlockSpec((1, N_MOD, D_MODEL), lambda i: (i // N_LAT_TILES, 0, 0)),
        vector,
        _layer_resident((D_MODEL, 2 * D_FF), layer),
        _layer_resident((D_FF, D_MODEL), layer),
    ]
    operands = [h, mods, g, w13, w2]
    if mixer is not None:
        in_specs += [row_tile(DA_HEADS * LANES), row_tile(GQA_KV_HEADS * LANES),
                     pl.BlockSpec((2, TM, LRU_WIDTH), lambda i: (0, i, 0)), row_tile(LRU_WIDTH),
                     _layer_resident((D_MODEL, D_MODEL), layer)]
        operands += list(mixer)
    if final_g is not None:
        in_specs.append(vector)
        operands.append(final_g)
    return pl.pallas_call(
        functools.partial(_ffn_kernel, i0, mixer is not None, final_g is not None),
        grid=(n_tiles,),
        in_specs=in_specs,
        out_specs=row_tile(D_MODEL),
        out_shape=jax.ShapeDtypeStruct((n_tiles * TM, D_MODEL), F32),
        scratch_shapes=[pltpu.VMEM((TM, D_MODEL), F32), pltpu.VMEM((TM, FF_CHUNK), BF16)],
        compiler_params=_params(1),
        name="ffn",
    )(*operands)


def _swap16(x):
    lane = lax.broadcasted_iota(jnp.int32, x.shape, 1)
    return jnp.where(lane % 32 < 16, pltpu.roll(x, LANES - 16, 1), pltpu.roll(x, 16, 1))


def _rope(x, cos, sin):
    return x * cos + _swap16(x) * sin


def _head_mean_sq(x):
    sq = x * x
    hi = sq.astype(BF16)
    lo = (sq - hi.astype(F32)).astype(BF16)
    r = lax.broadcasted_iota(jnp.int32, (LANES, LANES), 0) // HEAD_DIM
    c = lax.broadcasted_iota(jnp.int32, (LANES, LANES), 1) // HEAD_DIM
    ones = jnp.where(r == c, 1.0, 0.0).astype(BF16)
    tot = (jnp.dot(hi, ones, preferred_element_type=F32)
           + jnp.dot(lo, ones, preferred_element_type=F32))
    return tot * (1.0 / HEAD_DIM)


def _inproj_kernel(h_ref, mod_ref, g_ref, w_ref, cos_ref, sin_ref, qkg_ref,
                   qt_da_ref, k_da_ref, vt_da_ref, qt_g_ref, k_g_ref, vt_g_ref, xb_ref, gb_ref):
    mods = mod_ref[0]
    z = _norm_mod(h_ref[...], g_ref[...], mods[3:4], mods[4:5]).astype(BF16)
    proj = jnp.dot(z, w_ref[0], preferred_element_type=F32)
    cos = cos_ref[...]
    sin = sin_ref[...]
    q_scale = SCALE * LOG2E

    def slab(j):
        return proj[:, j * LANES:(j + 1) * LANES]

    for h in range(DA_HEADS):
        q = _rope(slab(h), cos, sin) * q_scale
        qt_da_ref[h] = q.T.astype(BF16)
        k_da_ref[:, h * LANES:(h + 1) * LANES] = _rope(slab(4 + h), cos, sin).astype(BF16)
        vt_da_ref[h] = slab(8 + h).T.astype(BF16)

    gq_gain = qkg_ref[0:1, :]
    gk_gain = qkg_ref[1:2, :]
    for g in range(GQA_KV_HEADS):
        q = slab(12 + g)
        q = (q * lax.rsqrt(_head_mean_sq(q) + EPS)) * gq_gain
        qt_g_ref[g] = (_rope(q, cos, sin) * q_scale).T.astype(BF16)
    k = slab(14)
    k = (k * lax.rsqrt(_head_mean_sq(k) + EPS)) * gk_gain
    k_g_ref[...] = _rope(k, cos, sin).astype(BF16)
    vt_g_ref[...] = slab(15).T.astype(BF16)
    xb_ref[...] = proj[:, 2048:2304]
    gb_ref[...] = proj[:, 2304:2560]


def _inproj(h, mods, g, w_in, layer, cos_t, sin_t, qk_gain):
    row_tile = lambda width: pl.BlockSpec((TM, width), lambda i: (i, 0))
    return pl.pallas_call(
        _inproj_kernel,
        grid=(N_TILES,),
        in_specs=[
            row_tile(D_MODEL),
            pl.BlockSpec((1, N_MOD, D_MODEL), lambda i: (i // N_LAT_TILES, 0, 0)),
            pl.BlockSpec((1, D_MODEL), lambda i: (0, 0)),
            _layer_resident((D_MODEL, IN_WIDTH), layer),
            row_tile(LANES),
            row_tile(LANES),
            pl.BlockSpec((2, LANES), lambda i: (0, 0)),
        ],
        out_specs=[
            pl.BlockSpec((DA_HEADS, LANES, TM), lambda i: (0, 0, i)),
            row_tile(DA_HEADS * LANES),
            pl.BlockSpec((DA_HEADS, LANES, TM), lambda i: (0, 0, i)),
            pl.BlockSpec((GQA_KV_HEADS, LANES, TM), lambda i: (0, 0, i)),
            row_tile(LANES),
            pl.BlockSpec((LANES, TM), lambda i: (0, i)),
            row_tile(LRU_WIDTH),
            row_tile(LRU_WIDTH),
        ],
        out_shape=[
            jax.ShapeDtypeStruct((DA_HEADS, LANES, T_PAD), BF16),
            jax.ShapeDtypeStruct((T_PAD, DA_HEADS * LANES), BF16),
            jax.ShapeDtypeStruct((DA_HEADS, LANES, T_PAD), BF16),
            jax.ShapeDtypeStruct((GQA_KV_HEADS, LANES, T_PAD), BF16),
            jax.ShapeDtypeStruct((T_PAD, LANES), BF16),
            jax.ShapeDtypeStruct((LANES, T_PAD), BF16),
            jax.ShapeDtypeStruct((T_PAD, LRU_WIDTH), F32),
            jax.ShapeDtypeStruct((T_PAD, LRU_WIDTH), F32),
        ],
        compiler_params=_params(1),
        name="inproj",
    )(h, mods, g, w_in, cos_t, sin_t, qk_gain)


N_LRU_STEPS = N_LRU_LAT + 2


def _lru_tile_index(d, j):
    lat = jnp.where(d == 0, j - 1, N_LRU_LAT - j)
    return jnp.where(j == 0, N_LRU_LAT, jnp.where(j == N_LRU_STEPS - 1, N_LRU_LAT + 1, lat))


def _lru_kernel(x_ref, cw_ref, cb_ref, wa_ref, wi_ref, ba_ref, bi_ref, lam_ref, o_ref, carry_ref):
    d = pl.program_id(0)
    j = pl.program_id(1)
    ti = _lru_tile_index(d, j)
    t0 = pl.multiple_of(ti * TL, TL)

    @pl.when(j == 0)
    def _():
        carry_ref[...] = jnp.zeros_like(carry_ref)

    x = x_ref[pl.ds(t0, TL), :]
    prev8 = x_ref[pl.ds(pl.multiple_of(jnp.maximum(t0 - 8, 0), 8), 8), :]
    next8 = x_ref[pl.ds(pl.multiple_of(jnp.minimum(t0 + TL, T_PAD - 8), 8), 8), :]
    has_prev = jnp.logical_and(ti != 0, ti < N_LRU_LAT)
    has_next = ti < N_LRU_LAT - 1
    prev_row = jnp.where(has_prev, prev8[7:8], 0.0)
    next_row0 = jnp.where(has_next, next8[0:1], 0.0)
    next_row1 = jnp.where(has_next, next8[1:2], 0.0)
    row = lax.broadcasted_iota(jnp.int32, (TL, LRU_WIDTH), 0)
    xm1 = jnp.where(row == 0, prev_row, pltpu.roll(x, 1, 0))
    xp1 = jnp.where(row == TL - 1, next_row0, pltpu.roll(x, TL - 1, 0))
    xp2 = jnp.where(row == TL - 2, next_row0,
                    jnp.where(row == TL - 1, next_row1, pltpu.roll(x, TL - 2, 0)))
    cw = cw_ref[...]
    xc = xm1 * cw[0:1] + x * cw[1:2] + xp1 * cw[2:3] + xp2 * cw[3:4] + cb_ref[...]

    xcb = xc.astype(BF16)
    r = _sigmoid(jnp.dot(xcb, wa_ref[0], preferred_element_type=F32) + ba_ref[0])
    gi = _sigmoid(jnp.dot(xcb, wi_ref[0], preferred_element_type=F32) + bi_ref[0])
    neg_lam = -lam_ref[0]
    softplus = jnp.maximum(neg_lam, 0.0) + jnp.log1p(jnp.exp(-jnp.abs(neg_lam)))
    log_a = (-LRU_C) * r * softplus
    a = jnp.exp(log_a)
    u = jnp.sqrt(-jnp.tanh(log_a) * (a * a + 1.0)) * (gi * xc)

    def scan(A, B, forward):
        k = 1
        while k < TL:
            if forward:
                keep = row >= k
                As = jnp.where(keep, pltpu.roll(A, k, 0), 1.0)
                Bs = jnp.where(keep, pltpu.roll(B, k, 0), 0.0)
            else:
                keep = row < TL - k
                As = jnp.where(keep, pltpu.roll(A, TL - k, 0), 1.0)
                Bs = jnp.where(keep, pltpu.roll(B, TL - k, 0), 0.0)
            B = A * Bs + B
            A = A * As
            k *= 2
        return A, B

    carry = carry_ref[0:1, :]

    @pl.when(d == 0)
    def _():
        A, B = scan(a, u, True)
        hs = B + A * carry
        o_ref[0] = hs
        carry_ref[0:1, :] = hs[TL - 1:TL]

    @pl.when(d == 1)
    def _():
        A, B = scan(a, u, False)
        hs = B + A * carry
        o_ref[0] = hs
        carry_ref[0:1, :] = hs[0:1]


def _lru(xb, conv_w, conv_b, wa, wi, ba, bi, lam):
    per_dir = lambda *tail: pl.BlockSpec((1,) + tail, lambda d, j: (d,) + (0,) * len(tail))
    return pl.pallas_call(
        _lru_kernel,
        grid=(2, N_LRU_STEPS),
        in_specs=[
            _resident((T_PAD, LRU_WIDTH)),
            pl.BlockSpec((4, LRU_WIDTH), lambda d, j: (0, 0)),
            pl.BlockSpec((1, LRU_WIDTH), lambda d, j: (0, 0)),
            per_dir(LRU_WIDTH, LRU_WIDTH),
            per_dir(LRU_WIDTH, LRU_WIDTH),
            per_dir(1, LRU_WIDTH),
            per_dir(1, LRU_WIDTH),
            per_dir(1, LRU_WIDTH),
        ],
        out_specs=pl.BlockSpec((1, TL, LRU_WIDTH), lambda d, j: (d, _lru_tile_index(d, j), 0)),
        out_shape=jax.ShapeDtypeStruct((2, T_PAD, LRU_WIDTH), F32),
        scratch_shapes=[pltpu.VMEM((8, LRU_WIDTH), F32)],
        compiler_params=_params(2),
        name="rglru",
    )(xb, conv_w, conv_b, wa, wi, ba, bi, lam)


def _softmax_chunk(s, m_old, l_old):
    m_new = jnp.maximum(m_old, jnp.max(s, axis=0, keepdims=True))
    alpha = jnp.exp2(m_old - m_new)
    p = jnp.exp2(s - m_new)
    l_new = alpha * l_old + jnp.sum(p, axis=0, keepdims=True)
    return p.astype(BF16), m_new, l_new, alpha


def _attend_ctx(kc, vc, weights):
    outs = []
    for w in weights:
        s = jnp.dot(kc, w, preferred_element_type=F32)
        p, _, l, _ = _softmax_chunk(s, jnp.full((1, CTX_LEN), NEG_BIG, F32),
                                    jnp.zeros((1, CTX_LEN), F32))
        outs.append(jnp.dot(vc, p, preferred_element_type=F32) * (1.0 / l))
    return outs


def _key_norm_bound(k_ref, kmax_ref):
    r = lax.broadcasted_iota(jnp.int32, (LANES, LANES), 0) // HEAD_DIM
    c = lax.broadcasted_iota(jnp.int32, (LANES, LANES), 1) // HEAD_DIM
    ones = jnp.where(r == c, 1.0, 0.0).astype(BF16)

    def chunk(i, best):
        kf = k_ref[pl.ds(pl.multiple_of(i * TK, TK), TK), :].astype(F32)
        n2 = jnp.dot((kf * kf).astype(BF16), ones, preferred_element_type=F32)
        return jnp.maximum(best, jnp.max(n2, axis=0, keepdims=True))

    kmax_ref[0:1, :] = lax.fori_loop(0, N_K_CHUNKS, chunk, jnp.zeros((1, LANES), F32))


def _attend_lat_fast(k_ref, v_chunk, weights, shift, acc_ref, p_refs):
    maps = range(len(weights))

    def scores_exp(c, p_ref, l):
        kc = k_ref[pl.ds(pl.multiple_of(c * TK_FAST, TK_FAST), TK_FAST), :]
        l_new = []
        for j in maps:
            p = jnp.exp2(jnp.dot(kc, weights[j], preferred_element_type=F32) - shift[j])
            l_new.append(l[j] + jnp.sum(p, axis=0, keepdims=True))
            p_ref[j] = p.astype(BF16)
        return tuple(l_new)

    def values(c, p_ref):
        vc = v_chunk(pl.multiple_of(c * TK_FAST, TK_FAST), TK_FAST)
        for j in maps:
            acc_ref[j] += jnp.dot(vc, p_ref[j], preferred_element_type=F32)

    acc_ref[...] = jnp.zeros_like(acc_ref)
    l = scores_exp(0, p_refs[0], tuple(jnp.zeros((1, TQ), F32) for _ in maps))

    def chunk_pair(i, l):
        c0 = 2 * i
        l = scores_exp(c0 + 1, p_refs[1], l)
        values(c0, p_refs[0])
        l = scores_exp(c0 + 2, p_refs[0], l)
        values(c0 + 1, p_refs[1])
        return l

    n_pairs = (N_K_FAST - 1) // 2
    for i in range(n_pairs):
        l = chunk_pair(i, l)
    if N_K_FAST % 2 == 0:
        l = scores_exp(N_K_FAST - 1, p_refs[1], l)
        values(N_K_FAST - 2, p_refs[0])
        values(N_K_FAST - 1, p_refs[1])
    else:
        values(N_K_FAST - 1, p_refs[0])
    return l


def _attend_lat_exact(k_ref, v_chunk, weights, acc_ref, s_refs, p_refs):
    maps = range(len(weights))

    def scores(c, s_ref):
        kc = k_ref[pl.ds(pl.multiple_of(c * TK, TK), TK), :]
        for j in maps:
            s_ref[j] = jnp.dot(kc, weights[j], preferred_element_type=F32)

    def softmax(s_ref, p_ref, m, l):
        m_new, l_new, alpha = [], [], []
        for j in maps:
            m_j = jnp.maximum(m[j], jnp.max(s_ref[j], axis=0, keepdims=True))
            a_j = jnp.exp2(m[j] - m_j)
            p = jnp.exp2(s_ref[j] - m_j)
            l_new.append(a_j * l[j] + jnp.sum(p, axis=0, keepdims=True))
            p_ref[j, 0:TK, :] = p.astype(BF16)
            m_new.append(m_j)
            alpha.append(a_j)
        return m_new, l_new, alpha

    def values(c, p_ref, alpha):
        vc = v_chunk(pl.multiple_of(c * TK, TK), TK)
        for j in maps:
            acc_ref[j] = alpha[j] * acc_ref[j] + jnp.dot(vc, p_ref[j, 0:TK, :],
                                                         preferred_element_type=F32)

    acc_ref[...] = jnp.zeros_like(acc_ref)
    p_refs[1][...] = jnp.zeros_like(p_refs[1])
    scores(0, s_refs[0])
    stat = lambda v: [jnp.full((1, TQ), v, F32) for _ in maps]

    def chunk_pair(i, carry):
        m, l, alpha_b = carry
        c0 = 2 * i
        scores(c0 + 1, s_refs[1])
        m, l, alpha_a = softmax(s_refs[0], p_refs[0], m, l)
        values(jnp.maximum(c0 - 1, 0), p_refs[1], alpha_b)
        scores(jnp.minimum(c0 + 2, N_K_CHUNKS - 1), s_refs[0])
        m, l, alpha_b = softmax(s_refs[1], p_refs[1], m, l)
        values(c0, p_refs[0], alpha_a)
        return m, l, alpha_b

    _, l, alpha_b = lax.fori_loop(0, N_K_CHUNKS // 2, chunk_pair,
                                  (stat(NEG_BIG), stat(0.0), stat(1.0)))
    values(N_K_CHUNKS - 1, p_refs[1], alpha_b)
    return l


def _attend(ctx_only, k_ref, vt_at, weights, halves, scratch):
    if ctx_only:
        return _attend_ctx(k_ref[...], vt_at(0, CTX_LEN), weights)
    acc_ref, l_ref, kmax_ref, s0_ref, s1_ref, p0_ref, p1_ref = scratch
    maps = range(len(weights))

    @pl.when(pl.program_id(1) == 0)
    def _():
        _key_norm_bound(k_ref, kmax_ref)

    lane = lax.broadcasted_iota(jnp.int32, (1, LANES), 1)
    shift = []
    for j in maps:
        k2 = jnp.max(jnp.where(lane // HEAD_DIM == halves[j], kmax_ref[0:1, :], 0.0),
                     axis=-1, keepdims=True)
        wf = weights[j].astype(F32)
        q2 = jnp.sum(wf * wf, axis=0, keepdims=True)
        shift.append(jnp.sqrt(q2 * k2) * BOUND_SLACK - SHIFT_GUARD)

    l = _attend_lat_fast(k_ref, vt_at, weights, shift, acc_ref, (p0_ref, p1_ref))
    n_bad = jnp.zeros((), F32)
    for j in maps:
        l_ref[j:j + 1, :] = l[j]
        ok = jnp.logical_and(l[j] >= L_MIN, l[j] <= L_MAX)
        n_bad = n_bad + jnp.sum(jnp.where(ok, 0.0, 1.0))

    @pl.when(n_bad > 0.0)
    def _():
        l = _attend_lat_exact(k_ref, vt_at, weights, acc_ref, (s0_ref, s1_ref), (p0_ref, p1_ref))
        for j in maps:
            l_ref[j:j + 1, :] = l[j]

    return [acc_ref[j] * (1.0 / l_ref[j:j + 1, :]) for j in maps]


def _attn_call(kernel, ctx_only, n_heads, dv, in_blocks, operands, out_width, name):
    if ctx_only:
        tq, n_q, n_keys, q0, k0 = CTX_LEN, 1, CTX_LEN, SEQ // CTX_LEN, SEQ // CTX_LEN
        scratch = []
    else:
        tq, n_q, n_keys, q0, k0 = TQ, SEQ // TQ, T_ALL, 0, 0
        scratch = [pltpu.VMEM((2, dv, TQ), F32), pltpu.VMEM((8, TQ), F32),
                   pltpu.VMEM((8, LANES), F32),
                   pltpu.VMEM((2, TK, TQ), F32), pltpu.VMEM((2, TK, TQ), F32),
                   pltpu.VMEM((2, TK_FAST, TQ), BF16), pltpu.VMEM((2, TK_FAST, TQ), BF16)]
    return pl.pallas_call(
        functools.partial(kernel, ctx_only),
        grid=(n_heads, n_q),
        in_specs=in_blocks(tq, n_keys, q0, k0),
        out_specs=pl.BlockSpec((tq, LANES), lambda h, i: (i, h)),
        out_shape=jax.ShapeDtypeStruct((n_q * tq, out_width), BF16),
        scratch_shapes=scratch,
        compiler_params=_params(2),
        name=name + ("_ctx" if ctx_only else ""),
    )(*operands)


def _da_kernel(lam_init, ctx_only, qt_ref, k_ref, vt_ref, lam_ref, g_ref, o_ref, *scratch):
    qt = qt_ref[0]
    row = lax.broadcasted_iota(jnp.int32, qt.shape, 0)
    zero = jnp.zeros_like(qt)
    weights = [jnp.where(row < HEAD_DIM, qt, zero), jnp.where(row >= HEAD_DIM, qt, zero)]
    o1, o2 = _attend(ctx_only, k_ref, lambda s, n: vt_ref[0, :, pl.ds(s, n)], weights, (0, 1),
                     scratch)
    lf = lam_ref[...]
    lam = (jnp.exp(jnp.sum(lf[0:1] * lf[1:2], axis=-1, keepdims=True))
           - jnp.exp(jnp.sum(lf[2:3] * lf[3:4], axis=-1, keepdims=True)) + lam_init)
    o = o1 - lam * o2
    ms = jnp.mean(o * o, axis=0, keepdims=True)
    on = (o * lax.rsqrt(ms + EPS)) * g_ref[...] * (1.0 - lam_init)
    o_ref[...] = on.T.astype(o_ref.dtype)


def _da_attention(qt, k, vt, da_lam, subln_g, lam_init, ctx_only):
    def in_blocks(tq, n_keys, q0, k0):
        return [
            pl.BlockSpec((1, LANES, tq), lambda h, i: (h, 0, i + q0)),
            pl.BlockSpec((n_keys, LANES), lambda h, i: (k0, h)),
            pl.BlockSpec((1, LANES, n_keys), lambda h, i: (h, 0, k0)),
            pl.BlockSpec((4, HEAD_DIM), lambda h, i: (0, 0)),
            pl.BlockSpec((LANES, 1), lambda h, i: (0, 0)),
        ]
    return _attn_call(functools.partial(_da_kernel, lam_init), ctx_only, DA_HEADS, LANES,
                      in_blocks, (qt, k, vt, da_lam, subln_g), DA_HEADS * LANES, "da_attention")


def _gqa_kernel(ctx_only, qt_ref, k_ref, vt_ref, o_ref, *scratch):
    g = pl.program_id(0)
    qt = qt_ref[0]
    zero = jnp.zeros((HEAD_DIM, qt.shape[1]), qt.dtype)
    weights = []
    for j in range(2):
        qj = qt[j * HEAD_DIM:(j + 1) * HEAD_DIM]
        weights.append(jnp.where(g == 0, jnp.concatenate([qj, zero], axis=0),
                                 jnp.concatenate([zero, qj], axis=0)))
    def vt_at(s, n):
        v = vt_ref[:, pl.ds(s, n)]
        return jnp.concatenate([v, jnp.zeros_like(v)], axis=0)

    o0, o1 = _attend(ctx_only, k_ref, vt_at, weights, (g, g), scratch)
    o_ref[...] = jnp.concatenate([o0[0:HEAD_DIM], o1[0:HEAD_DIM]], axis=0).T.astype(o_ref.dtype)


def _gqa_attention(qt, k, vt, ctx_only):
    def in_blocks(tq, n_keys, q0, k0):
        return [
            pl.BlockSpec((1, LANES, tq), lambda g, i: (g, 0, i + q0)),
            pl.BlockSpec((n_keys, LANES), lambda g, i: (k0, 0)),
            pl.BlockSpec((HEAD_DIM, n_keys), lambda g, i: (g, k0)),
        ]
    return _attn_call(_gqa_kernel, ctx_only, GQA_KV_HEADS, LANES, in_blocks, (qt, k, vt),
                      GQA_KV_HEADS * LANES, "gqa_attention")


def _rope_tables():
    half = HEAD_DIM // 4
    t = jnp.arange(SEQ, dtype=jnp.int32)
    row = (t // GRID_W).astype(F32)
    col = (t % GRID_W).astype(F32)
    inv = ROPE_THETA ** (-jnp.arange(half, dtype=F32) / half)
    ang_r = row[:, None] * inv
    ang_c = col[:, None] * inv
    cos64 = jnp.concatenate([jnp.cos(ang_r)] * 2 + [jnp.cos(ang_c)] * 2, axis=-1)
    sin64 = jnp.concatenate([-jnp.sin(ang_r), jnp.sin(ang_r), -jnp.sin(ang_c), jnp.sin(ang_c)],
                            axis=-1)
    n_rest = T_PAD - SEQ
    cos_t = jnp.concatenate([jnp.tile(cos64, (1, 2)), jnp.ones((n_rest, LANES), F32)], axis=0)
    sin_t = jnp.concatenate([jnp.tile(sin64, (1, 2)), jnp.zeros((n_rest, LANES), F32)], axis=0)
    return cos_t, sin_t


def _block_diag(w):
    eye = jnp.eye(LRU_BLOCKS, dtype=w.dtype)
    full = jnp.einsum('dkij,kl->dkilj', w, eye)
    return full.reshape(2, LRU_WIDTH, LRU_WIDTH).astype(BF16)


def kernel(x, c, ctx, c_ctx, ada_w, ada_b, norm_g, ffn1_w13, ffn1_w2, ffn2_w13, ffn2_w2,
           w_in, w_out, da_lam, da_subln_g, qk_norm_g, lru_conv_w, lru_conv_b,
           lru_wa, lru_ba, lru_wi, lru_bi, lru_lambda, final_g):
    assert x.shape == (1, SEQ, D_MODEL) and ctx.shape == (1, CTX_LEN, D_MODEL)
    cc = jnp.zeros((8, D_MODEL), F32).at[0].set(c[0]).at[1].set(c_ctx)
    mods_all = _ada_mods(cc, ada_w, ada_b).reshape(DEPTH, 8, N_MOD, D_MODEL)[:, :2]
    cos_t, sin_t = _rope_tables()
    ffn1_w13, ffn1_w2, ffn2_w13, ffn2_w2, w_in, w_out = (
        w.astype(BF16) for w in (ffn1_w13, ffn1_w2, ffn2_w13, ffn2_w2, w_in, w_out))

    def stream(lat, ctx_rows):
        rest = jnp.zeros((T_PAD - T_ALL, lat.shape[1]), lat.dtype)
        return jnp.concatenate([lat, ctx_rows, rest], axis=0)

    h = stream(x[0], ctx[0])
    for l in range(DEPTH):
        last = l == DEPTH - 1
        lam_init = 0.8 - 0.6 * math.exp(-0.3 * l)
        mods = mods_all[l]

        h = _ffn(h, mods, norm_g[l, 0:1], ffn1_w13, ffn1_w2, l, 0, N_TILES)

        qk_gain = jnp.tile(qk_norm_g[l], (1, 2))
        qt_da, k_da, vt_da, qt_g, k_g, vt_g, xb, gb = _inproj(
            h, mods, norm_g[l, 1:2], w_in, l, cos_t, sin_t, qk_gain)

        hs = _lru(xb, lru_conv_w[l], lru_conv_b[l][None, :], _block_diag(lru_wa[l]),
                  _block_diag(lru_wi[l]), lru_ba[l][:, None, :], lru_bi[l][:, None, :],
                  lru_lambda[l][:, None, :])

        n_tiles = N_LAT_TILES if last else N_TILES
        da_args = (qt_da, k_da, vt_da, da_lam[l], da_subln_g[l][:, None], lam_init)
        a = _da_attention(*da_args, False)
        g = _gqa_attention(qt_g, k_g, vt_g, False)
        if not last:
            a = stream(a, _da_attention(*da_args, True))
            g = stream(g, _gqa_attention(qt_g, k_g, vt_g, True))

        h = _ffn(h, mods, norm_g[l, 2:3], ffn2_w13, ffn2_w2, l, 6, n_tiles,
                 mixer=(a, g, hs, gb, w_out), final_g=final_g[None, :] if last else None)
    return h[None]
```
